```python
import jax, jax.numpy as jnp
from jax import lax
import numpy as np

D_MODEL = 2048
BATCH = 2
SEQ = 8192
DEPTH = 2

GRID_W = 64
CTX_LEN = 256
N_HEADS = 32
N_KV_HEADS = 4
HEAD_DIM = 64
GROUPS = N_HEADS // N_KV_HEADS
WINDOW = 128
BLOCK = 128
ROPE_THETA = 10000.0
D_CONV = D_MODEL
CONV_WIDTH = 3
D_FF = ((8 * D_MODEL // 3 + 255) // 256) * 256
Q_W = N_HEADS * HEAD_DIM
KV_W = N_KV_HEADS * HEAD_DIM
IN_SIZES = (Q_W, KV_W, KV_W, D_CONV, D_CONV, D_CONV, D_MODEL, D_MODEL)
IN_COLS = sum(IN_SIZES)
IN_OFFSETS = tuple(int(o) for o in np.cumsum(IN_SIZES)[:-1])
EPS = 1e-6
NEG = -1e30

kernel_name = "hybrid_conv_swa_dit_block"


def rmsnorm(x, g):
    xf = x.astype(jnp.float32)
    y = xf * lax.rsqrt(jnp.mean(xf * xf, axis=-1, keepdims=True) + EPS)
    return y.astype(x.dtype) * g


def modulate(x, g, shift, scale):
    return rmsnorm(x, g) * (1.0 + scale) + shift


def ada_split(cvec, w, b):
    m = jax.nn.silu(cvec) @ w + b
    return jnp.split(m, 6, axis=-1)


def rope_1d(x, pos):
    half = x.shape[-1] // 2
    freqs = ROPE_THETA ** (-jnp.arange(half, dtype=jnp.float32) / half)
    ang = pos[:, None] * freqs[None, :]
    cos = jnp.cos(ang)[None, :, None, :].astype(x.dtype)
    sin = jnp.sin(ang)[None, :, None, :].astype(x.dtype)
    x1, x2 = x[..., :half], x[..., half:]
    return jnp.concatenate([x1 * cos - x2 * sin, x2 * cos + x1 * sin], axis=-1)


def axial_rope(x, row, col):
    d = x.shape[-1] // 2
    return jnp.concatenate([rope_1d(x[..., :d], row), rope_1d(x[..., d:], col)], axis=-1)


def softmax_with_sink(logits, sink):
    s = jnp.broadcast_to(sink.astype(jnp.float32).reshape(N_KV_HEADS, GROUPS)[None, :, :, None, None],
                         logits.shape[:-1] + (1,))
    p = jax.nn.softmax(jnp.concatenate([logits, s], axis=-1), axis=-1)
    return p[..., :-1]


def context_attention(q, k, v, sink):
    b, n = q.shape[0], q.shape[1]
    qg = q.reshape(b, n, N_KV_HEADS, GROUPS, HEAD_DIM)
    s = jnp.einsum('bqkgd,bskd->bkgqs', qg, k).astype(jnp.float32) * (HEAD_DIM ** -0.5)
    p = softmax_with_sink(s, sink).astype(v.dtype)
    o = jnp.einsum('bkgqs,bskd->bqkgd', p, v)
    return o.reshape(b, n, Q_W)


def latent_window_attention(q, k, v, kc, vc, sink):
    b, s_len = q.shape[0], q.shape[1]
    nb = s_len // BLOCK
    qb = q.reshape(b, nb, BLOCK, N_KV_HEADS, GROUPS, HEAD_DIM).transpose(1, 0, 2, 3, 4, 5)

    def band(t):
        tp = jnp.pad(t, ((0, 0), (BLOCK, BLOCK), (0, 0), (0, 0)))
        tp = tp.reshape(b, nb + 2, BLOCK, N_KV_HEADS, HEAD_DIM)
        tb = jnp.concatenate([tp[:, :-2], tp[:, 1:-1], tp[:, 2:]], axis=2)
        return tb.transpose(1, 0, 2, 3, 4)

    kb, vb = band(k), band(v)
    bids = jnp.arange(nb, dtype=jnp.int32)
    scale = HEAD_DIM ** -0.5

    def one_block(args):
        q_blk, k_blk, v_blk, bid = args
        s_loc = jnp.einsum('bqkgd,bskd->bkgqs', q_blk, k_blk).astype(jnp.float32) * scale
        s_ctx = jnp.einsum('bqkgd,bskd->bkgqs', q_blk, kc).astype(jnp.float32) * scale
        qpos = bid * BLOCK + jnp.arange(BLOCK, dtype=jnp.int32)
        kpos = (bid - 1) * BLOCK + jnp.arange(3 * BLOCK, dtype=jnp.int32)
        valid = (jnp.abs(kpos[None, :] - qpos[:, None]) <= WINDOW) & (kpos[None, :] >= 0) & (kpos[None, :] < s_len)
        s_loc = jnp.where(valid[None, None, None], s_loc, NEG)
        p = softmax_with_sink(jnp.concatenate([s_ctx, s_loc], axis=-1), sink).astype(v_blk.dtype)
        n_ctx = kc.shape[1]
        o = jnp.einsum('bkgqs,bskd->bqkgd', p[..., :n_ctx], vc) + \
            jnp.einsum('bkgqs,bskd->bqkgd', p[..., n_ctx:], v_blk)
        return o

    o = lax.map(one_block, (qb, kb, vb, bids))
    return o.transpose(1, 0, 2, 3, 4, 5).reshape(b, s_len, Q_W)


def short_conv(u, w, bias):
    up = jnp.pad(u, ((0, 0), (1, 1), (0, 0)))
    return up[:, :-2] * w[0] + up[:, 1:-1] * w[1] + up[:, 2:] * w[2] + bias


def merge_branches(attn, cb, cc, cx, ga, gc, conv_w, conv_b, w_attn_out, w_conv_out, w_o):
    attn_branch = attn @ w_attn_out
    conv_branch = (cb * short_conv(cc * cx, conv_w, conv_b)) @ w_conv_out
    m = jax.nn.sigmoid(ga) * attn_branch + jax.nn.sigmoid(gc) * conv_branch
    return m @ w_o


def swiglu(h, w_in, w_out):
    gate, up = jnp.split(h @ w_in, 2, axis=-1)
    return (jax.nn.silu(gate) * up) @ w_out


def heads(t, n):
    return t.reshape(t.shape[0], t.shape[1], n, HEAD_DIM)


def setup_inputs(seed: int = 0) -> dict:
    key = jax.random.key(seed)
    ks = jax.random.split(key, 20)
    f32 = jnp.float32

    def nrm(k, shape, scale):
        return jax.random.normal(k, shape, f32) * scale

    return {
        "x": nrm(ks[0], (BATCH, SEQ, D_MODEL), 1.0),
        "c": nrm(ks[1], (BATCH, D_MODEL), 1.0),
        "ctx": nrm(ks[2], (BATCH, CTX_LEN, D_MODEL), 1.0),
        "c_ctx": nrm(ks[3], (D_MODEL,), 1.0),
        "ada_w": nrm(ks[4], (DEPTH, D_MODEL, 6 * D_MODEL), 0.5 * D_MODEL ** -0.5),
        "ada_b": nrm(ks[5], (DEPTH, 6 * D_MODEL), 0.01),
        "norm1_g": 1.0 + nrm(ks[6], (DEPTH, D_MODEL), 0.01),
        "norm2_g": 1.0 + nrm(ks[7], (DEPTH, D_MODEL), 0.01),
        "w_in": nrm(ks[8], (DEPTH, D_MODEL, IN_COLS), D_MODEL ** -0.5),
        "conv_w": nrm(ks[9], (DEPTH, CONV_WIDTH, D_CONV), CONV_WIDTH ** -0.5),
        "conv_b": nrm(ks[10], (DEPTH, D_CONV), 0.01),
        "sink": nrm(ks[11], (DEPTH, N_HEADS), 0.5),
        "w_attn_out": nrm(ks[12], (DEPTH, Q_W, D_MODEL), Q_W ** -0.5),
        "w_conv_out": nrm(ks[13], (DEPTH, D_CONV, D_MODEL), D_CONV ** -0.5),
        "w_o": nrm(ks[14], (DEPTH, D_MODEL, D_MODEL), D_MODEL ** -0.5),
        "w_ffn_in": nrm(ks[15], (DEPTH, D_MODEL, 2 * D_FF), D_MODEL ** -0.5),
        "w_ffn_out": nrm(ks[16], (DEPTH, D_FF, D_MODEL), D_FF ** -0.5),
        "final_g": 1.0 + nrm(ks[17], (D_MODEL,), 0.01),
    }


def reference(x, c, ctx, c_ctx, ada_w, ada_b, norm1_g, norm2_g, w_in, conv_w, conv_b, sink,
              w_attn_out, w_conv_out, w_o, w_ffn_in, w_ffn_out, final_g):
    n_lat = x.shape[1]
    ROWS = n_lat // GRID_W
    grid_r, grid_c = jnp.meshgrid(jnp.arange(ROWS, dtype=jnp.int32), jnp.arange(GRID_W, dtype=jnp.int32), indexing='ij')
    row = grid_r.reshape(-1).astype(jnp.float32)
    col = grid_c.reshape(-1).astype(jnp.float32)

    xc = ctx
    for l in range(DEPTH):
        last = l == DEPTH - 1
        sh1, sc1, g1, sh2, sc2, g2 = [t[:, None, :] for t in ada_split(c, ada_w[l], ada_b[l])]
        csh1, csc1, cg1, csh2, csc2, cg2 = ada_split(c_ctx, ada_w[l], ada_b[l])

        h = modulate(x, norm1_g[l], sh1, sc1)
        q, k, v, cb, cc, cx, ga, gc = jnp.split(h @ w_in[l], IN_OFFSETS, axis=-1)
        q = axial_rope(heads(q, N_HEADS), row, col)
        k = axial_rope(heads(k, N_KV_HEADS), row, col)
        v = heads(v, N_KV_HEADS)

        hc = modulate(xc, norm1_g[l], csh1, csc1)
        if last:
            kc_, vc_ = jnp.split(hc @ w_in[l][:, Q_W:Q_W + 2 * KV_W], 2, axis=-1)
        else:
            qc_, kc_, vc_, cbc, ccc, cxc, gac, gcc = jnp.split(hc @ w_in[l], IN_OFFSETS, axis=-1)
        kc = heads(kc_, N_KV_HEADS)
        vc = heads(vc_, N_KV_HEADS)

        attn = latent_window_attention(q, k, v, kc, vc, sink[l])
        x = x + g1 * merge_branches(attn, cb, cc, cx, ga, gc, conv_w[l], conv_b[l],
                                    w_attn_out[l], w_conv_out[l], w_o[l])

        if not last:
            attn_c = context_attention(heads(qc_, N_HEADS), kc, vc, sink[l])
            xc = xc + cg1 * merge_branches(attn_c, cbc, ccc, cxc, gac, gcc, conv_w[l], conv_b[l],
                                           w_attn_out[l], w_conv_out[l], w_o[l])
            xc = xc + cg2 * swiglu(modulate(xc, norm2_g[l], csh2, csc2), w_ffn_in[l], w_ffn_out[l])

        x = x + g2 * swiglu(modulate(x, norm2_g[l], sh2, sc2), w_ffn_in[l], w_ffn_out[l])

    return rmsnorm(x, final_g)
```

```python
import functools

import jax
import jax.numpy as jnp
import numpy as np
from jax import lax
from jax.experimental import pallas as pl
from jax.experimental.pallas import tpu as pltpu

GRID_W = 64
N_HEADS = 32
N_KV_HEADS = 4
HEAD_DIM = 64
WINDOW = 128
ROPE_THETA = 10000.0
EPS = 1e-6
NEG = -1e30

LANES = 128
BF16_ROWS = 16
V7X_VMEM_BYTES = 64 * 1024 * 1024
VMEM_LIMIT_BYTES = V7X_VMEM_BYTES * 7 // 8
ATT_BLOCK = 128
MOD_ROWS = 8

F32 = jnp.float32
BF16 = jnp.bfloat16


def _params(*sem):
    return pltpu.CompilerParams(dimension_semantics=sem, vmem_limit_bytes=VMEM_LIMIT_BYTES)


def _sigmoid(v):
    return 1.0 / (1.0 + jnp.exp(-v))


def _norm_mod(x, g, shift, scale):
    y = x * lax.rsqrt(jnp.mean(x * x, axis=-1, keepdims=True) + EPS)
    return (y * g) * (1.0 + scale) + shift


def _dot(a, b):
    return jnp.dot(a, b, preferred_element_type=F32)


def _dot_nt(a, b):
    return lax.dot_general(a, b, (((1,), (1,)), ((), ())), preferred_element_type=F32)


def _ada_kernel(c_ref, w_ref, b_ref, o_ref):
    c = c_ref[...]
    s = c * _sigmoid(c)
    o_ref[...] = _dot(s.astype(BF16), w_ref[...].astype(BF16)) + b_ref[...]


def _ada(cc, ada_w, ada_b):
    depth, d, n = ada_w.shape
    tn = _tile(n, 1024)
    return pl.pallas_call(
        _ada_kernel,
        grid=(depth, n // tn),
        in_specs=[
            pl.BlockSpec((MOD_ROWS, d), lambda l, j: (0, 0)),
            pl.BlockSpec((None, d, tn), lambda l, j: (l, 0, j)),
            pl.BlockSpec((None, 1, tn), lambda l, j: (l, 0, j)),
        ],
        out_specs=pl.BlockSpec((None, MOD_ROWS, tn), lambda l, j: (l, 0, j)),
        out_shape=jax.ShapeDtypeStruct((depth, MOD_ROWS, n), F32),
        compiler_params=_params("parallel", "parallel"),
        name="ada_mod",
    )(cc, ada_w, ada_b.reshape(depth, 1, n))


def _mod_spec(layer, which, row_of_tile):
    def index(i, j):
        return (layer, row_of_tile(i), which, 0, 0)
    return index


def _rope(v, cos, sin_a, sin_b):
    return v * cos + pltpu.roll(v, LANES - 16, 1) * sin_a + pltpu.roll(v, 16, 1) * sin_b


def _inproj_kernel(*refs, nq, use_rope, kv_only, tn):
    if use_rope:
        x_ref, g_ref, sh_ref, sc_ref, w_ref, cos_ref, sa_ref, sb_ref = refs[:8]
        rest = refs[8:]
    else:
        x_ref, g_ref, sh_ref, sc_ref, w_ref = refs[:5]
        rest = refs[5:]
    if kv_only:
        kvx_ref, h_scr = rest
    else:
        q_ref, kvx_ref, gate_ref, h_scr = rest
    j = pl.program_id(1)

    @pl.when(j == 0)
    def _():
        h_scr[...] = _norm_mod(x_ref[...], g_ref[...], sh_ref[...], sc_ref[...]).astype(BF16)

    def rope(v):
        if not use_rope:
            return v
        return _rope(v, cos_ref[...], sa_ref[...], sb_ref[...])

    def kv_epilogue():
        acc = _dot(h_scr[...], w_ref[...])
        kvw = tn // 2
        lo_mask = lax.broadcasted_iota(jnp.int32, (acc.shape[0], LANES), 1) < HEAD_DIM
        for part in range(2):
            for pr in range(kvw // LANES):
                v = acc[:, part * kvw + pr * LANES: part * kvw + (pr + 1) * LANES]
                if part == 0:
                    v = rope(v)
                r = pltpu.roll(v, HEAD_DIM, 1)
                tiles = ((jnp.where(lo_mask, v, 0.0), jnp.where(lo_mask, 0.0, r)),
                         (jnp.where(lo_mask, r, 0.0), jnp.where(lo_mask, 0.0, v)))
                for odd in range(2):
                    head = 2 * pr + odd
                    for hi in range(2):
                        col = head * 4 * LANES + (2 * part + hi) * LANES
                        kvx_ref[:, col:col + LANES] = tiles[odd][hi].astype(BF16)

    if kv_only:
        kv_epilogue()
        return

    @pl.when(j < nq)
    def _():
        acc = _dot(h_scr[...], w_ref[...])
        for cg in range(tn // LANES):
            v = rope(acc[:, cg * LANES:(cg + 1) * LANES]) * (HEAD_DIM ** -0.5)
            q_ref[:, cg * LANES:(cg + 1) * LANES] = v.astype(BF16)

    @pl.when(j == nq)
    def _():
        kv_epilogue()

    @pl.when(j > nq)
    def _():
        gate_ref[...] = _dot(h_scr[...], w_ref[...])


def _inproj(xr, norm_g, mods, w_bf, layer, *, tm, row_of_tile, rope_tabs, seq, kv_only):
    m, d = xr.shape
    q_w = N_HEADS * HEAD_DIM
    kv_w = N_KV_HEADS * HEAD_DIM
    tn = 2 * kv_w
    nq = q_w // tn
    gate_off = (q_w + 2 * kv_w + 3 * d) // tn
    ng = 2 * d // tn
    use_rope = rope_tabs is not None
    kvx_w = N_KV_HEADS * 4 * LANES

    if kv_only:
        grid = (m // tm, 1)
        w_index = lambda i, j: (0, nq)
    else:
        grid = (m // tm, nq + 1 + ng)
        w_index = lambda i, j: (0, jnp.where(j <= nq, j, j + gate_off - nq - 1))

    in_specs = [
        pl.BlockSpec((tm, d), lambda i, j: (i, 0)),
        pl.BlockSpec((None, 1, d), lambda i, j: (layer, 0, 0)),
        pl.BlockSpec((None, None, None, 1, d), _mod_spec(layer, 0, row_of_tile)),
        pl.BlockSpec((None, None, None, 1, d), _mod_spec(layer, 1, row_of_tile)),
        pl.BlockSpec((None, d, tn), lambda i, j: (layer,) + w_index(i, j)),
    ]
    args = [xr, norm_g, mods, mods, w_bf]
    if use_rope:
        tps = seq // tm
        for t in rope_tabs:
            in_specs.append(pl.BlockSpec((tm, LANES), lambda i, j: (i % tps, 0)))
            args.append(t)

    kvx_spec = pl.BlockSpec((tm, kvx_w), lambda i, j: (i, 0))
    kvx_shape = jax.ShapeDtypeStruct((m, kvx_w), BF16)
    if kv_only:
        out_specs, out_shape = kvx_spec, kvx_shape
    else:
        out_specs = [
            pl.BlockSpec((tm, tn), lambda i, j: (i, jnp.minimum(j, nq - 1))),
            kvx_spec,
            pl.BlockSpec((tm, tn), lambda i, j: (i, jnp.clip(j - nq - 1, 0, ng - 1))),
        ]
        out_shape = [jax.ShapeDtypeStruct((m, q_w), BF16), kvx_shape,
                     jax.ShapeDtypeStruct((m, 2 * d), F32)]
    return pl.pallas_call(
        functools.partial(_inproj_kernel, nq=nq, use_rope=use_rope, kv_only=kv_only, tn=tn),
        grid=grid,
        in_specs=in_specs,
        out_specs=out_specs,
        out_shape=out_shape,
        scratch_shapes=[pltpu.VMEM((tm, d), BF16)],
        compiler_params=_params("parallel", "arbitrary"),
        name="inproj_kv" if kv_only else "inproj",
    )(*args)


def _conv_kernel(xp_ref, x_ref, xn_ref, g_ref, sh_ref, sc_ref, wb_ref, wc_ref, wx_ref,
                 cw_ref, cb_ref, y_ref, h_scr, *, tm, seq):
    i = pl.program_id(0)
    j = pl.program_id(1)
    halo = BF16_ROWS

    @pl.when(j == 0)
    def _():
        g, sh, sc = g_ref[...], sh_ref[...], sc_ref[...]
        h_scr[0:halo] = _norm_mod(xp_ref[...], g, sh, sc).astype(BF16)
        h_scr[halo:halo + tm] = _norm_mod(x_ref[...], g, sh, sc).astype(BF16)
        h_scr[halo + tm:] = _norm_mod(xn_ref[...], g, sh, sc).astype(BF16)

    hx = h_scr[...]
    u = _dot(hx, wc_ref[...]) * _dot(hx, wx_ref[...])
    rows = tm + 2 * halo
    u_up = pltpu.roll(u, 1, 0)[halo:halo + tm]
    u_dn = pltpu.roll(u, rows - 1, 0)[halo:halo + tm]
    row = lax.broadcasted_iota(jnp.int32, (tm, 1), 0)
    seq_start = (i * tm) % seq == 0
    seq_end = ((i + 1) * tm) % seq == 0
    u_up = jnp.where(jnp.logical_and(row == 0, seq_start), 0.0, u_up)
    u_dn = jnp.where(jnp.logical_and(row == tm - 1, seq_end), 0.0, u_dn)
    w = cw_ref[...]
    conv = u_up * w[0:1] + u[halo:halo + tm] * w[1:2] + u_dn * w[2:3] + cb_ref[...]
    y_ref[...] = (_dot(h_scr[halo:halo + tm], wb_ref[...]) * conv).astype(BF16)


def _conv_branch(xr, norm_g, mods, w_bf, conv_w, conv_b, layer, *, tm, tn, row_of_tile, seq):
    m, d = xr.shape
    d_conv = conv_w.shape[-1]
    halo = BF16_ROWS
    off = (N_HEADS * HEAD_DIM + 2 * N_KV_HEADS * HEAD_DIM) // tn
    nc = d_conv // tn
    hb = tm // halo
    last_hb = m // halo - 1

    def wspec(k):
        return pl.BlockSpec((None, d, tn), lambda i, j: (layer, 0, off + k * nc + j))

    return pl.pallas_call(
        functools.partial(_conv_kernel, tm=tm, seq=seq),
        grid=(m // tm, nc),
        in_specs=[
            pl.BlockSpec((halo, d), lambda i, j: (jnp.maximum(i * hb - 1, 0), 0)),
            pl.BlockSpec((tm, d), lambda i, j: (i, 0)),
            pl.BlockSpec((halo, d), lambda i, j: (jnp.minimum((i + 1) * hb, last_hb), 0)),
            pl.BlockSpec((None, 1, d), lambda i, j: (layer, 0, 0)),
            pl.BlockSpec((None, None, None, 1, d), _mod_spec(layer, 0, row_of_tile)),
            pl.BlockSpec((None, None, None, 1, d), _mod_spec(layer, 1, row_of_tile)),
            wspec(0), wspec(1), wspec(2),
            pl.BlockSpec((None, 3, tn), lambda i, j: (layer, 0, j)),
            pl.BlockSpec((None, 1, tn), lambda i, j: (layer, 0, j)),
        ],
        out_specs=pl.BlockSpec((tm, tn), lambda i, j: (i, j)),
        out_shape=jax.ShapeDtypeStruct((m, d_conv), BF16),
        scratch_shapes=[pltpu.VMEM((tm + 2 * halo, d), BF16)],
        compiler_params=_params("parallel", "arbitrary"),
        name="conv_branch",
    )(xr, xr, xr, norm_g, mods, mods, w_bf, w_bf, w_bf, conv_w, conv_b)


def _pair_attention(qp, pieces, sink_e, sink_o):
    logits = []
    for kvx, bias in pieces:
        s_e = _dot_nt(qp, kvx[:, 0:LANES])
        s_o = _dot_nt(qp, kvx[:, LANES:2 * LANES])
        if bias is not None:
            s_e = s_e + bias
            s_o = s_o + bias
        logits.append((s_e, s_o))
    out = None
    inv = []
    for par, sink in ((0, sink_e), (1, sink_o)):
        mx = None
        for lg in logits:
            pm = jnp.max(lg[par], axis=-1, keepdims=True)
            mx = pm if mx is None else jnp.maximum(mx, pm)
        mx = jnp.maximum(mx, sink)
        den = jnp.exp(sink - mx)
        for (kvx, _), lg in zip(pieces, logits):
            p = jnp.exp(lg[par] - mx)
            den = den + jnp.sum(p, axis=-1, keepdims=True)
            v = kvx[:, (2 + par) * LANES:(3 + par) * LANES]
            pv = _dot(p.astype(BF16), v)
            out = pv if out is None else out + pv
        inv.append(1.0 / den)
    lane = lax.broadcasted_iota(jnp.int32, out.shape, 1)
    return out * jnp.where(lane < HEAD_DIM, inv[0], inv[1])


def _attn_kernel(sink_ref, q_ref, kvp_ref, kvc_ref, kvn_ref, ctx_ref, o_ref, band_scr, *, tq, nblk_seq):
    i = pl.program_id(1)
    h = pl.program_id(2)
    blk = ATT_BLOCK
    groups = N_HEADS // N_KV_HEADS
    band_scr[0:blk] = kvp_ref[...]
    band_scr[blk:blk + tq] = kvc_ref[...]
    band_scr[blk + tq:] = kvn_ref[...]
    ctx = ctx_ref[...]
    r = lax.broadcasted_iota(jnp.int32, (blk, 3 * blk), 0)
    c = lax.broadcasted_iota(jnp.int32, (blk, 3 * blk), 1)
    base_bias = jnp.where(jnp.abs(c - blk - r) <= WINDOW, 0.0, NEG)

    def body(sb, carry):
        r0 = pl.multiple_of(sb * blk, blk)
        gblk = i * (tq // blk) + sb
        pen_first = jnp.where(gblk == 0, NEG, 0.0)
        pen_last = jnp.where(gblk == nblk_seq - 1, NEG, 0.0)
        bias = base_bias + jnp.where(c < blk, pen_first, 0.0) + jnp.where(c >= 2 * blk, pen_last, 0.0)
        band = band_scr[pl.ds(r0, 3 * blk), :]
        for jp in range(groups // 2):
            qp = q_ref[pl.ds(r0, blk), jp * LANES:(jp + 1) * LANES]
            o = _pair_attention(qp, ((ctx, None), (band, bias)),
                                sink_ref[h * groups + 2 * jp], sink_ref[h * groups + 2 * jp + 1])
            o_ref[pl.ds(r0, blk), jp * LANES:(jp + 1) * LANES] = o.astype(BF16)
        return carry

    lax.fori_loop(0, tq // blk, body, 0)


def _attention(q, kvx, kvx_ctx, sink, *, batch, seq, n_ctx, tq):
    m, q_w = q.shape
    blk = ATT_BLOCK
    hw = q_w // N_KV_HEADS
    kw = 4 * LANES
    nb = seq // blk
    tpb = tq // blk

    return pl.pallas_call(
        functools.partial(_attn_kernel, tq=tq, nblk_seq=nb),
        grid=(batch, seq // tq, N_KV_HEADS),
        in_specs=[
            pl.BlockSpec(memory_space=pltpu.SMEM),
            pl.BlockSpec((tq, hw), lambda b, i, h: (b * (seq // tq) + i, h)),
            pl.BlockSpec((blk, kw), lambda b, i, h: (b * nb + jnp.maximum(i * tpb - 1, 0), h)),
            pl.BlockSpec((tq, kw), lambda b, i, h: (b * (seq // tq) + i, h)),
            pl.BlockSpec((blk, kw), lambda b, i, h: (b * nb + jnp.minimum((i + 1) * tpb, nb - 1), h)),
            pl.BlockSpec((n_ctx, kw), lambda b, i, h: (b, h)),
        ],
        out_specs=pl.BlockSpec((tq, hw), lambda b, i, h: (b * (seq // tq) + i, h)),
        out_shape=jax.ShapeDtypeStruct((m, q_w), BF16),
        scratch_shapes=[pltpu.VMEM((tq + 2 * blk, kw), BF16)],
        compiler_params=_params("parallel", "parallel", "arbitrary"),
        name="band_attention",
    )(sink, q, kvx, kvx, kvx, kvx_ctx)


def _ctx_attn_kernel(sink_ref, q_ref, ctx_ref, o_ref, *, n_ctx):
    h = pl.program_id(1)
    groups = N_HEADS // N_KV_HEADS
    ctx = ctx_ref[...]
    for r0 in range(0, n_ctx, ATT_BLOCK):
        for jp in range(groups // 2):
            qp = q_ref[r0:r0 + ATT_BLOCK, jp * LANES:(jp + 1) * LANES]
            o = _pair_attention(qp, ((ctx, None),),
                                sink_ref[h * groups + 2 * jp], sink_ref[h * groups + 2 * jp + 1])
            o_ref[r0:r0 + ATT_BLOCK, jp * LANES:(jp + 1) * LANES] = o.astype(BF16)


def _ctx_attention(q, kvx_ctx, sink, *, batch, n_ctx):
    m, q_w = q.shape
    hw = q_w // N_KV_HEADS
    kw = 4 * LANES
    return pl.pallas_call(
        functools.partial(_ctx_attn_kernel, n_ctx=n_ctx),
        grid=(batch, N_KV_HEADS),
        in_specs=[
            pl.BlockSpec(memory_space=pltpu.SMEM),
            pl.BlockSpec((n_ctx, hw), lambda b, h: (b, h)),
            pl.BlockSpec((n_ctx, kw), lambda b, h: (b, h)),
        ],
        out_specs=pl.BlockSpec((n_ctx, hw), lambda b, h: (b, h)),
        out_shape=jax.ShapeDtypeStruct((m, q_w), BF16),
        compiler_params=_params("parallel", "arbitrary"),
        name="ctx_attention",
    )(sink, q, kvx_ctx)


def _merge_kernel(a_ref, y_ref, ga_ref, gc_ref, wa_ref, wc_ref, m_ref):
    m = (_sigmoid(ga_ref[...]) * _dot(a_ref[...], wa_ref[...])
         + _sigmoid(gc_ref[...]) * _dot(y_ref[...], wc_ref[...]))
    m_ref[...] = m.astype(BF16)


def _merge(attn, y, gates, wa_bf, wc_bf, layer, *, tm, tn):
    m, q_w = attn.shape
    d_conv = y.shape[1]
    d = wa_bf.shape[-1]
    nn = d // tn
    return pl.pallas_call(
        _merge_kernel,
        grid=(m // tm, nn),
        in_specs=[
            pl.BlockSpec((tm, q_w), lambda i, j: (i, 0)),
            pl.BlockSpec((tm, d_conv), lambda i, j: (i, 0)),
            pl.BlockSpec((tm, tn), lambda i, j: (i, j)),
            pl.BlockSpec((tm, tn), lambda i, j: (i, nn + j)),
            pl.BlockSpec((None, q_w, tn), lambda i, j: (layer, 0, j)),
            pl.BlockSpec((None, d_conv, tn), lambda i, j: (layer, 0, j)),
        ],
        out_specs=pl.BlockSpec((tm, tn), lambda i, j: (i, j)),
        out_shape=jax.ShapeDtypeStruct((m, d), BF16),
        compiler_params=_params("parallel", "arbitrary"),
        name="merge",
    )(attn, y, gates, gates, wa_bf, wc_bf)


def _outproj_kernel(m_ref, w_ref, x_ref, g_ref, o_ref):
    o_ref[...] = x_ref[...] + g_ref[...] * _dot(m_ref[...], w_ref[...])


def _outproj(mm, xr, mods, wo_bf, layer, *, tm, tn, row_of_tile):
    m, d = xr.shape
    return pl.pallas_call(
        _outproj_kernel,
        grid=(m // tm, d // tn),
        in_specs=[
            pl.BlockSpec((tm, d), lambda i, j: (i, 0)),
            pl.BlockSpec((None, d, tn), lambda i, j: (layer, 0, j)),
            pl.BlockSpec((tm, tn), lambda i, j: (i, j)),
            pl.BlockSpec((None, None, None, 1, tn), lambda i, j: (layer, row_of_tile(i), 2, 0, j)),
        ],
        out_specs=pl.BlockSpec((tm, tn), lambda i, j: (i, j)),
        out_shape=jax.ShapeDtypeStruct((m, d), F32),
        compiler_params=_params("parallel", "arbitrary"),
        name="outproj",
    )(mm, wo_bf, xr, mods)


def _ffn_kernel(*refs, final):
    if final:
        x_ref, g_ref, sh_ref, sc_ref, gt_ref, wg_ref, wu_ref, wo_ref, fg_ref, o_ref, h_scr = refs
    else:
        x_ref, g_ref, sh_ref, sc_ref, gt_ref, wg_ref, wu_ref, wo_ref, o_ref, h_scr = refs
    f = pl.program_id(1)
    nf = pl.num_programs(1)

    @pl.when(f == 0)
    def _():
        h_scr[...] = _norm_mod(x_ref[...], g_ref[...], sh_ref[...], sc_ref[...]).astype(BF16)

    def contrib():
        h = h_scr[...]
        gate = _dot(h, wg_ref[...])
        act = (gate * _sigmoid(gate) * _dot(h, wu_ref[...])).astype(BF16)
        return _dot(act, wo_ref[...])

    @pl.when(f == 0)
    def _():
        o_ref[...] = contrib()

    @pl.when(f > 0)
    def _():
        o_ref[...] += contrib()

    @pl.when(f == nf - 1)
    def _():
        xn = x_ref[...] + gt_ref[...] * o_ref[...]
        if final:
            xn = xn * lax.rsqrt(jnp.mean(xn * xn, axis=-1, keepdims=True) + EPS) * fg_ref[...]
        o_ref[...] = xn


def _ffn(xr, norm_g, mods, wi_bf, wo_bf, layer, *, tm, tf, row_of_tile, final_g):
    m, d = xr.shape
    d_ff = wo_bf.shape[1]
    nf = d_ff // tf
    final = final_g is not None
    in_specs = [
        pl.BlockSpec((tm, d), lambda i, f: (i, 0)),
        pl.BlockSpec((None, 1, d), lambda i, f: (layer, 0, 0)),
        pl.BlockSpec((None, None, None, 1, d), _mod_spec(layer, 3, row_of_tile)),
        pl.BlockSpec((None, None, None, 1, d), _mod_spec(layer, 4, row_of_tile)),
        pl.BlockSpec((None, None, None, 1, d), _mod_spec(layer, 5, row_of_tile)),
        pl.BlockSpec((None, d, tf), lambda i, f: (layer, 0, f)),
        pl.BlockSpec((None, d, tf), lambda i, f: (layer, 0, nf + f)),
        pl.BlockSpec((None, tf, d), lambda i, f: (layer, f, 0)),
    ]
    args = [xr, norm_g, mods, mods, mods, wi_bf, wi_bf, wo_bf]
    if final:
        in_specs.append(pl.BlockSpec((1, d), lambda i, f: (0, 0)))
        args.append(final_g)
    return pl.pallas_call(
        functools.partial(_ffn_kernel, final=final),
        grid=(m // tm, nf),
        in_specs=in_specs,
        out_specs=pl.BlockSpec((tm, d), lambda i, f: (i, 0)),
        out_shape=jax.ShapeDtypeStruct((m, d), F32),
        scratch_shapes=[pltpu.VMEM((tm, d), BF16)],
        compiler_params=_params("parallel", "arbitrary"),
        name="ffn_final" if final else "ffn",
    )(*args)


def _rope_tables(seq):
    quarter = HEAD_DIM // 4
    t = np.arange(seq)
    pos = np.stack([t // GRID_W, t % GRID_W], axis=1).astype(np.float32)
    d = np.arange(LANES) % HEAD_DIM
    axis = d // (HEAD_DIM // 2)
    first_half = (d // quarter) % 2 == 0
    freqs = jnp.asarray(ROPE_THETA, F32) ** (-jnp.arange(quarter, dtype=F32) / quarter)
    ang = jnp.asarray(pos)[:, axis] * freqs[d % quarter][None, :]
    cos, sin = jnp.cos(ang), jnp.sin(ang)
    fh = jnp.asarray(first_half)[None, :]
    return cos, jnp.where(fh, -sin, 0.0), jnp.where(fh, 0.0, sin)


def _tile(n, pref):
    return pref if n % pref == 0 else n


def kernel(x, c, ctx, c_ctx, ada_w, ada_b, norm1_g, norm2_g, w_in, conv_w, conv_b, sink,
           w_attn_out, w_conv_out, w_o, w_ffn_in, w_ffn_out, final_g):
    batch, seq, d = x.shape
    n_ctx = ctx.shape[1]
    depth = ada_w.shape[0]
    d_ff = w_ffn_out.shape[1]
    assert batch + 1 <= MOD_ROWS and seq % ATT_BLOCK == 0 and n_ctx % ATT_BLOCK == 0

    cc = jnp.zeros((MOD_ROWS, d), F32).at[:batch].set(c).at[batch].set(c_ctx)
    mods = _ada(cc, ada_w, ada_b).reshape(depth, MOD_ROWS, 6, 1, d)

    w_in_bf = w_in.astype(BF16)
    wa_bf = w_attn_out.astype(BF16)
    wc_bf = w_conv_out.astype(BF16)
    wo_bf = w_o.astype(BF16)
    wfi_bf = w_ffn_in.astype(BF16)
    wfo_bf = w_ffn_out.astype(BF16)
    n1 = norm1_g.reshape(depth, 1, d)
    n2 = norm2_g.reshape(depth, 1, d)
    conv_b3 = conv_b.reshape(depth, 1, -1)
    fg = final_g.reshape(1, d)
    rope_tabs = _rope_tables(seq)

    tm = _tile(seq, 1024)
    tm_ffn = _tile(seq, 512)
    tn = 2 * N_KV_HEADS * HEAD_DIM
    tf = _tile(d_ff, tn)
    tq = _tile(seq, 512)
    lat_row = lambda t: (lambda i: i // (seq // t))
    ctx_row = lambda i: batch

    xl = x.reshape(batch * seq, d)
    xc = ctx.reshape(batch * n_ctx, d)
    for l in range(depth):
        last = l == depth - 1
        ctx_kw = dict(tm=n_ctx, row_of_tile=ctx_row)
        if last:
            kvx_c = _inproj(xc, n1, mods, w_in_bf, l, rope_tabs=None, seq=n_ctx, kv_only=True, **ctx_kw)
        else:
            q_c, kvx_c, gates_c = _inproj(xc, n1, mods, w_in_bf, l, rope_tabs=None, seq=n_ctx,
                                          kv_only=False, **ctx_kw)
            y_c = _conv_branch(xc, n1, mods, w_in_bf, conv_w, conv_b3, l, tn=tn, seq=n_ctx, **ctx_kw)

        q, kvx, gates = _inproj(xl, n1, mods, w_in_bf, l, tm=tm, row_of_tile=lat_row(tm),
                                rope_tabs=rope_tabs, seq=seq, kv_only=False)
        y = _conv_branch(xl, n1, mods, w_in_bf, conv_w, conv_b3, l, tm=tm, tn=tn,
                         row_of_tile=lat_row(tm), seq=seq)
        attn = _attention(q, kvx, kvx_c, sink[l], batch=batch, seq=seq, n_ctx=n_ctx, tq=tq)
        mm = _merge(attn, y, gates, wa_bf, wc_bf, l, tm=tm, tn=tn)
        xl = _outproj(mm, xl, mods, wo_bf, l, tm=tm, tn=tn, row_of_tile=lat_row(tm))

        if not last:
            attn_c = _ctx_attention(q_c, kvx_c, sink[l], batch=batch, n_ctx=n_ctx)
            mm_c = _merge(attn_c, y_c, gates_c, wa_bf, wc_bf, l, tm=n_ctx, tn=tn)
            xc = _outproj(mm_c, xc, mods, wo_bf, l, tn=tn, **ctx_kw)
            xc = _ffn(xc, n2, mods, wfi_bf, wfo_bf, l, tf=tf, final_g=None, **ctx_kw)

        xl = _ffn(xl, n2, mods, wfi_bf, wfo_bf, l, tm=tm_ffn, tf=tf, row_of_tile=lat_row(tm_ffn),
                  final_g=fg if last else None)
    return xl.reshape(batch, seq, d)
```

```python
import functools

import jax
import jax.numpy as jnp
import numpy as np
from jax import lax
from jax.experimental import pallas as pl
from jax.experimental.pallas import tpu as pltpu

GRID_W = 64
N_HEADS = 32
N_KV_HEADS = 4
HEAD_DIM = 64
WINDOW = 128
ROPE_THETA = 10000.0
EPS = 1e-6
NEG = -1e30
LOG2E = 1.4426950408889634
Q_SCALE = HEAD_DIM ** -0.5 * LOG2E

LANES = 128
BF16_ROWS = 16
V7X_VMEM_BYTES = 64 * 1024 * 1024
VMEM_LIMIT_BYTES = V7X_VMEM_BYTES * 7 // 8
ATT_BLOCK = 128
MOD_ROWS = 8

F32 = jnp.float32
BF16 = jnp.bfloat16


def _params(*sem):
    return pltpu.CompilerParams(dimension_semantics=sem, vmem_limit_bytes=VMEM_LIMIT_BYTES)


def _sigmoid(v):
    return 1.0 / (1.0 + jnp.exp(-v))


def _norm_mod(x, g, shift, scale):
    y = x * lax.rsqrt(jnp.mean(x * x, axis=-1, keepdims=True) + EPS)
    return (y * g) * (1.0 + scale) + shift


def _dot(a, b):
    return jnp.dot(a, b, preferred_element_type=F32)


def _dot_nt(a, b):
    return lax.dot_general(a, b, (((1,), (1,)), ((), ())), preferred_element_type=F32)


def _ada_kernel(c_ref, w_ref, b_ref, o_ref):
    c = c_ref[...]
    s = c * _sigmoid(c)
    o_ref[...] = _dot(s.astype(BF16), w_ref[...].astype(BF16)) + b_ref[...]


def _ada(cc, ada_w, ada_b):
    depth, d, n = ada_w.shape
    tn = _tile(n, 1024)
    return pl.pallas_call(
        _ada_kernel,
        grid=(depth, n // tn),
        in_specs=[
            pl.BlockSpec((MOD_ROWS, d), lambda l, j: (0, 0)),
            pl.BlockSpec((None, d, tn), lambda l, j: (l, 0, j)),
            pl.BlockSpec((None, 1, tn), lambda l, j: (l, 0, j)),
        ],
        out_specs=pl.BlockSpec((None, MOD_ROWS, tn), lambda l, j: (l, 0, j)),
        out_shape=jax.ShapeDtypeStruct((depth, MOD_ROWS, n), F32),
        compiler_params=_params("parallel", "parallel"),
        name="ada_mod",
    )(cc, ada_w, ada_b.reshape(depth, 1, n))


def _mod_spec(layer, which, row_of_tile):
    def index(i, j):
        return (layer, row_of_tile(i), which, 0, 0)
    return index


def _rope(v, cos, sin_a, sin_b):
    return v * cos + pltpu.roll(v, LANES - 16, 1) * sin_a + pltpu.roll(v, 16, 1) * sin_b


def _inproj_kernel(*refs, nq, use_rope, kv_only, tn):
    if use_rope:
        x_ref, g_ref, sh_ref, sc_ref, w_ref, cos_ref, sa_ref, sb_ref = refs[:8]
        rest = refs[8:]
    else:
        x_ref, g_ref, sh_ref, sc_ref, w_ref = refs[:5]
        rest = refs[5:]
    if kv_only:
        kvx_ref, h_scr = rest
    else:
        q_ref, kvx_ref, gate_ref, h_scr = rest
    j = pl.program_id(1)

    @pl.when(j == 0)
    def _():
        h_scr[...] = _norm_mod(x_ref[...], g_ref[...], sh_ref[...], sc_ref[...]).astype(BF16)

    def rope(v):
        if not use_rope:
            return v
        return _rope(v, cos_ref[...], sa_ref[...], sb_ref[...])

    def kv_epilogue():
        acc = _dot(h_scr[...], w_ref[...])
        kvw = tn // 2
        lo_mask = lax.broadcasted_iota(jnp.int32, (acc.shape[0], LANES), 1) < HEAD_DIM
        for part in range(2):
            for pr in range(kvw // LANES):
                v = acc[:, part * kvw + pr * LANES: part * kvw + (pr + 1) * LANES]
                if part == 0:
                    v = rope(v)
                r = pltpu.roll(v, HEAD_DIM, 1)
                tiles = ((jnp.where(lo_mask, v, 0.0), jnp.where(lo_mask, 0.0, r)),
                         (jnp.where(lo_mask, r, 0.0), jnp.where(lo_mask, 0.0, v)))
                for odd in range(2):
                    head = 2 * pr + odd
                    for hi in range(2):
                        col = head * 4 * LANES + (2 * part + hi) * LANES
                        kvx_ref[:, col:col + LANES] = tiles[odd][hi].astype(BF16)

    if kv_only:
        kv_epilogue()
        return

    @pl.when(j < nq)
    def _():
        acc = _dot(h_scr[...], w_ref[...])
        for cg in range(tn // LANES):
            v = rope(acc[:, cg * LANES:(cg + 1) * LANES]) * Q_SCALE
            q_ref[:, cg * LANES:(cg + 1) * LANES] = v.astype(BF16)

    @pl.when(j == nq)
    def _():
        kv_epilogue()

    @pl.when(j > nq)
    def _():
        gate_ref[...] = _dot(h_scr[...], w_ref[...])


def _inproj(xr, norm_g, mods, w_bf, layer, *, tm, row_of_tile, rope_tabs, seq, kv_only):
    m, d = xr.shape
    q_w = N_HEADS * HEAD_DIM
    kv_w = N_KV_HEADS * HEAD_DIM
    tn = 2 * kv_w
    nq = q_w // tn
    gate_off = (q_w + 2 * kv_w + 3 * d) // tn
    ng = 2 * d // tn
    use_rope = rope_tabs is not None
    kvx_w = N_KV_HEADS * 4 * LANES

    if kv_only:
        grid = (m // tm, 1)
        w_index = lambda i, j: (0, nq)
    else:
        grid = (m // tm, nq + 1 + ng)
        w_index = lambda i, j: (0, jnp.where(j <= nq, j, j + gate_off - nq - 1))

    in_specs = [
        pl.BlockSpec((tm, d), lambda i, j: (i, 0)),
        pl.BlockSpec((None, 1, d), lambda i, j: (layer, 0, 0)),
        pl.BlockSpec((None, None, None, 1, d), _mod_spec(layer, 0, row_of_tile)),
        pl.BlockSpec((None, None, None, 1, d), _mod_spec(layer, 1, row_of_tile)),
        pl.BlockSpec((None, d, tn), lambda i, j: (layer,) + w_index(i, j)),
    ]
    args = [xr, norm_g, mods, mods, w_bf]
    if use_rope:
        tps = seq // tm
        for t in rope_tabs:
            in_specs.append(pl.BlockSpec((tm, LANES), lambda i, j: (i % tps, 0)))
            args.append(t)

    kvx_spec = pl.BlockSpec((tm, kvx_w), lambda i, j: (i, 0))
    kvx_shape = jax.ShapeDtypeStruct((m, kvx_w), BF16)
    if kv_only:
        out_specs, out_shape = kvx_spec, kvx_shape
    else:
        out_specs = [
            pl.BlockSpec((tm, tn), lambda i, j: (i, jnp.minimum(j, nq - 1))),
            kvx_spec,
            pl.BlockSpec((tm, tn), lambda i, j: (i, jnp.clip(j - nq - 1, 0, ng - 1))),
        ]
        out_shape = [jax.ShapeDtypeStruct((m, q_w), BF16), kvx_shape,
                     jax.ShapeDtypeStruct((m, 2 * d), F32)]
    return pl.pallas_call(
        functools.partial(_inproj_kernel, nq=nq, use_rope=use_rope, kv_only=kv_only, tn=tn),
        grid=grid,
        in_specs=in_specs,
        out_specs=out_specs,
        out_shape=out_shape,
        scratch_shapes=[pltpu.VMEM((tm, d), BF16)],
        compiler_params=_params("parallel", "arbitrary"),
        name="inproj_kv" if kv_only else "inproj",
    )(*args)


def _conv_kernel(xp_ref, x_ref, xn_ref, g_ref, sh_ref, sc_ref, wb_ref, wc_ref, wx_ref,
                 cw_ref, cb_ref, y_ref, h_scr, *, tm, seq):
    i = pl.program_id(0)
    j = pl.program_id(1)
    halo = BF16_ROWS

    @pl.when(j == 0)
    def _():
        g, sh, sc = g_ref[...], sh_ref[...], sc_ref[...]
        h_scr[0:halo] = _norm_mod(xp_ref[...], g, sh, sc).astype(BF16)
        h_scr[halo:halo + tm] = _norm_mod(x_ref[...], g, sh, sc).astype(BF16)
        h_scr[halo + tm:] = _norm_mod(xn_ref[...], g, sh, sc).astype(BF16)

    hx = h_scr[...]
    u = _dot(hx, wc_ref[...]) * _dot(hx, wx_ref[...])
    rows = tm + 2 * halo
    u_up = pltpu.roll(u, 1, 0)[halo:halo + tm]
    u_dn = pltpu.roll(u, rows - 1, 0)[halo:halo + tm]
    row = lax.broadcasted_iota(jnp.int32, (tm, 1), 0)
    seq_start = (i * tm) % seq == 0
    seq_end = ((i + 1) * tm) % seq == 0
    u_up = jnp.where(jnp.logical_and(row == 0, seq_start), 0.0, u_up)
    u_dn = jnp.where(jnp.logical_and(row == tm - 1, seq_end), 0.0, u_dn)
    w = cw_ref[...]
    conv = u_up * w[0:1] + u[halo:halo + tm] * w[1:2] + u_dn * w[2:3] + cb_ref[...]
    y_ref[...] = (_dot(h_scr[halo:halo + tm], wb_ref[...]) * conv).astype(BF16)


def _conv_branch(xr, norm_g, mods, w_bf, conv_w, conv_b, layer, *, tm, tn, row_of_tile, seq):
    m, d = xr.shape
    d_conv = conv_w.shape[-1]
    halo = BF16_ROWS
    off = (N_HEADS * HEAD_DIM + 2 * N_KV_HEADS * HEAD_DIM) // tn
    nc = d_conv // tn
    hb = tm // halo
    last_hb = m // halo - 1

    def wspec(k):
        return pl.BlockSpec((None, d, tn), lambda i, j: (layer, 0, off + k * nc + j))

    return pl.pallas_call(
        functools.partial(_conv_kernel, tm=tm, seq=seq),
        grid=(m // tm, nc),
        in_specs=[
            pl.BlockSpec((halo, d), lambda i, j: (jnp.maximum(i * hb - 1, 0), 0)),
            pl.BlockSpec((tm, d), lambda i, j: (i, 0)),
            pl.BlockSpec((halo, d), lambda i, j: (jnp.minimum((i + 1) * hb, last_hb), 0)),
            pl.BlockSpec((None, 1, d), lambda i, j: (layer, 0, 0)),
            pl.BlockSpec((None, None, None, 1, d), _mod_spec(layer, 0, row_of_tile)),
            pl.BlockSpec((None, None, None, 1, d), _mod_spec(layer, 1, row_of_tile)),
            wspec(0), wspec(1), wspec(2),
            pl.BlockSpec((None, 3, tn), lambda i, j: (layer, 0, j)),
            pl.BlockSpec((None, 1, tn), lambda i, j: (layer, 0, j)),
        ],
        out_specs=pl.BlockSpec((tm, tn), lambda i, j: (i, j)),
        out_shape=jax.ShapeDtypeStruct((m, d_conv), BF16),
        scratch_shapes=[pltpu.VMEM((tm + 2 * halo, d), BF16)],
        compiler_params=_params("parallel", "arbitrary"),
        name="conv_branch",
    )(xr, xr, xr, norm_g, mods, mods, w_bf, w_bf, w_bf, conv_w, conv_b)


def _group_attention(q_ref, rows, pieces, sinks, o_ref):
    blk = ATT_BLOCK
    npair = len(sinks) // 2
    qs = jnp.concatenate([q_ref[rows, jp * LANES:(jp + 1) * LANES] for jp in range(npair)], axis=0)
    probs, vals, inv = [], [], []
    for par in range(2):
        chunks = []
        for kvx, biases in pieces:
            s = _dot_nt(qs, kvx[:, par * LANES:(par + 1) * LANES])
            for ci in range(s.shape[1] // LANES):
                ch = s[:, ci * LANES:(ci + 1) * LANES]
                if biases is not None and biases[ci] is not None:
                    ch = ch + jnp.concatenate([biases[ci]] * npair, axis=0)
                chunks.append(ch)
            vals.append(kvx[:, (2 + par) * LANES:(3 + par) * LANES])
        sink_col = jnp.concatenate([jnp.full((blk, 1), sinks[2 * jp + par], F32) for jp in range(npair)], axis=0)
        mx = jnp.max(functools.reduce(jnp.maximum, chunks), axis=-1, keepdims=True)
        mx = jnp.maximum(mx, sink_col)
        ps = [jnp.exp2(ch - mx) for ch in chunks]
        den = jnp.sum(functools.reduce(jnp.add, ps), axis=-1, keepdims=True) + jnp.exp2(sink_col - mx)
        inv.append(1.0 / den)
        probs.extend(p.astype(BF16) for p in ps)
    out = _dot(jnp.concatenate(probs, axis=1), jnp.concatenate(vals, axis=0))
    lane = lax.broadcasted_iota(jnp.int32, out.shape, 1)
    out = out * jnp.where(lane < HEAD_DIM, inv[0], inv[1])
    for jp in range(npair):
        o_ref[rows, jp * LANES:(jp + 1) * LANES] = out[jp * blk:(jp + 1) * blk].astype(BF16)


def _attn_kernel(sink_ref, q_ref, kvp_ref, kvc_ref, kvn_ref, ctx_ref, o_ref, band_scr, *, tq, nblk_seq):
    i = pl.program_id(1)
    h = pl.program_id(2)
    blk = ATT_BLOCK
    groups = N_HEADS // N_KV_HEADS
    band_scr[0:blk] = kvp_ref[...]
    band_scr[blk:blk + tq] = kvc_ref[...]
    band_scr[blk + tq:] = kvn_ref[...]
    ctx = ctx_ref[...]
    sinks = [sink_ref[h * groups + g] * LOG2E for g in range(groups)]
    r = lax.broadcasted_iota(jnp.int32, (blk, blk), 0)
    c = lax.broadcasted_iota(jnp.int32, (blk, blk), 1)
    bias_prev = jnp.where(c >= r, 0.0, NEG)
    bias_next = jnp.where(c <= r, 0.0, NEG)

    for sb in range(tq // blk):
        r0 = sb * blk
        gblk = i * (tq // blk) + sb
        pen_first = jnp.where(gblk == 0, NEG, 0.0)
        pen_last = jnp.where(gblk == nblk_seq - 1, NEG, 0.0)
        biases = (bias_prev + pen_first, None, bias_next + pen_last)
        band = band_scr[r0:r0 + 3 * blk, :]
        _group_attention(q_ref, pl.ds(r0, blk), ((ctx, None), (band, biases)), sinks, o_ref)


def _attention(q, kvx, kvx_ctx, sink, *, batch, seq, n_ctx, tq):
    m, q_w = q.shape
    blk = ATT_BLOCK
    hw = q_w // N_KV_HEADS
    kw = 4 * LANES
    nb = seq // blk
    tpb = tq // blk

    return pl.pallas_call(
        functools.partial(_attn_kernel, tq=tq, nblk_seq=nb),
        grid=(batch, seq // tq, N_KV_HEADS),
        in_specs=[
            pl.BlockSpec(memory_space=pltpu.SMEM),
            pl.BlockSpec((tq, hw), lambda b, i, h: (b * (seq // tq) + i, h)),
            pl.BlockSpec((blk, kw), lambda b, i, h: (b * nb + jnp.maximum(i * tpb - 1, 0), h)),
            pl.BlockSpec((tq, kw), lambda b, i, h: (b * (seq // tq) + i, h)),
            pl.BlockSpec((blk, kw), lambda b, i, h: (b * nb + jnp.minimum((i + 1) * tpb, nb - 1), h)),
            pl.BlockSpec((n_ctx, kw), lambda b, i, h: (b, h)),
        ],
        out_specs=pl.BlockSpec((tq, hw), lambda b, i, h: (b * (seq // tq) + i, h)),
        out_shape=jax.ShapeDtypeStruct((m, q_w), BF16),
        scratch_shapes=[pltpu.VMEM((tq + 2 * blk, kw), BF16)],
        compiler_params=_params("parallel", "parallel", "arbitrary"),
        name="band_attention",
    )(sink, q, kvx, kvx, kvx, kvx_ctx)


def _ctx_attn_kernel(sink_ref, q_ref, ctx_ref, o_ref, *, n_ctx):
    h = pl.program_id(1)
    groups = N_HEADS // N_KV_HEADS
    ctx = ctx_ref[...]
    sinks = [sink_ref[h * groups + g] * LOG2E for g in range(groups)]
    for r0 in range(0, n_ctx, ATT_BLOCK):
        _group_attention(q_ref, pl.ds(r0, ATT_BLOCK), ((ctx, None),), sinks, o_ref)


def _ctx_attention(q, kvx_ctx, sink, *, batch, n_ctx):
    m, q_w = q.shape
    hw = q_w // N_KV_HEADS
    kw = 4 * LANES
    return pl.pallas_call(
        functools.partial(_ctx_attn_kernel, n_ctx=n_ctx),
        grid=(batch, N_KV_HEADS),
        in_specs=[
            pl.BlockSpec(memory_space=pltpu.SMEM),
            pl.BlockSpec((n_ctx, hw), lambda b, h: (b, h)),
            pl.BlockSpec((n_ctx, kw), lambda b, h: (b, h)),
        ],
        out_specs=pl.BlockSpec((n_ctx, hw), lambda b, h: (b, h)),
        out_shape=jax.ShapeDtypeStruct((m, q_w), BF16),
        compiler_params=_params("parallel", "arbitrary"),
        name="ctx_attention",
    )(sink, q, kvx_ctx)


def _merge_kernel(a_ref, y_ref, ga_ref, gc_ref, wa_ref, wc_ref, m_ref):
    m = (_sigmoid(ga_ref[...]) * _dot(a_ref[...], wa_ref[...])
         + _sigmoid(gc_ref[...]) * _dot(y_ref[...], wc_ref[...]))
    m_ref[...] = m.astype(BF16)


def _merge(attn, y, gates, wa_bf, wc_bf, layer, *, tm, tn):
    m, q_w = attn.shape
    d_conv = y.shape[1]
    d = wa_bf.shape[-1]
    nn = d // tn
    return pl.pallas_call(
        _merge_kernel,
        grid=(m // tm, nn),
        in_specs=[
            pl.BlockSpec((tm, q_w), lambda i, j: (i, 0)),
            pl.BlockSpec((tm, d_conv), lambda i, j: (i, 0)),
            pl.BlockSpec((tm, tn), lambda i, j: (i, j)),
            pl.BlockSpec((tm, tn), lambda i, j: (i, nn + j)),
            pl.BlockSpec((None, q_w, tn), lambda i, j: (layer, 0, j)),
            pl.BlockSpec((None, d_conv, tn), lambda i, j: (layer, 0, j)),
        ],
        out_specs=pl.BlockSpec((tm, tn), lambda i, j: (i, j)),
        out_shape=jax.ShapeDtypeStruct((m, d), BF16),
        compiler_params=_params("parallel", "arbitrary"),
        name="merge",
    )(attn, y, gates, gates, wa_bf, wc_bf)


def _outproj_kernel(m_ref, w_ref, x_ref, g_ref, o_ref):
    o_ref[...] = x_ref[...] + g_ref[...] * _dot(m_ref[...], w_ref[...])


def _outproj(mm, xr, mods, wo_bf, layer, *, tm, tn, row_of_tile):
    m, d = xr.shape
    return pl.pallas_call(
        _outproj_kernel,
        grid=(m // tm, d // tn),
        in_specs=[
            pl.BlockSpec((tm, d), lambda i, j: (i, 0)),
            pl.BlockSpec((None, d, tn), lambda i, j: (layer, 0, j)),
            pl.BlockSpec((tm, tn), lambda i, j: (i, j)),
            pl.BlockSpec((None, None, None, 1, tn), lambda i, j: (layer, row_of_tile(i), 2, 0, j)),
        ],
        out_specs=pl.BlockSpec((tm, tn), lambda i, j: (i, j)),
        out_shape=jax.ShapeDtypeStruct((m, d), F32),
        compiler_params=_params("parallel", "arbitrary"),
        name="outproj",
    )(mm, wo_bf, xr, mods)


def _ffn_kernel(*refs, final):
    if final:
        x_ref, g_ref, sh_ref, sc_ref, gt_ref, wg_ref, wu_ref, wo_ref, fg_ref, o_ref, h_scr = refs
    else:
        x_ref, g_ref, sh_ref, sc_ref, gt_ref, wg_ref, wu_ref, wo_ref, o_ref, h_scr = refs
    f = pl.program_id(1)
    nf = pl.num_programs(1)

    @pl.when(f == 0)
    def _():
        h_scr[...] = _norm_mod(x_ref[...], g_ref[...], sh_ref[...], sc_ref[...]).astype(BF16)

    def contrib():
        h = h_scr[...]
        gate = _dot(h, wg_ref[...])
        act = (gate * _sigmoid(gate) * _dot(h, wu_ref[...])).astype(BF16)
        return _dot(act, wo_ref[...])

    @pl.when(f == 0)
    def _():
        o_ref[...] = contrib()

    @pl.when(f > 0)
    def _():
        o_ref[...] += contrib()

    @pl.when(f == nf - 1)
    def _():
        xn = x_ref[...] + gt_ref[...] * o_ref[...]
        if final:
            xn = xn * lax.rsqrt(jnp.mean(xn * xn, axis=-1, keepdims=True) + EPS) * fg_ref[...]
        o_ref[...] = xn


def _ffn(xr, norm_g, mods, wi_bf, wo_bf, layer, *, tm, tf, row_of_tile, final_g):
    m, d = xr.shape
    d_ff = wo_bf.shape[1]
    nf = d_ff // tf
    final = final_g is not None
    in_specs = [
        pl.BlockSpec((tm, d), lambda i, f: (i, 0)),
        pl.BlockSpec((None, 1, d), lambda i, f: (layer, 0, 0)),
        pl.BlockSpec((None, None, None, 1, d), _mod_spec(layer, 3, row_of_tile)),
        pl.BlockSpec((None, None, None, 1, d), _mod_spec(layer, 4, row_of_tile)),
        pl.BlockSpec((None, None, None, 1, d), _mod_spec(layer, 5, row_of_tile)),
        pl.BlockSpec((None, d, tf), lambda i, f: (layer, 0, f)),
        pl.BlockSpec((None, d, tf), lambda i, f: (layer, 0, nf + f)),
        pl.BlockSpec((None, tf, d), lambda i, f: (layer, f, 0)),
    ]
    args = [xr, norm_g, mods, mods, mods, wi_bf, wi_bf, wo_bf]
    if final:
        in_specs.append(pl.BlockSpec((1, d), lambda i, f: (0, 0)))
        args.append(final_g)
    return pl.pallas_call(
        functools.partial(_ffn_kernel, final=final),
        grid=(m // tm, nf),
        in_specs=in_specs,
        out_specs=pl.BlockSpec((tm, d), lambda i, f: (i, 0)),
        out_shape=jax.ShapeDtypeStruct((m, d), F32),
        scratch_shapes=[pltpu.VMEM((tm, d), BF16)],
        compiler_params=_params("parallel", "arbitrary"),
        name="ffn_final" if final else "ffn",
    )(*args)


def _rope_tables(seq):
    quarter = HEAD_DIM // 4
    t = np.arange(seq)
    pos = np.stack([t // GRID_W, t % GRID_W], axis=1).astype(np.float32)
    d = np.arange(LANES) % HEAD_DIM
    axis = d // (HEAD_DIM // 2)
    first_half = (d // quarter) % 2 == 0
    freqs = jnp.asarray(ROPE_THETA, F32) ** (-jnp.arange(quarter, dtype=F32) / quarter)
    ang = jnp.asarray(pos)[:, axis] * freqs[d % quarter][None, :]
    cos, sin = jnp.cos(ang), jnp.sin(ang)
    fh = jnp.asarray(first_half)[None, :]
    return cos, jnp.where(fh, -sin, 0.0), jnp.where(fh, 0.0, sin)


def _tile(n, pref):
    return pref if n % pref == 0 else n


def kernel(x, c, ctx, c_ctx, ada_w, ada_b, norm1_g, norm2_g, w_in, conv_w, conv_b, sink,
           w_attn_out, w_conv_out, w_o, w_ffn_in, w_ffn_out, final_g):
    batch, seq, d = x.shape
    n_ctx = ctx.shape[1]
    depth = ada_w.shape[0]
    d_ff = w_ffn_out.shape[1]
    assert batch + 1 <= MOD_ROWS and seq % ATT_BLOCK == 0 and n_ctx % ATT_BLOCK == 0

    cc = jnp.zeros((MOD_ROWS, d), F32).at[:batch].set(c).at[batch].set(c_ctx)
    mods = _ada(cc, ada_w, ada_b).reshape(depth, MOD_ROWS, 6, 1, d)

    w_in_bf = w_in.astype(BF16)
    wa_bf = w_attn_out.astype(BF16)
    wc_bf = w_conv_out.astype(BF16)
    wo_bf = w_o.astype(BF16)
    wfi_bf = w_ffn_in.astype(BF16)
    wfo_bf = w_ffn_out.astype(BF16)
    n1 = norm1_g.reshape(depth, 1, d)
    n2 = norm2_g.reshape(depth, 1, d)
    conv_b3 = conv_b.reshape(depth, 1, -1)
    fg = final_g.reshape(1, d)
    rope_tabs = _rope_tables(seq)

    tm = _tile(seq, 1024)
    tm_ffn = _tile(seq, 512)
    tn = 2 * N_KV_HEADS * HEAD_DIM
    tf = _tile(d_ff, tn)
    tq = _tile(seq, 512)
    lat_row = lambda t: (lambda i: i // (seq // t))
    ctx_row = lambda i: batch

    xl = x.reshape(batch * seq, d)
    xc = ctx.reshape(batch * n_ctx, d)
    for l in range(depth):
        last = l == depth - 1
        ctx_kw = dict(tm=n_ctx, row_of_tile=ctx_row)
        if last:
            kvx_c = _inproj(xc, n1, mods, w_in_bf, l, rope_tabs=None, seq=n_ctx, kv_only=True, **ctx_kw)
        else:
            q_c, kvx_c, gates_c = _inproj(xc, n1, mods, w_in_bf, l, rope_tabs=None, seq=n_ctx,
                                          kv_only=False, **ctx_kw)
            y_c = _conv_branch(xc, n1, mods, w_in_bf, conv_w, conv_b3, l, tn=tn, seq=n_ctx, **ctx_kw)

        q, kvx, gates = _inproj(xl, n1, mods, w_in_bf, l, tm=tm, row_of_tile=lat_row(tm),
                                rope_tabs=rope_tabs, seq=seq, kv_only=False)
        y = _conv_branch(xl, n1, mods, w_in_bf, conv_w, conv_b3, l, tm=tm, tn=tn,
                         row_of_tile=lat_row(tm), seq=seq)
        attn = _attention(q, kvx, kvx_c, sink[l], batch=batch, seq=seq, n_ctx=n_ctx, tq=tq)
        mm = _merge(attn, y, gates, wa_bf, wc_bf, l, tm=tm, tn=tn)
        xl = _outproj(mm, xl, mods, wo_bf, l, tm=tm, tn=tn, row_of_tile=lat_row(tm))

        if not last:
            attn_c = _ctx_attention(q_c, kvx_c, sink[l], batch=batch, n_ctx=n_ctx)
            mm_c = _merge(attn_c, y_c, gates_c, wa_bf, wc_bf, l, tm=n_ctx, tn=tn)
            xc = _outproj(mm_c, xc, mods, wo_bf, l, tn=tn, **ctx_kw)
            xc = _ffn(xc, n2, mods, wfi_bf, wfo_bf, l, tf=tf, final_g=None, **ctx_kw)

        xl = _ffn(xl, n2, mods, wfi_bf, wfo_bf, l, tm=tm_ffn, tf=tf, row_of_tile=lat_row(tm_ffn),
                  final_g=fg if last else None)
    return xl.reshape(batch, seq, d)
```

```python
import functools

import jax
import jax.numpy as jnp
import numpy as np
from jax import lax
from jax.experimental import pallas as pl
from jax.experimental.pallas import tpu as pltpu

GRID_W = 64
N_HEADS = 32
N_KV_HEADS = 4
HEAD_DIM = 64
WINDOW = 128
ROPE_THETA = 10000.0
EPS = 1e-6
NEG = -1e30
LOG2E = 1.4426950408889634
Q_SCALE = HEAD_DIM ** -0.5 * LOG2E

LANES = 128
BF16_ROWS = 16
V7X_VMEM_BYTES = 64 * 1024 * 1024
VMEM_LIMIT_BYTES = V7X_VMEM_BYTES * 15 // 16
ATT_BLOCK = 128
MOD_ROWS = 8
FFN_OUT_CHUNK = 512

F32 = jnp.float32
BF16 = jnp.bfloat16


def _params(*sem):
    return pltpu.CompilerParams(dimension_semantics=sem, vmem_limit_bytes=VMEM_LIMIT_BYTES)


def _sigmoid(v):
    return 1.0 / (1.0 + jnp.exp(-v))


def _norm_mod(x, g, shift, scale):
    y = x * lax.rsqrt(jnp.mean(x * x, axis=-1, keepdims=True) + EPS)
    return (y * g) * (1.0 + scale) + shift


def _dot(a, b):
    return jnp.dot(a, b, preferred_element_type=F32)


def _dot_nt(a, b):
    return lax.dot_general(a, b, (((1,), (1,)), ((), ())), preferred_element_type=F32)


def _ada_kernel(c_ref, w_ref, b_ref, o_ref):
    c = c_ref[...]
    s = c * _sigmoid(c)
    o_ref[...] = _dot(s.astype(BF16), w_ref[...].astype(BF16)) + b_ref[...]


def _ada(cc, ada_w, ada_b):
    depth, d, n = ada_w.shape
    tn = _tile(n, 1024)
    return pl.pallas_call(
        _ada_kernel,
        grid=(depth, n // tn),
        in_specs=[
            pl.BlockSpec((MOD_ROWS, d), lambda l, j: (0, 0)),
            pl.BlockSpec((None, d, tn), lambda l, j: (l, 0, j)),
            pl.BlockSpec((None, 1, tn), lambda l, j: (l, 0, j)),
        ],
        out_specs=pl.BlockSpec((None, MOD_ROWS, tn), lambda l, j: (l, 0, j)),
        out_shape=jax.ShapeDtypeStruct((depth, MOD_ROWS, n), F32),
        compiler_params=_params("parallel", "parallel"),
        name="ada_mod",
    )(cc, ada_w, ada_b.reshape(depth, 1, n))


def _mod_spec(layer, which, row_of_tile):
    def index(i, j):
        return (layer, row_of_tile(i), which, 0, 0)
    return index


def _rope(v, cos, sin_a, sin_b):
    return v * cos + pltpu.roll(v, LANES - 16, 1) * sin_a + pltpu.roll(v, 16, 1) * sin_b


def _inproj_kernel(*refs, use_rope, with_q):
    refs = list(refs)
    x_ref, g_ref, sh_ref, sc_ref, wkv_ref = refs[:5]
    del refs[:5]
    wq_ref = refs.pop(0) if with_q else None
    if use_rope:
        cos_ref, sa_ref, sb_ref = refs[:3]
        del refs[:3]
    q_ref = refs.pop(0) if with_q else None
    kvx_ref, h_ref = refs
    j = pl.program_id(1)

    def rope(v):
        if not use_rope:
            return v
        return _rope(v, cos_ref[...], sa_ref[...], sb_ref[...])

    @pl.when(j == 0)
    def _():
        h = _norm_mod(x_ref[...], g_ref[...], sh_ref[...], sc_ref[...]).astype(BF16)
        h_ref[...] = h
        acc = _dot(h, wkv_ref[...])
        kvw = acc.shape[1] // 2
        lo_mask = lax.broadcasted_iota(jnp.int32, (acc.shape[0], LANES), 1) < HEAD_DIM
        for part in range(2):
            for pr in range(kvw // LANES):
                v = acc[:, part * kvw + pr * LANES: part * kvw + (pr + 1) * LANES]
                if part == 0:
                    v = rope(v)
                r = pltpu.roll(v, HEAD_DIM, 1)
                tiles = ((jnp.where(lo_mask, v, 0.0), jnp.where(lo_mask, 0.0, r)),
                         (jnp.where(lo_mask, r, 0.0), jnp.where(lo_mask, 0.0, v)))
                for odd in range(2):
                    head = 2 * pr + odd
                    for hi in range(2):
                        col = head * 4 * LANES + (2 * part + hi) * LANES
                        kvx_ref[:, col:col + LANES] = tiles[odd][hi].astype(BF16)

    if with_q:
        acc = _dot(h_ref[...], wq_ref[...])
        for cg in range(acc.shape[1] // LANES):
            v = rope(acc[:, cg * LANES:(cg + 1) * LANES]) * Q_SCALE
            q_ref[:, cg * LANES:(cg + 1) * LANES] = v.astype(BF16)


def _inproj(xr, norm_g, mods, w_bf, layer, *, tm, row_of_tile, rope_tabs, seq, with_q):
    m, d = xr.shape
    q_w = N_HEADS * HEAD_DIM
    kv_w = N_KV_HEADS * HEAD_DIM
    tnq = _tile(q_w, 1024)
    use_rope = rope_tabs is not None
    kvx_w = N_KV_HEADS * 4 * LANES

    in_specs = [
        pl.BlockSpec((tm, d), lambda i, j: (i, 0)),
        pl.BlockSpec((None, 1, d), lambda i, j: (layer, 0, 0)),
        pl.BlockSpec((None, None, None, 1, d), _mod_spec(layer, 0, row_of_tile)),
        pl.BlockSpec((None, None, None, 1, d), _mod_spec(layer, 1, row_of_tile)),
        pl.BlockSpec((None, d, 2 * kv_w), lambda i, j: (layer, 0, q_w // (2 * kv_w))),
    ]
    args = [xr, norm_g, mods, mods, w_bf]
    if with_q:
        in_specs.append(pl.BlockSpec((None, d, tnq), lambda i, j: (layer, 0, j)))
        args.append(w_bf)
    if use_rope:
        tps = seq // tm
        for t in rope_tabs:
            in_specs.append(pl.BlockSpec((tm, LANES), lambda i, j: (i % tps, 0)))
            args.append(t)

    out_specs = [pl.BlockSpec((tm, kvx_w), lambda i, j: (i, 0)),
                 pl.BlockSpec((tm, d), lambda i, j: (i, 0))]
    out_shape = [jax.ShapeDtypeStruct((m, kvx_w), BF16), jax.ShapeDtypeStruct((m, d), BF16)]
    if with_q:
        out_specs.insert(0, pl.BlockSpec((tm, tnq), lambda i, j: (i, j)))
        out_shape.insert(0, jax.ShapeDtypeStruct((m, q_w), BF16))
    out = pl.pallas_call(
        functools.partial(_inproj_kernel, use_rope=use_rope, with_q=with_q),
        grid=(m // tm, q_w // tnq if with_q else 1),
        in_specs=in_specs,
        out_specs=out_specs,
        out_shape=out_shape,
        compiler_params=_params("parallel", "arbitrary"),
        name="inproj" if with_q else "inproj_kv",
    )(*args)
    return out if with_q else (None,) + tuple(out)


def _conv_kernel(hp_ref, h_ref, hn_ref, wb_ref, wc_ref, wx_ref, cw_ref, cb_ref, y_ref, h_scr, *, tm, seq):
    i = pl.program_id(0)
    j = pl.program_id(1)
    halo = BF16_ROWS

    @pl.when(j == 0)
    def _():
        h_scr[0:halo] = hp_ref[...]
        h_scr[halo:halo + tm] = h_ref[...]
        h_scr[halo + tm:] = hn_ref[...]

    hx = h_scr[...]
    u = _dot(hx, wc_ref[...]) * _dot(hx, wx_ref[...])
    rows = tm + 2 * halo
    u_up = pltpu.roll(u, 1, 0)[halo:halo + tm]
    u_dn = pltpu.roll(u, rows - 1, 0)[halo:halo + tm]
    row = lax.broadcasted_iota(jnp.int32, (tm, 1), 0)
    seq_start = (i * tm) % seq == 0
    seq_end = ((i + 1) * tm) % seq == 0
    u_up = jnp.where(jnp.logical_and(row == 0, seq_start), 0.0, u_up)
    u_dn = jnp.where(jnp.logical_and(row == tm - 1, seq_end), 0.0, u_dn)
    w = cw_ref[...]
    conv = u_up * w[0:1] + u[halo:halo + tm] * w[1:2] + u_dn * w[2:3] + cb_ref[...]
    y_ref[...] = (_dot(h_ref[...], wb_ref[...]) * conv).astype(BF16)


def _conv_branch(h, w_bf, conv_w, conv_b, layer, *, tm, tn, seq):
    m, d = h.shape
    d_conv = conv_w.shape[-1]
    halo = BF16_ROWS
    off = (N_HEADS * HEAD_DIM + 2 * N_KV_HEADS * HEAD_DIM) // tn
    nc = d_conv // tn
    hb = tm // halo
    last_hb = m // halo - 1

    def wspec(k):
        return pl.BlockSpec((None, d, tn), lambda i, j: (layer, 0, off + k * nc + j))

    return pl.pallas_call(
        functools.partial(_conv_kernel, tm=tm, seq=seq),
        grid=(m // tm, nc),
        in_specs=[
            pl.BlockSpec((halo, d), lambda i, j: (jnp.maximum(i * hb - 1, 0), 0)),
            pl.BlockSpec((tm, d), lambda i, j: (i, 0)),
            pl.BlockSpec((halo, d), lambda i, j: (jnp.minimum((i + 1) * hb, last_hb), 0)),
            wspec(0), wspec(1), wspec(2),
            pl.BlockSpec((None, 3, tn), lambda i, j: (layer, 0, j)),
            pl.BlockSpec((None, 1, tn), lambda i, j: (layer, 0, j)),
        ],
        out_specs=pl.BlockSpec((tm, tn), lambda i, j: (i, j)),
        out_shape=jax.ShapeDtypeStruct((m, d_conv), BF16),
        scratch_shapes=[pltpu.VMEM((tm + 2 * halo, d), BF16)],
        compiler_params=_params("parallel", "arbitrary"),
        name="conv_branch",
    )(h, h, h, w_bf, w_bf, w_bf, conv_w, conv_b)


def _group_attention(q_ref, rows, pieces, sinks, o_ref):
    blk = ATT_BLOCK
    npair = len(sinks) // 2
    qs = jnp.concatenate([q_ref[rows, jp * LANES:(jp + 1) * LANES] for jp in range(npair)], axis=0)
    probs, vals, inv = [], [], []
    for par in range(2):
        chunks = []
        for kvx, biases in pieces:
            s = _dot_nt(qs, kvx[:, par * LANES:(par + 1) * LANES])
            for ci in range(s.shape[1] // LANES):
                ch = s[:, ci * LANES:(ci + 1) * LANES]
                if biases is not None and biases[ci] is not None:
                    ch = ch + jnp.concatenate([biases[ci]] * npair, axis=0)
                chunks.append(ch)
            vals.append(kvx[:, (2 + par) * LANES:(3 + par) * LANES])
        sink_col = jnp.concatenate([jnp.full((blk, 1), sinks[2 * jp + par], F32) for jp in range(npair)], axis=0)
        mx = jnp.max(functools.reduce(jnp.maximum, chunks), axis=-1, keepdims=True)
        mx = jnp.maximum(mx, sink_col)
        ps = [jnp.exp2(ch - mx) for ch in chunks]
        den = jnp.sum(functools.reduce(jnp.add, ps), axis=-1, keepdims=True) + jnp.exp2(sink_col - mx)
        inv.append(1.0 / den)
        probs.extend(p.astype(BF16) for p in ps)
    out = _dot(jnp.concatenate(probs, axis=1), jnp.concatenate(vals, axis=0))
    lane = lax.broadcasted_iota(jnp.int32, out.shape, 1)
    out = out * jnp.where(lane < HEAD_DIM, inv[0], inv[1])
    for jp in range(npair):
        o_ref[rows, jp * LANES:(jp + 1) * LANES] = out[jp * blk:(jp + 1) * blk].astype(BF16)


def _attn_kernel(sink_ref, q_ref, kvp_ref, kvc_ref, kvn_ref, ctx_ref, o_ref, band_scr, *, tq, nblk_seq):
    i = pl.program_id(1)
    h = pl.program_id(2)
    blk = ATT_BLOCK
    groups = N_HEADS // N_KV_HEADS
    band_scr[0:blk] = kvp_ref[...]
    band_scr[blk:blk + tq] = kvc_ref[...]
    band_scr[blk + tq:] = kvn_ref[...]
    ctx = ctx_ref[...]
    sinks = [sink_ref[h * groups + g] * LOG2E for g in range(groups)]
    r = lax.broadcasted_iota(jnp.int32, (blk, blk), 0)
    c = lax.broadcasted_iota(jnp.int32, (blk, blk), 1)
    bias_prev = jnp.where(c >= r, 0.0, NEG)
    bias_next = jnp.where(c <= r, 0.0, NEG)

    for sb in range(tq // blk):
        r0 = sb * blk
        gblk = i * (tq // blk) + sb
        pen_first = jnp.where(gblk == 0, NEG, 0.0)
        pen_last = jnp.where(gblk == nblk_seq - 1, NEG, 0.0)
        biases = (bias_prev + pen_first, None, bias_next + pen_last)
        band = band_scr[r0:r0 + 3 * blk, :]
        _group_attention(q_ref, pl.ds(r0, blk), ((ctx, None), (band, biases)), sinks, o_ref)


def _attention(q, kvx, kvx_ctx, sink, *, batch, seq, n_ctx, tq):
    m, q_w = q.shape
    blk = ATT_BLOCK
    hw = q_w // N_KV_HEADS
    kw = 4 * LANES
    nb = seq // blk
    tpb = tq // blk

    return pl.pallas_call(
        functools.partial(_attn_kernel, tq=tq, nblk_seq=nb),
        grid=(batch, seq // tq, N_KV_HEADS),
        in_specs=[
            pl.BlockSpec(memory_space=pltpu.SMEM),
            pl.BlockSpec((tq, hw), lambda b, i, h: (b * (seq // tq) + i, h)),
            pl.BlockSpec((blk, kw), lambda b, i, h: (b * nb + jnp.maximum(i * tpb - 1, 0), h)),
            pl.BlockSpec((tq, kw), lambda b, i, h: (b * (seq // tq) + i, h)),
            pl.BlockSpec((blk, kw), lambda b, i, h: (b * nb + jnp.minimum((i + 1) * tpb, nb - 1), h)),
            pl.BlockSpec((n_ctx, kw), lambda b, i, h: (b, h)),
        ],
        out_specs=pl.BlockSpec((tq, hw), lambda b, i, h: (b * (seq // tq) + i, h)),
        out_shape=jax.ShapeDtypeStruct((m, q_w), BF16),
        scratch_shapes=[pltpu.VMEM((tq + 2 * blk, kw), BF16)],
        compiler_params=_params("parallel", "parallel", "arbitrary"),
        name="band_attention",
    )(sink, q, kvx, kvx, kvx, kvx_ctx)


def _ctx_attn_kernel(sink_ref, q_ref, ctx_ref, o_ref, *, n_ctx):
    h = pl.program_id(1)
    groups = N_HEADS // N_KV_HEADS
    ctx = ctx_ref[...]
    sinks = [sink_ref[h * groups + g] * LOG2E for g in range(groups)]
    for r0 in range(0, n_ctx, ATT_BLOCK):
        _group_attention(q_ref, pl.ds(r0, ATT_BLOCK), ((ctx, None),), sinks, o_ref)


def _ctx_attention(q, kvx_ctx, sink, *, batch, n_ctx):
    m, q_w = q.shape
    hw = q_w // N_KV_HEADS
    kw = 4 * LANES
    return pl.pallas_call(
        functools.partial(_ctx_attn_kernel, n_ctx=n_ctx),
        grid=(batch, N_KV_HEADS),
        in_specs=[
            pl.BlockSpec(memory_space=pltpu.SMEM),
            pl.BlockSpec((n_ctx, hw), lambda b, h: (b, h)),
            pl.BlockSpec((n_ctx, kw), lambda b, h: (b, h)),
        ],
        out_specs=pl.BlockSpec((n_ctx, hw), lambda b, h: (b, h)),
        out_shape=jax.ShapeDtypeStruct((m, q_w), BF16),
        compiler_params=_params("parallel", "arbitrary"),
        name="ctx_attention",
    )(sink, q, kvx_ctx)


def _merge_kernel(a_ref, y_ref, h_ref, wa_ref, wc_ref, wga_ref, wgc_ref, m_ref):
    h = h_ref[...]
    m = (_sigmoid(_dot(h, wga_ref[...])) * _dot(a_ref[...], wa_ref[...])
         + _sigmoid(_dot(h, wgc_ref[...])) * _dot(y_ref[...], wc_ref[...]))
    m_ref[...] = m.astype(BF16)


def _merge(attn, y, h, w_in_bf, wa_bf, wc_bf, layer, *, tm, tn):
    m, q_w = attn.shape
    d_conv = y.shape[1]
    d_in = h.shape[1]
    d = wa_bf.shape[-1]
    nn = d // tn
    gate_off = (q_w + 2 * N_KV_HEADS * HEAD_DIM + 3 * d_conv) // tn
    return pl.pallas_call(
        _merge_kernel,
        grid=(m // tm, nn),
        in_specs=[
            pl.BlockSpec((tm, q_w), lambda i, j: (i, 0)),
            pl.BlockSpec((tm, d_conv), lambda i, j: (i, 0)),
            pl.BlockSpec((tm, d_in), lambda i, j: (i, 0)),
            pl.BlockSpec((None, q_w, tn), lambda i, j: (layer, 0, j)),
            pl.BlockSpec((None, d_conv, tn), lambda i, j: (layer, 0, j)),
            pl.BlockSpec((None, d_in, tn), lambda i, j: (layer, 0, gate_off + j)),
            pl.BlockSpec((None, d_in, tn), lambda i, j: (layer, 0, gate_off + nn + j)),
        ],
        out_specs=pl.BlockSpec((tm, tn), lambda i, j: (i, j)),
        out_shape=jax.ShapeDtypeStruct((m, d), BF16),
        compiler_params=_params("parallel", "arbitrary"),
        name="merge",
    )(attn, y, h, wa_bf, wc_bf, w_in_bf, w_in_bf)


def _outproj_kernel(m_ref, w_ref, x_ref, g_ref, o_ref):
    o_ref[...] = x_ref[...] + g_ref[...] * _dot(m_ref[...], w_ref[...])


def _outproj(mm, xr, mods, wo_bf, layer, *, tm, tn, row_of_tile):
    m, d = xr.shape
    return pl.pallas_call(
        _outproj_kernel,
        grid=(m // tm, d // tn),
        in_specs=[
            pl.BlockSpec((tm, d), lambda i, j: (i, 0)),
            pl.BlockSpec((None, d, tn), lambda i, j: (layer, 0, j)),
            pl.BlockSpec((tm, tn), lambda i, j: (i, j)),
            pl.BlockSpec((None, None, None, 1, tn), lambda i, j: (layer, row_of_tile(i), 2, 0, j)),
        ],
        out_specs=pl.BlockSpec((tm, tn), lambda i, j: (i, j)),
        out_shape=jax.ShapeDtypeStruct((m, d), F32),
        compiler_params=_params("parallel", "arbitrary"),
        name="outproj",
    )(mm, wo_bf, xr, mods)


def _ffn_kernel(*refs, final):
    if final:
        x_ref, g_ref, sh_ref, sc_ref, gt_ref, wg_ref, wu_ref, wo_ref, fg_ref, o_ref, h_scr = refs
    else:
        x_ref, g_ref, sh_ref, sc_ref, gt_ref, wg_ref, wu_ref, wo_ref, o_ref, h_scr = refs
    f = pl.program_id(1)
    nf = pl.num_programs(1)

    @pl.when(f == 0)
    def _():
        h_scr[...] = _norm_mod(x_ref[...], g_ref[...], sh_ref[...], sc_ref[...]).astype(BF16)

    def accumulate(first):
        h = h_scr[...]
        gate = _dot(h, wg_ref[...])
        act = (gate * _sigmoid(gate) * _dot(h, wu_ref[...])).astype(BF16)
        d = o_ref.shape[1]
        cw = _tile(d, FFN_OUT_CHUNK)
        for c0 in range(0, d, cw):
            part = _dot(act, wo_ref[:, c0:c0 + cw])
            if first:
                o_ref[:, c0:c0 + cw] = part
            else:
                o_ref[:, c0:c0 + cw] += part

    @pl.when(f == 0)
    def _():
        accumulate(True)

    @pl.when(f > 0)
    def _():
        accumulate(False)

    @pl.when(f == nf - 1)
    def _():
        xn = x_ref[...] + gt_ref[...] * o_ref[...]
        if final:
            xn = xn * lax.rsqrt(jnp.mean(xn * xn, axis=-1, keepdims=True) + EPS) * fg_ref[...]
        o_ref[...] = xn


def _ffn(xr, norm_g, mods, wi_bf, wo_bf, layer, *, tm, tf, row_of_tile, final_g):
    m, d = xr.shape
    d_ff = wo_bf.shape[1]
    nf = d_ff // tf
    final = final_g is not None
    in_specs = [
        pl.BlockSpec((tm, d), lambda i, f: (i, 0)),
        pl.BlockSpec((None, 1, d), lambda i, f: (layer, 0, 0)),
        pl.BlockSpec((None, None, None, 1, d), _mod_spec(layer, 3, row_of_tile)),
        pl.BlockSpec((None, None, None, 1, d), _mod_spec(layer, 4, row_of_tile)),
        pl.BlockSpec((None, None, None, 1, d), _mod_spec(layer, 5, row_of_tile)),
        pl.BlockSpec((None, d, tf), lambda i, f: (layer, 0, f)),
        pl.BlockSpec((None, d, tf), lambda i, f: (layer, 0, nf + f)),
        pl.BlockSpec((None, tf, d), lambda i, f: (layer, f, 0)),
    ]
    args = [xr, norm_g, mods, mods, mods, wi_bf, wi_bf, wo_bf]
    if final:
        in_specs.append(pl.BlockSpec((1, d), lambda i, f: (0, 0)))
        args.append(final_g)
    return pl.pallas_call(
        functools.partial(_ffn_kernel, final=final),
        grid=(m // tm, nf),
        in_specs=in_specs,
        out_specs=pl.BlockSpec((tm, d), lambda i, f: (i, 0)),
        out_shape=jax.ShapeDtypeStruct((m, d), F32),
        scratch_shapes=[pltpu.VMEM((tm, d), BF16)],
        compiler_params=_params("parallel", "arbitrary"),
        name="ffn_final" if final else "ffn",
    )(*args)


def _rope_tables(seq):
    quarter = HEAD_DIM // 4
    t = np.arange(seq)
    pos = np.stack([t // GRID_W, t % GRID_W], axis=1).astype(np.float32)
    d = np.arange(LANES) % HEAD_DIM
    axis = d // (HEAD_DIM // 2)
    first_half = (d // quarter) % 2 == 0
    freqs = jnp.asarray(ROPE_THETA, F32) ** (-jnp.arange(quarter, dtype=F32) / quarter)
    ang = jnp.asarray(pos)[:, axis] * freqs[d % quarter][None, :]
    cos, sin = jnp.cos(ang), jnp.sin(ang)
    fh = jnp.asarray(first_half)[None, :]
    return cos, jnp.where(fh, -sin, 0.0), jnp.where(fh, 0.0, sin)


def _tile(n, pref):
    return pref if n % pref == 0 else n


def kernel(x, c, ctx, c_ctx, ada_w, ada_b, norm1_g, norm2_g, w_in, conv_w, conv_b, sink,
           w_attn_out, w_conv_out, w_o, w_ffn_in, w_ffn_out, final_g):
    batch, seq, d = x.shape
    n_ctx = ctx.shape[1]
    depth = ada_w.shape[0]
    d_ff = w_ffn_out.shape[1]
    assert batch + 1 <= MOD_ROWS and seq % ATT_BLOCK == 0 and n_ctx % ATT_BLOCK == 0

    cc = jnp.zeros((MOD_ROWS, d), F32).at[:batch].set(c).at[batch].set(c_ctx)
    mods = _ada(cc, ada_w, ada_b).reshape(depth, MOD_ROWS, 6, 1, d)

    w_in_bf = w_in.astype(BF16)
    wa_bf = w_attn_out.astype(BF16)
    wc_bf = w_conv_out.astype(BF16)
    wo_bf = w_o.astype(BF16)
    wfi_bf = w_ffn_in.astype(BF16)
    wfo_bf = w_ffn_out.astype(BF16)
    n1 = norm1_g.reshape(depth, 1, d)
    n2 = norm2_g.reshape(depth, 1, d)
    conv_b3 = conv_b.reshape(depth, 1, -1)
    fg = final_g.reshape(1, d)
    rope_tabs = _rope_tables(seq)

    tm = _tile(seq, 1024)
    tm_in = _tile(seq, 512)
    tn = 2 * N_KV_HEADS * HEAD_DIM
    tf = _tile(d_ff, tn)
    tq = _tile(seq, 1024)
    lat_row = lambda t: (lambda i: i // (seq // t))
    ctx_row = lambda i: batch

    xl = x.reshape(batch * seq, d)
    xc = ctx.reshape(batch * n_ctx, d)
    for l in range(depth):
        last = l == depth - 1
        q_c, kvx_c, h_c = _inproj(xc, n1, mods, w_in_bf, l, tm=n_ctx, row_of_tile=ctx_row, rope_tabs=None,
                                  seq=n_ctx, with_q=not last)
        q, kvx, h = _inproj(xl, n1, mods, w_in_bf, l, tm=tm_in, row_of_tile=lat_row(tm_in),
                            rope_tabs=rope_tabs, seq=seq, with_q=True)
        y = _conv_branch(h, w_in_bf, conv_w, conv_b3, l, tm=tm, tn=tn, seq=seq)
        attn = _attention(q, kvx, kvx_c, sink[l], batch=batch, seq=seq, n_ctx=n_ctx, tq=tq)
        mm = _merge(attn, y, h, w_in_bf, wa_bf, wc_bf, l, tm=tm, tn=tn)
        xl = _outproj(mm, xl, mods, wo_bf, l, tm=tm, tn=tn, row_of_tile=lat_row(tm))

        if not last:
            y_c = _conv_branch(h_c, w_in_bf, conv_w, conv_b3, l, tm=n_ctx, tn=tn, seq=n_ctx)
            attn_c = _ctx_attention(q_c, kvx_c, sink[l], batch=batch, n_ctx=n_ctx)
            mm_c = _merge(attn_c, y_c, h_c, w_in_bf, wa_bf, wc_bf, l, tm=n_ctx, tn=tn)
            xc = _outproj(mm_c, xc, mods, wo_bf, l, tm=n_ctx, tn=tn, row_of_tile=ctx_row)
            xc = _ffn(xc, n2, mods, wfi_bf, wfo_bf, l, tm=n_ctx, tf=tf, row_of_tile=ctx_row, final_g=None)

        xl = _ffn(xl, n2, mods, wfi_bf, wfo_bf, l, tm=tm, tf=tf, row_of_tile=lat_row(tm),
                  final_g=fg if last else None)
    return xl.reshape(batch, seq, d)
```

```python
import functools

import jax
import jax.numpy as jnp
import numpy as np
from jax import lax
from jax.experimental import pallas as pl
from jax.experimental.pallas import tpu as pltpu

GRID_W = 64
N_HEADS = 32
N_KV_HEADS = 4
HEAD_DIM = 64
WINDOW = 128
ROPE_THETA = 10000.0
EPS = 1e-6
NEG = -1e30
LOG2E = 1.4426950408889634
Q_SCALE = HEAD_DIM ** -0.5 * LOG2E

LANES = 128
BF16_ROWS = 16
V7X_VMEM_BYTES = 64 * 1024 * 1024
VMEM_LIMIT_BYTES = V7X_VMEM_BYTES * 15 // 16
ATT_BLOCK = 128
MOD_ROWS = 8
FFN_OUT_CHUNK = 512

F32 = jnp.float32
BF16 = jnp.bfloat16


def _params(*sem):
    return pltpu.CompilerParams(dimension_semantics=sem, vmem_limit_bytes=VMEM_LIMIT_BYTES)


def _sigmoid(v):
    return 1.0 / (1.0 + jnp.exp(-v))


def _norm_mod(x, g, shift, scale):
    y = x * lax.rsqrt(jnp.mean(x * x, axis=-1, keepdims=True) + EPS)
    return (y * g) * (1.0 + scale) + shift


def _dot(a, b):
    return jnp.dot(a, b, preferred_element_type=F32)


def _dot_nt(a, b):
    return lax.dot_general(a, b, (((1,), (1,)), ((), ())), preferred_element_type=F32)


def _ada_kernel(c_ref, w_ref, b_ref, o_ref):
    c = c_ref[...]
    s = c * _sigmoid(c)
    o_ref[...] = _dot(s.astype(BF16), w_ref[...].astype(BF16)) + b_ref[...]


def _ada(cc, ada_w, ada_b):
    depth, d, n = ada_w.shape
    tn = _tile(n, 1024)
    return pl.pallas_call(
        _ada_kernel,
        grid=(depth, n // tn),
        in_specs=[
            pl.BlockSpec((MOD_ROWS, d), lambda l, j: (0, 0)),
            pl.BlockSpec((None, d, tn), lambda l, j: (l, 0, j)),
            pl.BlockSpec((None, 1, tn), lambda l, j: (l, 0, j)),
        ],
        out_specs=pl.BlockSpec((None, MOD_ROWS, tn), lambda l, j: (l, 0, j)),
        out_shape=jax.ShapeDtypeStruct((depth, MOD_ROWS, n), F32),
        compiler_params=_params("parallel", "parallel"),
        name="ada_mod",
    )(cc, ada_w, ada_b.reshape(depth, 1, n))


def _mod_spec(layer, which, row_of_tile):
    def index(i, j):
        return (layer, row_of_tile(i), which, 0, 0)
    return index


def _rope(v, cos, sin_a, sin_b):
    return v * cos + pltpu.roll(v, LANES - 16, 1) * sin_a + pltpu.roll(v, 16, 1) * sin_b


def _inproj_kernel(*refs, use_rope, with_q):
    refs = list(refs)
    x_ref, g_ref, sh_ref, sc_ref, wkv_ref = refs[:5]
    del refs[:5]
    wq_ref = refs.pop(0) if with_q else None
    if use_rope:
        cos_ref, sa_ref, sb_ref = refs[:3]
        del refs[:3]
    q_ref = refs.pop(0) if with_q else None
    kvx_ref, h_ref = refs
    j = pl.program_id(1)

    def rope(v):
        if not use_rope:
            return v
        return _rope(v, cos_ref[...], sa_ref[...], sb_ref[...])

    @pl.when(j == 0)
    def _():
        h = _norm_mod(x_ref[...], g_ref[...], sh_ref[...], sc_ref[...]).astype(BF16)
        h_ref[...] = h
        acc = _dot(h, wkv_ref[...])
        kvw = acc.shape[1] // 2
        lo_mask = lax.broadcasted_iota(jnp.int32, (acc.shape[0], LANES), 1) < HEAD_DIM
        for part in range(2):
            for pr in range(kvw // LANES):
                v = acc[:, part * kvw + pr * LANES: part * kvw + (pr + 1) * LANES]
                if part == 0:
                    v = rope(v)
                r = pltpu.roll(v, HEAD_DIM, 1)
                tiles = ((jnp.where(lo_mask, v, 0.0), jnp.where(lo_mask, 0.0, r)),
                         (jnp.where(lo_mask, r, 0.0), jnp.where(lo_mask, 0.0, v)))
                for odd in range(2):
                    head = 2 * pr + odd
                    for hi in range(2):
                        col = head * 4 * LANES + (2 * part + hi) * LANES
                        kvx_ref[:, col:col + LANES] = tiles[odd][hi].astype(BF16)

    if with_q:
        acc = _dot(h_ref[...], wq_ref[...])
        for cg in range(acc.shape[1] // LANES):
            v = rope(acc[:, cg * LANES:(cg + 1) * LANES]) * Q_SCALE
            q_ref[:, cg * LANES:(cg + 1) * LANES] = v.astype(BF16)


def _inproj(xr, norm_g, mods, w_bf, layer, *, tm, row_of_tile, rope_tabs, seq, with_q):
    m, d = xr.shape
    q_w = N_HEADS * HEAD_DIM
    kv_w = N_KV_HEADS * HEAD_DIM
    tnq = _tile(q_w, 512)
    use_rope = rope_tabs is not None
    kvx_w = N_KV_HEADS * 4 * LANES

    in_specs = [
        pl.BlockSpec((tm, d), lambda i, j: (i, 0)),
        pl.BlockSpec((None, 1, d), lambda i, j: (layer, 0, 0)),
        pl.BlockSpec((None, None, None, 1, d), _mod_spec(layer, 0, row_of_tile)),
        pl.BlockSpec((None, None, None, 1, d), _mod_spec(layer, 1, row_of_tile)),
        pl.BlockSpec((None, d, 2 * kv_w), lambda i, j: (layer, 0, q_w // (2 * kv_w))),
    ]
    args = [xr, norm_g, mods, mods, w_bf]
    if with_q:
        in_specs.append(pl.BlockSpec((None, d, tnq), lambda i, j: (layer, 0, j)))
        args.append(w_bf)
    if use_rope:
        tps = seq // tm
        for t in rope_tabs:
            in_specs.append(pl.BlockSpec((tm, LANES), lambda i, j: (i % tps, 0)))
            args.append(t)

    out_specs = [pl.BlockSpec((tm, kvx_w), lambda i, j: (i, 0)),
                 pl.BlockSpec((tm, d), lambda i, j: (i, 0))]
    out_shape = [jax.ShapeDtypeStruct((m, kvx_w), BF16), jax.ShapeDtypeStruct((m, d), BF16)]
    if with_q:
        out_specs.insert(0, pl.BlockSpec((tm, tnq), lambda i, j: (i, j)))
        out_shape.insert(0, jax.ShapeDtypeStruct((m, q_w), BF16))
    out = pl.pallas_call(
        functools.partial(_inproj_kernel, use_rope=use_rope, with_q=with_q),
        grid=(m // tm, q_w // tnq if with_q else 1),
        in_specs=in_specs,
        out_specs=out_specs,
        out_shape=out_shape,
        compiler_params=_params("parallel", "arbitrary"),
        name="inproj" if with_q else "inproj_kv",
    )(*args)
    return out if with_q else (None,) + tuple(out)


def _conv_kernel(hp_ref, h_ref, hn_ref, wb_ref, wc_ref, wx_ref, cw_ref, cb_ref, y_ref, h_scr, *, tm, seq):
    i = pl.program_id(0)
    j = pl.program_id(1)
    halo = BF16_ROWS

    @pl.when(j == 0)
    def _():
        h_scr[0:halo] = hp_ref[...]
        h_scr[halo:halo + tm] = h_ref[...]
        h_scr[halo + tm:] = hn_ref[...]

    hx = h_scr[...]
    u = _dot(hx, wc_ref[...]) * _dot(hx, wx_ref[...])
    rows = tm + 2 * halo
    u_up = pltpu.roll(u, 1, 0)[halo:halo + tm]
    u_dn = pltpu.roll(u, rows - 1, 0)[halo:halo + tm]
    row = lax.broadcasted_iota(jnp.int32, (tm, 1), 0)
    seq_start = (i * tm) % seq == 0
    seq_end = ((i + 1) * tm) % seq == 0
    u_up = jnp.where(jnp.logical_and(row == 0, seq_start), 0.0, u_up)
    u_dn = jnp.where(jnp.logical_and(row == tm - 1, seq_end), 0.0, u_dn)
    w = cw_ref[...]
    conv = u_up * w[0:1] + u[halo:halo + tm] * w[1:2] + u_dn * w[2:3] + cb_ref[...]
    y_ref[...] = (_dot(h_ref[...], wb_ref[...]) * conv).astype(BF16)


def _conv_branch(h, w_bf, conv_w, conv_b, layer, *, tm, tn, seq):
    m, d = h.shape
    d_conv = conv_w.shape[-1]
    halo = BF16_ROWS
    off = (N_HEADS * HEAD_DIM + 2 * N_KV_HEADS * HEAD_DIM) // tn
    nc = d_conv // tn
    hb = tm // halo
    last_hb = m // halo - 1

    def wspec(k):
        return pl.BlockSpec((None, d, tn), lambda i, j: (layer, 0, off + k * nc + j))

    return pl.pallas_call(
        functools.partial(_conv_kernel, tm=tm, seq=seq),
        grid=(m // tm, nc),
        in_specs=[
            pl.BlockSpec((halo, d), lambda i, j: (jnp.maximum(i * hb - 1, 0), 0)),
            pl.BlockSpec((tm, d), lambda i, j: (i, 0)),
            pl.BlockSpec((halo, d), lambda i, j: (jnp.minimum((i + 1) * hb, last_hb), 0)),
            wspec(0), wspec(1), wspec(2),
            pl.BlockSpec((None, 3, tn), lambda i, j: (layer, 0, j)),
            pl.BlockSpec((None, 1, tn), lambda i, j: (layer, 0, j)),
        ],
        out_specs=pl.BlockSpec((tm, tn), lambda i, j: (i, j)),
        out_shape=jax.ShapeDtypeStruct((m, d_conv), BF16),
        scratch_shapes=[pltpu.VMEM((tm + 2 * halo, d), BF16)],
        compiler_params=_params("parallel", "arbitrary"),
        name="conv_branch",
    )(h, h, h, w_bf, w_bf, w_bf, conv_w, conv_b)


def _group_attention(q_ref, rows, pieces, sinks, o_ref):
    blk = ATT_BLOCK
    npair = len(sinks) // 2
    qs = jnp.concatenate([q_ref[rows, jp * LANES:(jp + 1) * LANES] for jp in range(npair)], axis=0)
    probs, vals, inv = [], [], []
    for par in range(2):
        chunks = []
        for kvx, biases in pieces:
            s = _dot_nt(qs, kvx[:, par * LANES:(par + 1) * LANES])
            for ci in range(s.shape[1] // LANES):
                ch = s[:, ci * LANES:(ci + 1) * LANES]
                if biases is not None and biases[ci] is not None:
                    ch = ch + jnp.concatenate([biases[ci]] * npair, axis=0)
                chunks.append(ch)
            vals.append(kvx[:, (2 + par) * LANES:(3 + par) * LANES])
        sink_col = jnp.concatenate([jnp.full((blk, 1), sinks[2 * jp + par], F32) for jp in range(npair)], axis=0)
        mx = jnp.max(functools.reduce(jnp.maximum, chunks), axis=-1, keepdims=True)
        mx = jnp.maximum(mx, sink_col)
        ps = [jnp.exp2(ch - mx) for ch in chunks]
        den = jnp.sum(functools.reduce(jnp.add, ps), axis=-1, keepdims=True) + jnp.exp2(sink_col - mx)
        inv.append(1.0 / den)
        probs.extend(p.astype(BF16) for p in ps)
    out = _dot(jnp.concatenate(probs, axis=1), jnp.concatenate(vals, axis=0))
    lane = lax.broadcasted_iota(jnp.int32, out.shape, 1)
    out = out * jnp.where(lane < HEAD_DIM, inv[0], inv[1])
    for jp in range(npair):
        o_ref[rows, jp * LANES:(jp + 1) * LANES] = out[jp * blk:(jp + 1) * blk].astype(BF16)


def _attn_kernel(sink_ref, q_ref, kvp_ref, kvc_ref, kvn_ref, ctx_ref, o_ref, band_scr, *, tq, nblk_seq):
    i = pl.program_id(1)
    h = pl.program_id(2)
    blk = ATT_BLOCK
    groups = N_HEADS // N_KV_HEADS
    band_scr[0:blk] = kvp_ref[...]
    band_scr[blk:blk + tq] = kvc_ref[...]
    band_scr[blk + tq:] = kvn_ref[...]
    ctx = ctx_ref[...]
    sinks = [sink_ref[h * groups + g] * LOG2E for g in range(groups)]
    r = lax.broadcasted_iota(jnp.int32, (blk, blk), 0)
    c = lax.broadcasted_iota(jnp.int32, (blk, blk), 1)
    bias_prev = jnp.where(c >= r, 0.0, NEG)
    bias_next = jnp.where(c <= r, 0.0, NEG)

    for sb in range(tq // blk):
        r0 = sb * blk
        gblk = i * (tq // blk) + sb
        pen_first = jnp.where(gblk == 0, NEG, 0.0)
        pen_last = jnp.where(gblk == nblk_seq - 1, NEG, 0.0)
        biases = (bias_prev + pen_first, None, bias_next + pen_last)
        band = band_scr[r0:r0 + 3 * blk, :]
        _group_attention(q_ref, pl.ds(r0, blk), ((ctx, None), (band, biases)), sinks, o_ref)


def _attention(q, kvx, kvx_ctx, sink, *, batch, seq, n_ctx, tq):
    m, q_w = q.shape
    blk = ATT_BLOCK
    hw = q_w // N_KV_HEADS
    kw = 4 * LANES
    nb = seq // blk
    tpb = tq // blk

    return pl.pallas_call(
        functools.partial(_attn_kernel, tq=tq, nblk_seq=nb),
        grid=(batch, seq // tq, N_KV_HEADS),
        in_specs=[
            pl.BlockSpec(memory_space=pltpu.SMEM),
            pl.BlockSpec((tq, hw), lambda b, i, h: (b * (seq // tq) + i, h)),
            pl.BlockSpec((blk, kw), lambda b, i, h: (b * nb + jnp.maximum(i * tpb - 1, 0), h)),
            pl.BlockSpec((tq, kw), lambda b, i, h: (b * (seq // tq) + i, h)),
            pl.BlockSpec((blk, kw), lambda b, i, h: (b * nb + jnp.minimum((i + 1) * tpb, nb - 1), h)),
            pl.BlockSpec((n_ctx, kw), lambda b, i, h: (b, h)),
        ],
        out_specs=pl.BlockSpec((tq, hw), lambda b, i, h: (b * (seq // tq) + i, h)),
        out_shape=jax.ShapeDtypeStruct((m, q_w), BF16),
        scratch_shapes=[pltpu.VMEM((tq + 2 * blk, kw), BF16)],
        compiler_params=_params("parallel", "parallel", "arbitrary"),
        name="band_attention",
    )(sink, q, kvx, kvx, kvx, kvx_ctx)


def _ctx_attn_kernel(sink_ref, q_ref, ctx_ref, o_ref, *, n_ctx):
    h = pl.program_id(1)
    groups = N_HEADS // N_KV_HEADS
    ctx = ctx_ref[...]
    sinks = [sink_ref[h * groups + g] * LOG2E for g in range(groups)]
    for r0 in range(0, n_ctx, ATT_BLOCK):
        _group_attention(q_ref, pl.ds(r0, ATT_BLOCK), ((ctx, None),), sinks, o_ref)


def _ctx_attention(q, kvx_ctx, sink, *, batch, n_ctx):
    m, q_w = q.shape
    hw = q_w // N_KV_HEADS
    kw = 4 * LANES
    return pl.pallas_call(
        functools.partial(_ctx_attn_kernel, n_ctx=n_ctx),
        grid=(batch, N_KV_HEADS),
        in_specs=[
            pl.BlockSpec(memory_space=pltpu.SMEM),
            pl.BlockSpec((n_ctx, hw), lambda b, h: (b, h)),
            pl.BlockSpec((n_ctx, kw), lambda b, h: (b, h)),
        ],
        out_specs=pl.BlockSpec((n_ctx, hw), lambda b, h: (b, h)),
        out_shape=jax.ShapeDtypeStruct((m, q_w), BF16),
        compiler_params=_params("parallel", "arbitrary"),
        name="ctx_attention",
    )(sink, q, kvx_ctx)


def _merge_kernel(a_ref, y_ref, h_ref, wa_ref, wc_ref, wga_ref, wgc_ref, m_ref):
    h = h_ref[...]
    m = (_sigmoid(_dot(h, wga_ref[...])) * _dot(a_ref[...], wa_ref[...])
         + _sigmoid(_dot(h, wgc_ref[...])) * _dot(y_ref[...], wc_ref[...]))
    m_ref[...] = m.astype(BF16)


def _merge(attn, y, h, w_in_bf, wa_bf, wc_bf, layer, *, tm, tn):
    m, q_w = attn.shape
    d_conv = y.shape[1]
    d_in = h.shape[1]
    d = wa_bf.shape[-1]
    nn = d // tn
    gate_off = (q_w + 2 * N_KV_HEADS * HEAD_DIM + 3 * d_conv) // tn
    return pl.pallas_call(
        _merge_kernel,
        grid=(m // tm, nn),
        in_specs=[
            pl.BlockSpec((tm, q_w), lambda i, j: (i, 0)),
            pl.BlockSpec((tm, d_conv), lambda i, j: (i, 0)),
            pl.BlockSpec((tm, d_in), lambda i, j: (i, 0)),
            pl.BlockSpec((None, q_w, tn), lambda i, j: (layer, 0, j)),
            pl.BlockSpec((None, d_conv, tn), lambda i, j: (layer, 0, j)),
            pl.BlockSpec((None, d_in, tn), lambda i, j: (layer, 0, gate_off + j)),
            pl.BlockSpec((None, d_in, tn), lambda i, j: (layer, 0, gate_off + nn + j)),
        ],
        out_specs=pl.BlockSpec((tm, tn), lambda i, j: (i, j)),
        out_shape=jax.ShapeDtypeStruct((m, d), BF16),
        compiler_params=_params("parallel", "arbitrary"),
        name="merge",
    )(attn, y, h, wa_bf, wc_bf, w_in_bf, w_in_bf)


def _outproj_kernel(m_ref, w_ref, x_ref, g_ref, o_ref):
    o_ref[...] = x_ref[...] + g_ref[...] * _dot(m_ref[...], w_ref[...])


def _outproj(mm, xr, mods, wo_bf, layer, *, tm, tn, row_of_tile):
    m, d = xr.shape
    return pl.pallas_call(
        _outproj_kernel,
        grid=(m // tm, d // tn),
        in_specs=[
            pl.BlockSpec((tm, d), lambda i, j: (i, 0)),
            pl.BlockSpec((None, d, tn), lambda i, j: (layer, 0, j)),
            pl.BlockSpec((tm, tn), lambda i, j: (i, j)),
            pl.BlockSpec((None, None, None, 1, tn), lambda i, j: (layer, row_of_tile(i), 2, 0, j)),
        ],
        out_specs=pl.BlockSpec((tm, tn), lambda i, j: (i, j)),
        out_shape=jax.ShapeDtypeStruct((m, d), F32),
        compiler_params=_params("parallel", "arbitrary"),
        name="outproj",
    )(mm, wo_bf, xr, mods)


def _ffn_kernel(*refs, final):
    if final:
        x_ref, g_ref, sh_ref, sc_ref, gt_ref, wg_ref, wu_ref, wo_ref, fg_ref, o_ref, h_scr = refs
    else:
        x_ref, g_ref, sh_ref, sc_ref, gt_ref, wg_ref, wu_ref, wo_ref, o_ref, h_scr = refs
    f = pl.program_id(1)
    nf = pl.num_programs(1)

    @pl.when(f == 0)
    def _():
        h_scr[...] = _norm_mod(x_ref[...], g_ref[...], sh_ref[...], sc_ref[...]).astype(BF16)

    def accumulate(first):
        h = h_scr[...]
        gate = _dot(h, wg_ref[...])
        act = (gate * _sigmoid(gate) * _dot(h, wu_ref[...])).astype(BF16)
        d = o_ref.shape[1]
        cw = _tile(d, FFN_OUT_CHUNK)
        for c0 in range(0, d, cw):
            part = _dot(act, wo_ref[:, c0:c0 + cw])
            if first:
                o_ref[:, c0:c0 + cw] = part
            else:
                o_ref[:, c0:c0 + cw] += part

    @pl.when(f == 0)
    def _():
        accumulate(True)

    @pl.when(f > 0)
    def _():
        accumulate(False)

    @pl.when(f == nf - 1)
    def _():
        xn = x_ref[...] + gt_ref[...] * o_ref[...]
        if final:
            xn = xn * lax.rsqrt(jnp.mean(xn * xn, axis=-1, keepdims=True) + EPS) * fg_ref[...]
        o_ref[...] = xn


def _ffn(xr, norm_g, mods, wi_bf, wo_bf, layer, *, tm, tf, row_of_tile, final_g):
    m, d = xr.shape
    d_ff = wo_bf.shape[1]
    nf = d_ff // tf
    final = final_g is not None
    in_specs = [
        pl.BlockSpec((tm, d), lambda i, f: (i, 0)),
        pl.BlockSpec((None, 1, d), lambda i, f: (layer, 0, 0)),
        pl.BlockSpec((None, None, None, 1, d), _mod_spec(layer, 3, row_of_tile)),
        pl.BlockSpec((None, None, None, 1, d), _mod_spec(layer, 4, row_of_tile)),
        pl.BlockSpec((None, None, None, 1, d), _mod_spec(layer, 5, row_of_tile)),
        pl.BlockSpec((None, d, tf), lambda i, f: (layer, 0, f)),
        pl.BlockSpec((None, d, tf), lambda i, f: (layer, 0, nf + f)),
        pl.BlockSpec((None, tf, d), lambda i, f: (layer, f, 0)),
    ]
    args = [xr, norm_g, mods, mods, mods, wi_bf, wi_bf, wo_bf]
    if final:
        in_specs.append(pl.BlockSpec((1, d), lambda i, f: (0, 0)))
        args.append(final_g)
    return pl.pallas_call(
        functools.partial(_ffn_kernel, final=final),
        grid=(m // tm, nf),
        in_specs=in_specs,
        out_specs=pl.BlockSpec((tm, d), lambda i, f: (i, 0)),
        out_shape=jax.ShapeDtypeStruct((m, d), F32),
        scratch_shapes=[pltpu.VMEM((tm, d), BF16)],
        compiler_params=_params("parallel", "arbitrary"),
        name="ffn_final" if final else "ffn",
    )(*args)


def _rope_tables(seq):
    quarter = HEAD_DIM // 4
    t = np.arange(seq)
    pos = np.stack([t // GRID_W, t % GRID_W], axis=1).astype(np.float32)
    d = np.arange(LANES) % HEAD_DIM
    axis = d // (HEAD_DIM // 2)
    first_half = (d // quarter) % 2 == 0
    freqs = jnp.asarray(ROPE_THETA, F32) ** (-jnp.arange(quarter, dtype=F32) / quarter)
    ang = jnp.asarray(pos)[:, axis] * freqs[d % quarter][None, :]
    cos, sin = jnp.cos(ang), jnp.sin(ang)
    fh = jnp.asarray(first_half)[None, :]
    return cos, jnp.where(fh, -sin, 0.0), jnp.where(fh, 0.0, sin)


def _tile(n, pref):
    return pref if n % pref == 0 else n


def kernel(x, c, ctx, c_ctx, ada_w, ada_b, norm1_g, norm2_g, w_in, conv_w, conv_b, sink,
           w_attn_out, w_conv_out, w_o, w_ffn_in, w_ffn_out, final_g):
    batch, seq, d = x.shape
    n_ctx = ctx.shape[1]
    depth = ada_w.shape[0]
    d_ff = w_ffn_out.shape[1]
    assert batch + 1 <= MOD_ROWS and seq % ATT_BLOCK == 0 and n_ctx % ATT_BLOCK == 0

    cc = jnp.zeros((MOD_ROWS, d), F32).at[:batch].set(c).at[batch].set(c_ctx)
    mods = _ada(cc, ada_w, ada_b).reshape(depth, MOD_ROWS, 6, 1, d)

    w_in_bf = w_in.astype(BF16)
    wa_bf = w_attn_out.astype(BF16)
    wc_bf = w_conv_out.astype(BF16)
    wo_bf = w_o.astype(BF16)
    wfi_bf = w_ffn_in.astype(BF16)
    wfo_bf = w_ffn_out.astype(BF16)
    n1 = norm1_g.reshape(depth, 1, d)
    n2 = norm2_g.reshape(depth, 1, d)
    conv_b3 = conv_b.reshape(depth, 1, -1)
    fg = final_g.reshape(1, d)
    rope_tabs = _rope_tables(seq)

    tm = _tile(seq, 1024)
    tm_out = _tile(seq, 512)
    tn = 2 * N_KV_HEADS * HEAD_DIM
    tf = _tile(d_ff, tn)
    tq = _tile(seq, 1024)
    m_ctx = batch * n_ctx
    lat_row = lambda t: (lambda i: i // (seq // t))
    ctx_row = lambda i: batch

    xl = x.reshape(batch * seq, d)
    xc = ctx.reshape(m_ctx, d)
    for l in range(depth):
        last = l == depth - 1
        q_c, kvx_c, h_c = _inproj(xc, n1, mods, w_in_bf, l, tm=m_ctx, row_of_tile=ctx_row, rope_tabs=None,
                                  seq=n_ctx, with_q=not last)
        q, kvx, h = _inproj(xl, n1, mods, w_in_bf, l, tm=tm, row_of_tile=lat_row(tm),
                            rope_tabs=rope_tabs, seq=seq, with_q=True)
        y = _conv_branch(h, w_in_bf, conv_w, conv_b3, l, tm=tm, tn=tn, seq=seq)
        attn = _attention(q, kvx, kvx_c, sink[l], batch=batch, seq=seq, n_ctx=n_ctx, tq=tq)
        mm = _merge(attn, y, h, w_in_bf, wa_bf, wc_bf, l, tm=tm, tn=tn)
        xl = _outproj(mm, xl, mods, wo_bf, l, tm=tm_out, tn=d, row_of_tile=lat_row(tm_out))

        if not last:
            y_c = _conv_branch(h_c, w_in_bf, conv_w, conv_b3, l, tm=n_ctx, tn=tn, seq=n_ctx)
            attn_c = _ctx_attention(q_c, kvx_c, sink[l], batch=batch, n_ctx=n_ctx)
            mm_c = _merge(attn_c, y_c, h_c, w_in_bf, wa_bf, wc_bf, l, tm=m_ctx, tn=tn)
            xc = _outproj(mm_c, xc, mods, wo_bf, l, tm=m_ctx, tn=d, row_of_tile=ctx_row)
            xc = _ffn(xc, n2, mods, wfi_bf, wfo_bf, l, tm=m_ctx, tf=tf, row_of_tile=ctx_row, final_g=None)

        xl = _ffn(xl, n2, mods, wfi_bf, wfo_bf, l, tm=tm, tf=tf, row_of_tile=lat_row(tm),
                  final_g=fg if last else None)
    return xl.reshape(batch, seq, d)
```

```python
import functools

import jax
import jax.numpy as jnp
import numpy as np
from jax import lax
from jax.experimental import pallas as pl
from jax.experimental.pallas import tpu as pltpu

GRID_W = 64
N_HEADS = 32
N_KV_HEADS = 4
HEAD_DIM = 64
WINDOW = 128
ROPE_THETA = 10000.0
EPS = 1e-6
NEG = -1e30
LOG2E = 1.4426950408889634
Q_SCALE = HEAD_DIM ** -0.5 * LOG2E

LANES = 128
BF16_ROWS = 16
ROW_CHUNK = 256
V7X_VMEM_BYTES = 64 * 1024 * 1024
VMEM_LIMIT_BYTES = V7X_VMEM_BYTES * 15 // 16
ATT_BLOCK = 128
MOD_ROWS = 8
FFN_OUT_CHUNK = 512

F32 = jnp.float32
BF16 = jnp.bfloat16


def _params(*sem):
    return pltpu.CompilerParams(dimension_semantics=sem, vmem_limit_bytes=VMEM_LIMIT_BYTES)


def _sigmoid(v):
    return 1.0 / (1.0 + jnp.exp(-v))


def _norm_mod(x, g, shift, scale):
    y = x * lax.rsqrt(jnp.mean(x * x, axis=-1, keepdims=True) + EPS)
    return (y * g) * (1.0 + scale) + shift


def _dot(a, b):
    return jnp.dot(a, b, preferred_element_type=F32)


def _dot_nt(a, b):
    return lax.dot_general(a, b, (((1,), (1,)), ((), ())), preferred_element_type=F32)


def _ada_kernel(c_ref, w_ref, b_ref, o_ref):
    c = c_ref[...]
    s = c * _sigmoid(c)
    o_ref[...] = _dot(s.astype(BF16), w_ref[...].astype(BF16)) + b_ref[...]


def _ada(cc, ada_w, ada_b):
    depth, d, n = ada_w.shape
    tn = _tile(n, 1024)
    return pl.pallas_call(
        _ada_kernel,
        grid=(depth, n // tn),
        in_specs=[
            pl.BlockSpec((MOD_ROWS, d), lambda l, j: (0, 0)),
            pl.BlockSpec((None, d, tn), lambda l, j: (l, 0, j)),
            pl.BlockSpec((None, 1, tn), lambda l, j: (l, 0, j)),
        ],
        out_specs=pl.BlockSpec((None, MOD_ROWS, tn), lambda l, j: (l, 0, j)),
        out_shape=jax.ShapeDtypeStruct((depth, MOD_ROWS, n), F32),
        compiler_params=_params("parallel", "parallel"),
        name="ada_mod",
    )(cc, ada_w, ada_b.reshape(depth, 1, n))


def _mod_spec(layer, which, row_of_tile):
    def index(i, j):
        return (layer, row_of_tile(i), which, 0, 0)
    return index


def _rope(v, cos, sin_a, sin_b):
    return v * cos + pltpu.roll(v, LANES - 16, 1) * sin_a + pltpu.roll(v, 16, 1) * sin_b


def _inproj_kernel(*refs, use_rope, with_q):
    refs = list(refs)
    x_ref, g_ref, sh_ref, sc_ref, wkv_ref = refs[:5]
    del refs[:5]
    wq_ref = refs.pop(0) if with_q else None
    if use_rope:
        cos_ref, sa_ref, sb_ref = refs[:3]
        del refs[:3]
    q_ref = refs.pop(0) if with_q else None
    kvx_ref, h_ref = refs
    j = pl.program_id(1)

    rows = h_ref.shape[0]
    rc = _tile(rows, ROW_CHUNK)

    @pl.when(j == 0)
    def _():
        g, sh, sc = g_ref[...], sh_ref[...], sc_ref[...]
        lo_mask = lax.broadcasted_iota(jnp.int32, (rc, LANES), 1) < HEAD_DIM
        for r0 in range(0, rows, rc):
            rs = slice(r0, r0 + rc)
            h = _norm_mod(x_ref[rs], g, sh, sc).astype(BF16)
            h_ref[rs] = h
            acc = _dot(h, wkv_ref[...])
            kvw = acc.shape[1] // 2
            for part in range(2):
                for pr in range(kvw // LANES):
                    v = acc[:, part * kvw + pr * LANES: part * kvw + (pr + 1) * LANES]
                    if part == 0 and use_rope:
                        v = _rope(v, cos_ref[rs], sa_ref[rs], sb_ref[rs])
                    r = pltpu.roll(v, HEAD_DIM, 1)
                    tiles = ((jnp.where(lo_mask, v, 0.0), jnp.where(lo_mask, 0.0, r)),
                             (jnp.where(lo_mask, r, 0.0), jnp.where(lo_mask, 0.0, v)))
                    for odd in range(2):
                        head = 2 * pr + odd
                        for hi in range(2):
                            col = head * 4 * LANES + (2 * part + hi) * LANES
                            kvx_ref[rs, col:col + LANES] = tiles[odd][hi].astype(BF16)

    if with_q:
        for r0 in range(0, rows, rc):
            acc = _dot(h_ref[r0:r0 + rc], wq_ref[...])
            for cg in range(acc.shape[1] // LANES):
                v = acc[:, cg * LANES:(cg + 1) * LANES]
                if use_rope:
                    v = _rope(v, cos_ref[r0:r0 + rc], sa_ref[r0:r0 + rc], sb_ref[r0:r0 + rc])
                q_ref[r0:r0 + rc, cg * LANES:(cg + 1) * LANES] = (v * Q_SCALE).astype(BF16)


def _inproj(xr, norm_g, mods, w_bf, layer, *, tm, row_of_tile, rope_tabs, seq, with_q):
    m, d = xr.shape
    q_w = N_HEADS * HEAD_DIM
    kv_w = N_KV_HEADS * HEAD_DIM
    tnq = _tile(q_w, 512)
    use_rope = rope_tabs is not None
    kvx_w = N_KV_HEADS * 4 * LANES

    in_specs = [
        pl.BlockSpec((tm, d), lambda i, j: (i, 0)),
        pl.BlockSpec((None, 1, d), lambda i, j: (layer, 0, 0)),
        pl.BlockSpec((None, None, None, 1, d), _mod_spec(layer, 0, row_of_tile)),
        pl.BlockSpec((None, None, None, 1, d), _mod_spec(layer, 1, row_of_tile)),
        pl.BlockSpec((None, d, 2 * kv_w), lambda i, j: (layer, 0, q_w // (2 * kv_w))),
    ]
    args = [xr, norm_g, mods, mods, w_bf]
    if with_q:
        in_specs.append(pl.BlockSpec((None, d, tnq), lambda i, j: (layer, 0, j)))
        args.append(w_bf)
    if use_rope:
        tps = seq // tm
        for t in rope_tabs:
            in_specs.append(pl.BlockSpec((tm, LANES), lambda i, j: (i % tps, 0)))
            args.append(t)

    out_specs = [pl.BlockSpec((tm, kvx_w), lambda i, j: (i, 0)),
                 pl.BlockSpec((tm, d), lambda i, j: (i, 0))]
    out_shape = [jax.ShapeDtypeStruct((m, kvx_w), BF16), jax.ShapeDtypeStruct((m, d), BF16)]
    if with_q:
        out_specs.insert(0, pl.BlockSpec((tm, tnq), lambda i, j: (i, j)))
        out_shape.insert(0, jax.ShapeDtypeStruct((m, q_w), BF16))
    out = pl.pallas_call(
        functools.partial(_inproj_kernel, use_rope=use_rope, with_q=with_q),
        grid=(m // tm, q_w // tnq if with_q else 1),
        in_specs=in_specs,
        out_specs=out_specs,
        out_shape=out_shape,
        compiler_params=_params("parallel", "arbitrary"),
        name="inproj" if with_q else "inproj_kv",
    )(*args)
    return out if with_q else (None,) + tuple(out)


def _conv_kernel(hp_ref, h_ref, hn_ref, wb_ref, wc_ref, wx_ref, cw_ref, cb_ref, y_ref, h_scr, *, tm, seq):
    i = pl.program_id(0)
    j = pl.program_id(1)
    halo = BF16_ROWS

    @pl.when(j == 0)
    def _():
        h_scr[0:halo] = hp_ref[...]
        h_scr[halo:halo + tm] = h_ref[...]
        h_scr[halo + tm:] = hn_ref[...]

    hx = h_scr[...]
    u = _dot(hx, wc_ref[...]) * _dot(hx, wx_ref[...])
    rows = tm + 2 * halo
    u_up = pltpu.roll(u, 1, 0)[halo:halo + tm]
    u_dn = pltpu.roll(u, rows - 1, 0)[halo:halo + tm]
    row = lax.broadcasted_iota(jnp.int32, (tm, 1), 0)
    seq_start = (i * tm) % seq == 0
    seq_end = ((i + 1) * tm) % seq == 0
    u_up = jnp.where(jnp.logical_and(row == 0, seq_start), 0.0, u_up)
    u_dn = jnp.where(jnp.logical_and(row == tm - 1, seq_end), 0.0, u_dn)
    w = cw_ref[...]
    conv = u_up * w[0:1] + u[halo:halo + tm] * w[1:2] + u_dn * w[2:3] + cb_ref[...]
    y_ref[...] = (_dot(h_ref[...], wb_ref[...]) * conv).astype(BF16)


def _conv_branch(h, w_bf, conv_w, conv_b, layer, *, tm, tn, seq):
    m, d = h.shape
    d_conv = conv_w.shape[-1]
    halo = BF16_ROWS
    off = (N_HEADS * HEAD_DIM + 2 * N_KV_HEADS * HEAD_DIM) // tn
    nc = d_conv // tn
    hb = tm // halo
    last_hb = m // halo - 1

    def wspec(k):
        return pl.BlockSpec((None, d, tn), lambda i, j: (layer, 0, off + k * nc + j))

    return pl.pallas_call(
        functools.partial(_conv_kernel, tm=tm, seq=seq),
        grid=(m // tm, nc),
        in_specs=[
            pl.BlockSpec((halo, d), lambda i, j: (jnp.maximum(i * hb - 1, 0), 0)),
            pl.BlockSpec((tm, d), lambda i, j: (i, 0)),
            pl.BlockSpec((halo, d), lambda i, j: (jnp.minimum((i + 1) * hb, last_hb), 0)),
            wspec(0), wspec(1), wspec(2),
            pl.BlockSpec((None, 3, tn), lambda i, j: (layer, 0, j)),
            pl.BlockSpec((None, 1, tn), lambda i, j: (layer, 0, j)),
        ],
        out_specs=pl.BlockSpec((tm, tn), lambda i, j: (i, j)),
        out_shape=jax.ShapeDtypeStruct((m, d_conv), BF16),
        scratch_shapes=[pltpu.VMEM((tm + 2 * halo, d), BF16)],
        compiler_params=_params("parallel", "arbitrary"),
        name="conv_branch",
    )(h, h, h, w_bf, w_bf, w_bf, conv_w, conv_b)


def _group_attention(q_ref, rows, pieces, sinks, ones_sel, o_ref):
    blk = ATT_BLOCK
    npair = len(sinks) // 2
    qs = jnp.concatenate([q_ref[rows, jp * LANES:(jp + 1) * LANES] for jp in range(npair)], axis=0)
    probs, vals, sink_p = [], [], []
    for par in range(2):
        chunks = []
        for kvx, biases in pieces:
            s = _dot_nt(qs, kvx[:, par * LANES:(par + 1) * LANES])
            for ci in range(s.shape[1] // LANES):
                ch = s[:, ci * LANES:(ci + 1) * LANES]
                if biases is not None and biases[ci] is not None:
                    ch = ch + jnp.concatenate([biases[ci]] * npair, axis=0)
                chunks.append(ch)
            vals.append(kvx[:, (2 + par) * LANES:(3 + par) * LANES])
        sink = jnp.concatenate([jnp.full((blk, LANES), sinks[2 * jp + par], F32) for jp in range(npair)], axis=0)
        mx = jnp.maximum(jnp.max(functools.reduce(jnp.maximum, chunks), axis=-1, keepdims=True), sink)
        probs.extend(jnp.exp2(ch - mx).astype(BF16) for ch in chunks)
        sink_p.append(jnp.exp2(sink - mx))
    rhs = jnp.concatenate([jnp.concatenate(vals, axis=0), ones_sel], axis=1)
    out = _dot(jnp.concatenate(probs, axis=1), rhs)
    lane = lax.broadcasted_iota(jnp.int32, (npair * blk, LANES), 1)
    den = out[:, LANES:] + jnp.where(lane < HEAD_DIM, sink_p[0], sink_p[1])
    res = out[:, :LANES] / den
    for jp in range(npair):
        o_ref[rows, jp * LANES:(jp + 1) * LANES] = res[jp * blk:(jp + 1) * blk].astype(BF16)


def _ones_selector(n):
    row = lax.broadcasted_iota(jnp.int32, (2 * n, LANES), 0)
    lane = lax.broadcasted_iota(jnp.int32, (2 * n, LANES), 1)
    return jnp.where((row < n) == (lane < HEAD_DIM), 1.0, 0.0).astype(BF16)


def _attn_kernel(sink_ref, q_ref, kvp_ref, kvc_ref, kvn_ref, ctx_ref, o_ref, band_scr, *, tq, nblk_seq):
    i = pl.program_id(1)
    h = pl.program_id(2)
    blk = ATT_BLOCK
    groups = N_HEADS // N_KV_HEADS
    band_scr[0:blk] = kvp_ref[...]
    band_scr[blk:blk + tq] = kvc_ref[...]
    band_scr[blk + tq:] = kvn_ref[...]
    ctx = ctx_ref[...]
    sinks = [sink_ref[h * groups + g] * LOG2E for g in range(groups)]
    r = lax.broadcasted_iota(jnp.int32, (blk, blk), 0)
    c = lax.broadcasted_iota(jnp.int32, (blk, blk), 1)
    bias_prev = jnp.where(c >= r, 0.0, NEG)
    bias_next = jnp.where(c <= r, 0.0, NEG)
    ones_sel = _ones_selector(ctx.shape[0] + 3 * blk)

    for sb in range(tq // blk):
        r0 = sb * blk
        gblk = i * (tq // blk) + sb
        pen_first = jnp.where(gblk == 0, NEG, 0.0)
        pen_last = jnp.where(gblk == nblk_seq - 1, NEG, 0.0)
        biases = (bias_prev + pen_first, None, bias_next + pen_last)
        band = band_scr[r0:r0 + 3 * blk, :]
        _group_attention(q_ref, pl.ds(r0, blk), ((ctx, None), (band, biases)), sinks, ones_sel, o_ref)


def _attention(q, kvx, kvx_ctx, sink, *, batch, seq, n_ctx, tq):
    m, q_w = q.shape
    blk = ATT_BLOCK
    hw = q_w // N_KV_HEADS
    kw = 4 * LANES
    nb = seq // blk
    tpb = tq // blk

    return pl.pallas_call(
        functools.partial(_attn_kernel, tq=tq, nblk_seq=nb),
        grid=(batch, seq // tq, N_KV_HEADS),
        in_specs=[
            pl.BlockSpec(memory_space=pltpu.SMEM),
            pl.BlockSpec((tq, hw), lambda b, i, h: (b * (seq // tq) + i, h)),
            pl.BlockSpec((blk, kw), lambda b, i, h: (b * nb + jnp.maximum(i * tpb - 1, 0), h)),
            pl.BlockSpec((tq, kw), lambda b, i, h: (b * (seq // tq) + i, h)),
            pl.BlockSpec((blk, kw), lambda b, i, h: (b * nb + jnp.minimum((i + 1) * tpb, nb - 1), h)),
            pl.BlockSpec((n_ctx, kw), lambda b, i, h: (b, h)),
        ],
        out_specs=pl.BlockSpec((tq, hw), lambda b, i, h: (b * (seq // tq) + i, h)),
        out_shape=jax.ShapeDtypeStruct((m, q_w), BF16),
        scratch_shapes=[pltpu.VMEM((tq + 2 * blk, kw), BF16)],
        compiler_params=_params("parallel", "parallel", "arbitrary"),
        name="band_attention",
    )(sink, q, kvx, kvx, kvx, kvx_ctx)


def _ctx_attn_kernel(sink_ref, q_ref, ctx_ref, o_ref, *, n_ctx):
    h = pl.program_id(1)
    groups = N_HEADS // N_KV_HEADS
    ctx = ctx_ref[...]
    sinks = [sink_ref[h * groups + g] * LOG2E for g in range(groups)]
    ones_sel = _ones_selector(n_ctx)
    for r0 in range(0, n_ctx, ATT_BLOCK):
        _group_attention(q_ref, pl.ds(r0, ATT_BLOCK), ((ctx, None),), sinks, ones_sel, o_ref)


def _ctx_attention(q, kvx_ctx, sink, *, batch, n_ctx):
    m, q_w = q.shape
    hw = q_w // N_KV_HEADS
    kw = 4 * LANES
    return pl.pallas_call(
        functools.partial(_ctx_attn_kernel, n_ctx=n_ctx),
        grid=(batch, N_KV_HEADS),
        in_specs=[
            pl.BlockSpec(memory_space=pltpu.SMEM),
            pl.BlockSpec((n_ctx, hw), lambda b, h: (b, h)),
            pl.BlockSpec((n_ctx, kw), lambda b, h: (b, h)),
        ],
        out_specs=pl.BlockSpec((n_ctx, hw), lambda b, h: (b, h)),
        out_shape=jax.ShapeDtypeStruct((m, q_w), BF16),
        compiler_params=_params("parallel", "arbitrary"),
        name="ctx_attention",
    )(sink, q, kvx_ctx)


def _merge_kernel(a_ref, y_ref, h_ref, wa_ref, wc_ref, wga_ref, wgc_ref, m_ref):
    h = h_ref[...]
    m = (_sigmoid(_dot(h, wga_ref[...])) * _dot(a_ref[...], wa_ref[...])
         + _sigmoid(_dot(h, wgc_ref[...])) * _dot(y_ref[...], wc_ref[...]))
    m_ref[...] = m.astype(BF16)


def _merge(attn, y, h, w_in_bf, wa_bf, wc_bf, layer, *, tm, tn):
    m, q_w = attn.shape
    d_conv = y.shape[1]
    d_in = h.shape[1]
    d = wa_bf.shape[-1]
    nn = d // tn
    gate_off = (q_w + 2 * N_KV_HEADS * HEAD_DIM + 3 * d_conv) // tn
    return pl.pallas_call(
        _merge_kernel,
        grid=(m // tm, nn),
        in_specs=[
            pl.BlockSpec((tm, q_w), lambda i, j: (i, 0)),
            pl.BlockSpec((tm, d_conv), lambda i, j: (i, 0)),
            pl.BlockSpec((tm, d_in), lambda i, j: (i, 0)),
            pl.BlockSpec((None, q_w, tn), lambda i, j: (layer, 0, j)),
            pl.BlockSpec((None, d_conv, tn), lambda i, j: (layer, 0, j)),
            pl.BlockSpec((None, d_in, tn), lambda i, j: (layer, 0, gate_off + j)),
            pl.BlockSpec((None, d_in, tn), lambda i, j: (layer, 0, gate_off + nn + j)),
        ],
        out_specs=pl.BlockSpec((tm, tn), lambda i, j: (i, j)),
        out_shape=jax.ShapeDtypeStruct((m, d), BF16),
        compiler_params=_params("parallel", "arbitrary"),
        name="merge",
    )(attn, y, h, wa_bf, wc_bf, w_in_bf, w_in_bf)


def _outproj_kernel(m_ref, w_ref, x_ref, g_ref, o_ref):
    o_ref[...] = x_ref[...] + g_ref[...] * _dot(m_ref[...], w_ref[...])


def _outproj(mm, xr, mods, wo_bf, layer, *, tm, tn, row_of_tile):
    m, d = xr.shape
    return pl.pallas_call(
        _outproj_kernel,
        grid=(m // tm, d // tn),
        in_specs=[
            pl.BlockSpec((tm, d), lambda i, j: (i, 0)),
            pl.BlockSpec((None, d, tn), lambda i, j: (layer, 0, j)),
            pl.BlockSpec((tm, tn), lambda i, j: (i, j)),
            pl.BlockSpec((None, None, None, 1, tn), lambda i, j: (layer, row_of_tile(i), 2, 0, j)),
        ],
        out_specs=pl.BlockSpec((tm, tn), lambda i, j: (i, j)),
        out_shape=jax.ShapeDtypeStruct((m, d), F32),
        compiler_params=_params("parallel", "arbitrary"),
        name="outproj",
    )(mm, wo_bf, xr, mods)


def _ffn_kernel(*refs, final):
    if final:
        x_ref, g_ref, sh_ref, sc_ref, gt_ref, wg_ref, wu_ref, wo_ref, fg_ref, o_ref, h_scr = refs
    else:
        x_ref, g_ref, sh_ref, sc_ref, gt_ref, wg_ref, wu_ref, wo_ref, o_ref, h_scr = refs
    f = pl.program_id(1)
    nf = pl.num_programs(1)

    @pl.when(f == 0)
    def _():
        h_scr[...] = _norm_mod(x_ref[...], g_ref[...], sh_ref[...], sc_ref[...]).astype(BF16)

    def accumulate(first):
        h = h_scr[...]
        gate = _dot(h, wg_ref[...])
        act = (gate * _sigmoid(gate) * _dot(h, wu_ref[...])).astype(BF16)
        d = o_ref.shape[1]
        cw = _tile(d, FFN_OUT_CHUNK)
        for c0 in range(0, d, cw):
            part = _dot(act, wo_ref[:, c0:c0 + cw])
            if first:
                o_ref[:, c0:c0 + cw] = part
            else:
                o_ref[:, c0:c0 + cw] += part

    @pl.when(f == 0)
    def _():
        accumulate(True)

    @pl.when(f > 0)
    def _():
        accumulate(False)

    @pl.when(f == nf - 1)
    def _():
        xn = x_ref[...] + gt_ref[...] * o_ref[...]
        if final:
            xn = xn * lax.rsqrt(jnp.mean(xn * xn, axis=-1, keepdims=True) + EPS) * fg_ref[...]
        o_ref[...] = xn


def _ffn(xr, norm_g, mods, wi_bf, wo_bf, layer, *, tm, tf, row_of_tile, final_g):
    m, d = xr.shape
    d_ff = wo_bf.shape[1]
    nf = d_ff // tf
    final = final_g is not None
    in_specs = [
        pl.BlockSpec((tm, d), lambda i, f: (i, 0)),
        pl.BlockSpec((None, 1, d), lambda i, f: (layer, 0, 0)),
        pl.BlockSpec((None, None, None, 1, d), _mod_spec(layer, 3, row_of_tile)),
        pl.BlockSpec((None, None, None, 1, d), _mod_spec(layer, 4, row_of_tile)),
        pl.BlockSpec((None, None, None, 1, d), _mod_spec(layer, 5, row_of_tile)),
        pl.BlockSpec((None, d, tf), lambda i, f: (layer, 0, f)),
        pl.BlockSpec((None, d, tf), lambda i, f: (layer, 0, nf + f)),
        pl.BlockSpec((None, tf, d), lambda i, f: (layer, f, 0)),
    ]
    args = [xr, norm_g, mods, mods, mods, wi_bf, wi_bf, wo_bf]
    if final:
        in_specs.append(pl.BlockSpec((1, d), lambda i, f: (0, 0)))
        args.append(final_g)
    return pl.pallas_call(
        functools.partial(_ffn_kernel, final=final),
        grid=(m // tm, nf),
        in_specs=in_specs,
        out_specs=pl.BlockSpec((tm, d), lambda i, f: (i, 0)),
        out_shape=jax.ShapeDtypeStruct((m, d), F32),
        scratch_shapes=[pltpu.VMEM((tm, d), BF16)],
        compiler_params=_params("parallel", "arbitrary"),
        name="ffn_final" if final else "ffn",
    )(*args)


def _rope_tables(seq):
    quarter = HEAD_DIM // 4
    t = np.arange(seq)
    pos = np.stack([t // GRID_W, t % GRID_W], axis=1).astype(np.float32)
    d = np.arange(LANES) % HEAD_DIM
    axis = d // (HEAD_DIM // 2)
    first_half = (d // quarter) % 2 == 0
    freqs = jnp.asarray(ROPE_THETA, F32) ** (-jnp.arange(quarter, dtype=F32) / quarter)
    ang = jnp.asarray(pos)[:, axis] * freqs[d % quarter][None, :]
    cos, sin = jnp.cos(ang), jnp.sin(ang)
    fh = jnp.asarray(first_half)[None, :]
    return cos, jnp.where(fh, -sin, 0.0), jnp.where(fh, 0.0, sin)


def _tile(n, pref):
    return pref if n % pref == 0 else n


def kernel(x, c, ctx, c_ctx, ada_w, ada_b, norm1_g, norm2_g, w_in, conv_w, conv_b, sink,
           w_attn_out, w_conv_out, w_o, w_ffn_in, w_ffn_out, final_g):
    batch, seq, d = x.shape
    n_ctx = ctx.shape[1]
    depth = ada_w.shape[0]
    d_ff = w_ffn_out.shape[1]
    assert batch + 1 <= MOD_ROWS and seq % ATT_BLOCK == 0 and n_ctx % ATT_BLOCK == 0

    cc = jnp.zeros((MOD_ROWS, d), F32).at[:batch].set(c).at[batch].set(c_ctx)
    mods = _ada(cc, ada_w, ada_b).reshape(depth, MOD_ROWS, 6, 1, d)

    w_in_bf = w_in.astype(BF16)
    wa_bf = w_attn_out.astype(BF16)
    wc_bf = w_conv_out.astype(BF16)
    wo_bf = w_o.astype(BF16)
    wfi_bf = w_ffn_in.astype(BF16)
    wfo_bf = w_ffn_out.astype(BF16)
    n1 = norm1_g.reshape(depth, 1, d)
    n2 = norm2_g.reshape(depth, 1, d)
    conv_b3 = conv_b.reshape(depth, 1, -1)
    fg = final_g.reshape(1, d)
    rope_tabs = _rope_tables(seq)

    tm = _tile(seq, 1024)
    tm_out = _tile(seq, 512)
    tn = 2 * N_KV_HEADS * HEAD_DIM
    tf = _tile(d_ff, tn)
    tq = _tile(seq, 2048)
    m_ctx = batch * n_ctx
    lat_row = lambda t: (lambda i: i // (seq // t))
    ctx_row = lambda i: batch

    xl = x.reshape(batch * seq, d)
    xc = ctx.reshape(m_ctx, d)
    for l in range(depth):
        last = l == depth - 1
        q_c, kvx_c, h_c = _inproj(xc, n1, mods, w_in_bf, l, tm=m_ctx, row_of_tile=ctx_row, rope_tabs=None,
                                  seq=n_ctx, with_q=not last)
        q, kvx, h = _inproj(xl, n1, mods, w_in_bf, l, tm=tm, row_of_tile=lat_row(tm),
                            rope_tabs=rope_tabs, seq=seq, with_q=True)
        y = _conv_branch(h, w_in_bf, conv_w, conv_b3, l, tm=tm, tn=tn, seq=seq)
        attn = _attention(q, kvx, kvx_c, sink[l], batch=batch, seq=seq, n_ctx=n_ctx, tq=tq)
        mm = _merge(attn, y, h, w_in_bf, wa_bf, wc_bf, l, tm=tm, tn=tn)
        xl = _outproj(mm, xl, mods, wo_bf, l, tm=tm_out, tn=d, row_of_tile=lat_row(tm_out))

        if not last:
            y_c = _conv_branch(h_c, w_in_bf, conv_w, conv_b3, l, tm=n_ctx, tn=tn, seq=n_ctx)
            attn_c = _ctx_attention(q_c, kvx_c, sink[l], batch=batch, n_ctx=n_ctx)
            mm_c = _merge(attn_c, y_c, h_c, w_in_bf, wa_bf, wc_bf, l, tm=m_ctx, tn=tn)
            xc = _outproj(mm_c, xc, mods, wo_bf, l, tm=m_ctx, tn=d, row_of_tile=ctx_row)
            xc = _ffn(xc, n2, mods, wfi_bf, wfo_bf, l, tm=m_ctx, tf=tf, row_of_tile=ctx_row, final_g=None)

        xl = _ffn(xl, n2, mods, wfi_bf, wfo_bf, l, tm=tm, tf=tf, row_of_tile=lat_row(tm),
                  final_g=fg if last else None)
    return xl.reshape(batch, seq, d)
```

```python
import functools

import jax
import jax.numpy as jnp
import numpy as np
from jax import lax
from jax.experimental import pallas as pl
from jax.experimental.pallas import tpu as pltpu

GRID_W = 64
N_HEADS = 32
N_KV_HEADS = 4
HEAD_DIM = 64
WINDOW = 128
ROPE_THETA = 10000.0
EPS = 1e-6
NEG = -1e30
LOG2E = 1.4426950408889634
Q_SCALE = HEAD_DIM ** -0.5 * LOG2E

LANES = 128
BF16_ROWS = 16
ROW_CHUNK = 256
V7X_VMEM_BYTES = 64 * 1024 * 1024
VMEM_LIMIT_BYTES = V7X_VMEM_BYTES * 15 // 16
ATT_BLOCK = 128
MOD_ROWS = 8
FFN_OUT_CHUNK = 512

F32 = jnp.float32
BF16 = jnp.bfloat16


def _params(*sem):
    return pltpu.CompilerParams(dimension_semantics=sem, vmem_limit_bytes=VMEM_LIMIT_BYTES)


def _sigmoid(v):
    return 1.0 / (1.0 + jnp.exp(-v))


def _norm_mod(x, g, shift, scale):
    y = x * lax.rsqrt(jnp.mean(x * x, axis=-1, keepdims=True) + EPS)
    return (y * g) * (1.0 + scale) + shift


def _dot(a, b):
    return jnp.dot(a, b, preferred_element_type=F32)


def _dot_nt(a, b):
    return lax.dot_general(a, b, (((1,), (1,)), ((), ())), preferred_element_type=F32)


def _ada_kernel(c_ref, w_ref, b_ref, o_ref):
    c = c_ref[...]
    s = c * _sigmoid(c)
    o_ref[...] = _dot(s.astype(BF16), w_ref[...].astype(BF16)) + b_ref[...]


def _ada(cc, ada_w, ada_b):
    depth, d, n = ada_w.shape
    tn = _tile(n, 1024)
    return pl.pallas_call(
        _ada_kernel,
        grid=(depth, n // tn),
        in_specs=[
            pl.BlockSpec((MOD_ROWS, d), lambda l, j: (0, 0)),
            pl.BlockSpec((None, d, tn), lambda l, j: (l, 0, j)),
            pl.BlockSpec((None, 1, tn), lambda l, j: (l, 0, j)),
        ],
        out_specs=pl.BlockSpec((None, MOD_ROWS, tn), lambda l, j: (l, 0, j)),
        out_shape=jax.ShapeDtypeStruct((depth, MOD_ROWS, n), F32),
        compiler_params=_params("parallel", "parallel"),
        name="ada_mod",
    )(cc, ada_w, ada_b.reshape(depth, 1, n))


def _mod_spec(layer, which, row_of_tile):
    def index(i, j):
        return (layer, row_of_tile(i), which, 0, 0)
    return index


def _rope(v, cos, sin_a, sin_b):
    return v * cos + pltpu.roll(v, LANES - 16, 1) * sin_a + pltpu.roll(v, 16, 1) * sin_b


def _inproj_kernel(*refs, use_rope, with_q, nx):
    refs = list(refs)
    x_refs = refs[:nx]
    del refs[:nx]
    g_ref, sh_ref, sc_ref, wkv_ref = refs[:4]
    del refs[:4]
    wq_ref = refs.pop(0) if with_q else None
    if use_rope:
        cos_ref, sa_ref, sb_ref = refs[:3]
        del refs[:3]
    q_ref = refs.pop(0) if with_q else None
    kvx_ref, h_ref = refs
    j = pl.program_id(1)

    rows = h_ref.shape[0]
    rc = _tile(rows, ROW_CHUNK)

    @pl.when(j == 0)
    def _():
        g, sh, sc = g_ref[...], sh_ref[...], sc_ref[...]
        lo_mask = lax.broadcasted_iota(jnp.int32, (rc, LANES), 1) < HEAD_DIM
        for ci, r0 in enumerate(range(0, rows, rc)):
            rs = slice(r0, r0 + rc)
            h = _norm_mod(x_refs[ci][...], g, sh, sc).astype(BF16)
            h_ref[rs] = h
            acc = _dot(h, wkv_ref[...])
            kvw = acc.shape[1] // 2
            for part in range(2):
                for pr in range(kvw // LANES):
                    v = acc[:, part * kvw + pr * LANES: part * kvw + (pr + 1) * LANES]
                    if part == 0 and use_rope:
                        v = _rope(v, cos_ref[rs], sa_ref[rs], sb_ref[rs])
                    r = pltpu.roll(v, HEAD_DIM, 1)
                    tiles = ((jnp.where(lo_mask, v, 0.0), jnp.where(lo_mask, 0.0, r)),
                             (jnp.where(lo_mask, r, 0.0), jnp.where(lo_mask, 0.0, v)))
                    for odd in range(2):
                        head = 2 * pr + odd
                        for hi in range(2):
                            col = head * 4 * LANES + (2 * part + hi) * LANES
                            kvx_ref[rs, col:col + LANES] = tiles[odd][hi].astype(BF16)

    if with_q:
        for r0 in range(0, rows, rc):
            acc = _dot(h_ref[r0:r0 + rc], wq_ref[...])
            for cg in range(acc.shape[1] // LANES):
                v = acc[:, cg * LANES:(cg + 1) * LANES]
                if use_rope:
                    v = _rope(v, cos_ref[r0:r0 + rc], sa_ref[r0:r0 + rc], sb_ref[r0:r0 + rc])
                q_ref[r0:r0 + rc, cg * LANES:(cg + 1) * LANES] = (v * Q_SCALE).astype(BF16)


def _inproj(xr, norm_g, mods, w_bf, layer, *, tm, row_of_tile, rope_tabs, seq, with_q):
    m, d = xr.shape
    q_w = N_HEADS * HEAD_DIM
    kv_w = N_KV_HEADS * HEAD_DIM
    tnq = _tile(q_w, 512)
    use_rope = rope_tabs is not None
    kvx_w = N_KV_HEADS * 4 * LANES

    rc = _tile(tm, ROW_CHUNK)
    nx = tm // rc
    last_tile = m // tm - 1

    def x_spec(k):
        def index(i, j):
            return (jnp.minimum(i + (j > k).astype(jnp.int32), last_tile) * nx + k, 0)
        return pl.BlockSpec((rc, d), index)

    in_specs = [x_spec(k) for k in range(nx)] + [
        pl.BlockSpec((None, 1, d), lambda i, j: (layer, 0, 0)),
        pl.BlockSpec((None, None, None, 1, d), _mod_spec(layer, 0, row_of_tile)),
        pl.BlockSpec((None, None, None, 1, d), _mod_spec(layer, 1, row_of_tile)),
        pl.BlockSpec((None, d, 2 * kv_w), lambda i, j: (layer, 0, q_w // (2 * kv_w))),
    ]
    args = [xr] * nx + [norm_g, mods, mods, w_bf]
    if with_q:
        in_specs.append(pl.BlockSpec((None, d, tnq), lambda i, j: (layer, 0, j)))
        args.append(w_bf)
    if use_rope:
        tps = seq // tm
        for t in rope_tabs:
            in_specs.append(pl.BlockSpec((tm, LANES), lambda i, j: (i % tps, 0)))
            args.append(t)

    out_specs = [pl.BlockSpec((tm, kvx_w), lambda i, j: (i, 0)),
                 pl.BlockSpec((tm, d), lambda i, j: (i, 0))]
    out_shape = [jax.ShapeDtypeStruct((m, kvx_w), BF16), jax.ShapeDtypeStruct((m, d), BF16)]
    if with_q:
        out_specs.insert(0, pl.BlockSpec((tm, tnq), lambda i, j: (i, j)))
        out_shape.insert(0, jax.ShapeDtypeStruct((m, q_w), BF16))
    out = pl.pallas_call(
        functools.partial(_inproj_kernel, use_rope=use_rope, with_q=with_q, nx=nx),
        grid=(m // tm, q_w // tnq if with_q else 1),
        in_specs=in_specs,
        out_specs=out_specs,
        out_shape=out_shape,
        compiler_params=_params("parallel", "arbitrary"),
        name="inproj" if with_q else "inproj_kv",
    )(*args)
    return out if with_q else (None,) + tuple(out)


def _conv_kernel(hp_ref, h_ref, hn_ref, wb_ref, wc_ref, wx_ref, cw_ref, cb_ref, y_ref, h_scr, *, tm, seq):
    i = pl.program_id(0)
    j = pl.program_id(1)
    halo = BF16_ROWS

    @pl.when(j == 0)
    def _():
        h_scr[0:halo] = hp_ref[...]
        h_scr[halo:halo + tm] = h_ref[...]
        h_scr[halo + tm:] = hn_ref[...]

    hx = h_scr[...]
    u = _dot(hx, wc_ref[...]) * _dot(hx, wx_ref[...])
    rows = tm + 2 * halo
    u_up = pltpu.roll(u, 1, 0)[halo:halo + tm]
    u_dn = pltpu.roll(u, rows - 1, 0)[halo:halo + tm]
    row = lax.broadcasted_iota(jnp.int32, (tm, 1), 0)
    seq_start = (i * tm) % seq == 0
    seq_end = ((i + 1) * tm) % seq == 0
    u_up = jnp.where(jnp.logical_and(row == 0, seq_start), 0.0, u_up)
    u_dn = jnp.where(jnp.logical_and(row == tm - 1, seq_end), 0.0, u_dn)
    w = cw_ref[...]
    conv = u_up * w[0:1] + u[halo:halo + tm] * w[1:2] + u_dn * w[2:3] + cb_ref[...]
    y_ref[...] = (_dot(h_ref[...], wb_ref[...]) * conv).astype(BF16)


def _conv_branch(h, w_bf, conv_w, conv_b, layer, *, tm, tn, seq):
    m, d = h.shape
    d_conv = conv_w.shape[-1]
    halo = BF16_ROWS
    off = (N_HEADS * HEAD_DIM + 2 * N_KV_HEADS * HEAD_DIM) // tn
    nc = d_conv // tn
    hb = tm // halo
    last_hb = m // halo - 1

    def wspec(k):
        return pl.BlockSpec((None, d, tn), lambda i, j: (layer, 0, off + k * nc + j))

    return pl.pallas_call(
        functools.partial(_conv_kernel, tm=tm, seq=seq),
        grid=(m // tm, nc),
        in_specs=[
            pl.BlockSpec((halo, d), lambda i, j: (jnp.maximum(i * hb - 1, 0), 0)),
            pl.BlockSpec((tm, d), lambda i, j: (i, 0)),
            pl.BlockSpec((halo, d), lambda i, j: (jnp.minimum((i + 1) * hb, last_hb), 0)),
            wspec(0), wspec(1), wspec(2),
            pl.BlockSpec((None, 3, tn), lambda i, j: (layer, 0, j)),
            pl.BlockSpec((None, 1, tn), lambda i, j: (layer, 0, j)),
        ],
        out_specs=pl.BlockSpec((tm, tn), lambda i, j: (i, j)),
        out_shape=jax.ShapeDtypeStruct((m, d_conv), BF16),
        scratch_shapes=[pltpu.VMEM((tm + 2 * halo, d), BF16)],
        compiler_params=_params("parallel", "arbitrary"),
        name="conv_branch",
    )(h, h, h, w_bf, w_bf, w_bf, conv_w, conv_b)


def _group_attention(q_ref, rows, pieces, sinks, ones_sel, o_ref):
    blk = ATT_BLOCK
    npair = len(sinks) // 2
    qs = jnp.concatenate([q_ref[rows, jp * LANES:(jp + 1) * LANES] for jp in range(npair)], axis=0)
    probs, vals, sink_p = [], [], []
    for par in range(2):
        chunks = []
        for kvx, biases in pieces:
            s = _dot_nt(qs, kvx[:, par * LANES:(par + 1) * LANES])
            for ci in range(s.shape[1] // LANES):
                ch = s[:, ci * LANES:(ci + 1) * LANES]
                if biases is not None and biases[ci] is not None:
                    ch = ch + jnp.concatenate([biases[ci]] * npair, axis=0)
                chunks.append(ch)
            vals.append(kvx[:, (2 + par) * LANES:(3 + par) * LANES])
        sink = jnp.concatenate([jnp.full((blk, LANES), sinks[2 * jp + par], F32) for jp in range(npair)], axis=0)
        mx = jnp.maximum(jnp.max(functools.reduce(jnp.maximum, chunks), axis=-1, keepdims=True), sink)
        probs.extend(jnp.exp2(ch - mx).astype(BF16) for ch in chunks)
        sink_p.append(jnp.exp2(sink - mx))
    rhs = jnp.concatenate([jnp.concatenate(vals, axis=0), ones_sel], axis=1)
    out = _dot(jnp.concatenate(probs, axis=1), rhs)
    lane = lax.broadcasted_iota(jnp.int32, (npair * blk, LANES), 1)
    den = out[:, LANES:] + jnp.where(lane < HEAD_DIM, sink_p[0], sink_p[1])
    res = out[:, :LANES] / den
    for jp in range(npair):
        o_ref[rows, jp * LANES:(jp + 1) * LANES] = res[jp * blk:(jp + 1) * blk].astype(BF16)


def _ones_selector(n):
    row = lax.broadcasted_iota(jnp.int32, (2 * n, LANES), 0)
    lane = lax.broadcasted_iota(jnp.int32, (2 * n, LANES), 1)
    return jnp.where((row < n) == (lane < HEAD_DIM), 1.0, 0.0).astype(BF16)


def _attn_kernel(sink_ref, q_ref, kvp_ref, kvc_ref, kvn_ref, ctx_ref, o_ref, band_scr, *, tq, nblk_seq):
    i = pl.program_id(1)
    h = pl.program_id(2)
    blk = ATT_BLOCK
    groups = N_HEADS // N_KV_HEADS
    band_scr[0:blk] = kvp_ref[...]
    band_scr[blk:blk + tq] = kvc_ref[...]
    band_scr[blk + tq:] = kvn_ref[...]
    ctx = ctx_ref[...]
    sinks = [sink_ref[h * groups + g] * LOG2E for g in range(groups)]
    r = lax.broadcasted_iota(jnp.int32, (blk, blk), 0)
    c = lax.broadcasted_iota(jnp.int32, (blk, blk), 1)
    bias_prev = jnp.where(c >= r, 0.0, NEG)
    bias_next = jnp.where(c <= r, 0.0, NEG)
    ones_sel = _ones_selector(ctx.shape[0] + 3 * blk)

    for sb in range(tq // blk):
        r0 = sb * blk
        gblk = i * (tq // blk) + sb
        pen_first = jnp.where(gblk == 0, NEG, 0.0)
        pen_last = jnp.where(gblk == nblk_seq - 1, NEG, 0.0)
        biases = (bias_prev + pen_first, None, bias_next + pen_last)
        band = band_scr[r0:r0 + 3 * blk, :]
        _group_attention(q_ref, pl.ds(r0, blk), ((ctx, None), (band, biases)), sinks, ones_sel, o_ref)


def _attention(q, kvx, kvx_ctx, sink, *, batch, seq, n_ctx, tq):
    m, q_w = q.shape
    blk = ATT_BLOCK
    hw = q_w // N_KV_HEADS
    kw = 4 * LANES
    nb = seq // blk
    tpb = tq // blk

    return pl.pallas_call(
        functools.partial(_attn_kernel, tq=tq, nblk_seq=nb),
        grid=(batch, seq // tq, N_KV_HEADS),
        in_specs=[
            pl.BlockSpec(memory_space=pltpu.SMEM),
            pl.BlockSpec((tq, hw), lambda b, i, h: (b * (seq // tq) + i, h)),
            pl.BlockSpec((blk, kw), lambda b, i, h: (b * nb + jnp.maximum(i * tpb - 1, 0), h)),
            pl.BlockSpec((tq, kw), lambda b, i, h: (b * (seq // tq) + i, h)),
            pl.BlockSpec((blk, kw), lambda b, i, h: (b * nb + jnp.minimum((i + 1) * tpb, nb - 1), h)),
            pl.BlockSpec((n_ctx, kw), lambda b, i, h: (b, h)),
        ],
        out_specs=pl.BlockSpec((tq, hw), lambda b, i, h: (b * (seq // tq) + i, h)),
        out_shape=jax.ShapeDtypeStruct((m, q_w), BF16),
        scratch_shapes=[pltpu.VMEM((tq + 2 * blk, kw), BF16)],
        compiler_params=_params("parallel", "parallel", "arbitrary"),
        name="band_attention",
    )(sink, q, kvx, kvx, kvx, kvx_ctx)


def _ctx_attn_kernel(sink_ref, q_ref, ctx_ref, o_ref, *, n_ctx):
    h = pl.program_id(1)
    groups = N_HEADS // N_KV_HEADS
    ctx = ctx_ref[...]
    sinks = [sink_ref[h * groups + g] * LOG2E for g in range(groups)]
    ones_sel = _ones_selector(n_ctx)
    for r0 in range(0, n_ctx, ATT_BLOCK):
        _group_attention(q_ref, pl.ds(r0, ATT_BLOCK), ((ctx, None),), sinks, ones_sel, o_ref)


def _ctx_attention(q, kvx_ctx, sink, *, batch, n_ctx):
    m, q_w = q.shape
    hw = q_w // N_KV_HEADS
    kw = 4 * LANES
    return pl.pallas_call(
        functools.partial(_ctx_attn_kernel, n_ctx=n_ctx),
        grid=(batch, N_KV_HEADS),
        in_specs=[
            pl.BlockSpec(memory_space=pltpu.SMEM),
            pl.BlockSpec((n_ctx, hw), lambda b, h: (b, h)),
            pl.BlockSpec((n_ctx, kw), lambda b, h: (b, h)),
        ],
        out_specs=pl.BlockSpec((n_ctx, hw), lambda b, h: (b, h)),
        out_shape=jax.ShapeDtypeStruct((m, q_w), BF16),
        compiler_params=_params("parallel", "arbitrary"),
        name="ctx_attention",
    )(sink, q, kvx_ctx)


def _merge_kernel(a_ref, y_ref, h_ref, wa_ref, wc_ref, wga_ref, wgc_ref, m_ref):
    h = h_ref[...]
    m = (_sigmoid(_dot(h, wga_ref[...])) * _dot(a_ref[...], wa_ref[...])
         + _sigmoid(_dot(h, wgc_ref[...])) * _dot(y_ref[...], wc_ref[...]))
    m_ref[...] = m.astype(BF16)


def _merge(attn, y, h, w_in_bf, wa_bf, wc_bf, layer, *, tm, tn):
    m, q_w = attn.shape
    d_conv = y.shape[1]
    d_in = h.shape[1]
    d = wa_bf.shape[-1]
    nn = d // tn
    gate_off = (q_w + 2 * N_KV_HEADS * HEAD_DIM + 3 * d_conv) // tn
    return pl.pallas_call(
        _merge_kernel,
        grid=(m // tm, nn),
        in_specs=[
            pl.BlockSpec((tm, q_w), lambda i, j: (i, 0)),
            pl.BlockSpec((tm, d_conv), lambda i, j: (i, 0)),
            pl.BlockSpec((tm, d_in), lambda i, j: (i, 0)),
            pl.BlockSpec((None, q_w, tn), lambda i, j: (layer, 0, j)),
            pl.BlockSpec((None, d_conv, tn), lambda i, j: (layer, 0, j)),
            pl.BlockSpec((None, d_in, tn), lambda i, j: (layer, 0, gate_off + j)),
            pl.BlockSpec((None, d_in, tn), lambda i, j: (layer, 0, gate_off + nn + j)),
        ],
        out_specs=pl.BlockSpec((tm, tn), lambda i, j: (i, j)),
        out_shape=jax.ShapeDtypeStruct((m, d), BF16),
        compiler_params=_params("parallel", "arbitrary"),
        name="merge",
    )(attn, y, h, wa_bf, wc_bf, w_in_bf, w_in_bf)


def _outproj_kernel(m_ref, w_ref, x_ref, g_ref, o_ref):
    o_ref[...] = x_ref[...] + g_ref[...] * _dot(m_ref[...], w_ref[...])


def _outproj(mm, xr, mods, wo_bf, layer, *, tm, tn, row_of_tile):
    m, d = xr.shape
    return pl.pallas_call(
        _outproj_kernel,
        grid=(m // tm, d // tn),
        in_specs=[
            pl.BlockSpec((tm, d), lambda i, j: (i, 0)),
            pl.BlockSpec((None, d, tn), lambda i, j: (layer, 0, j)),
            pl.BlockSpec((tm, tn), lambda i, j: (i, j)),
            pl.BlockSpec((None, None, None, 1, tn), lambda i, j: (layer, row_of_tile(i), 2, 0, j)),
        ],
        out_specs=pl.BlockSpec((tm, tn), lambda i, j: (i, j)),
        out_shape=jax.ShapeDtypeStruct((m, d), F32),
        compiler_params=_params("parallel", "arbitrary"),
        name="outproj",
    )(mm, wo_bf, xr, mods)


def _ffn_kernel(*refs, final):
    if final:
        x_ref, g_ref, sh_ref, sc_ref, gt_ref, wg_ref, wu_ref, wo_ref, fg_ref, o_ref, h_scr = refs
    else:
        x_ref, g_ref, sh_ref, sc_ref, gt_ref, wg_ref, wu_ref, wo_ref, o_ref, h_scr = refs
    f = pl.program_id(1)
    nf = pl.num_programs(1)

    @pl.when(f == 0)
    def _():
        h_scr[...] = _norm_mod(x_ref[...], g_ref[...], sh_ref[...], sc_ref[...]).astype(BF16)

    def accumulate(first):
        h = h_scr[...]
        gate = _dot(h, wg_ref[...])
        act = (gate * _sigmoid(gate) * _dot(h, wu_ref[...])).astype(BF16)
        d = o_ref.shape[1]
        cw = _tile(d, FFN_OUT_CHUNK)
        for c0 in range(0, d, cw):
            part = _dot(act, wo_ref[:, c0:c0 + cw])
            if first:
                o_ref[:, c0:c0 + cw] = part
            else:
                o_ref[:, c0:c0 + cw] += part

    @pl.when(f == 0)
    def _():
        accumulate(True)

    @pl.when(f > 0)
    def _():
        accumulate(False)

    @pl.when(f == nf - 1)
    def _():
        xn = x_ref[...] + gt_ref[...] * o_ref[...]
        if final:
            xn = xn * lax.rsqrt(jnp.mean(xn * xn, axis=-1, keepdims=True) + EPS) * fg_ref[...]
        o_ref[...] = xn


def _ffn(xr, norm_g, mods, wi_bf, wo_bf, layer, *, tm, tf, row_of_tile, final_g):
    m, d = xr.shape
    d_ff = wo_bf.shape[1]
    nf = d_ff // tf
    final = final_g is not None
    in_specs = [
        pl.BlockSpec((tm, d), lambda i, f: (i, 0)),
        pl.BlockSpec((None, 1, d), lambda i, f: (layer, 0, 0)),
        pl.BlockSpec((None, None, None, 1, d), _mod_spec(layer, 3, row_of_tile)),
        pl.BlockSpec((None, None, None, 1, d), _mod_spec(layer, 4, row_of_tile)),
        pl.BlockSpec((None, None, None, 1, d), _mod_spec(layer, 5, row_of_tile)),
        pl.BlockSpec((None, d, tf), lambda i, f: (layer, 0, f)),
        pl.BlockSpec((None, d, tf), lambda i, f: (layer, 0, nf + f)),
        pl.BlockSpec((None, tf, d), lambda i, f: (layer, f, 0)),
    ]
    args = [xr, norm_g, mods, mods, mods, wi_bf, wi_bf, wo_bf]
    if final:
        in_specs.append(pl.BlockSpec((1, d), lambda i, f: (0, 0)))
        args.append(final_g)
    return pl.pallas_call(
        functools.partial(_ffn_kernel, final=final),
        grid=(m // tm, nf),
        in_specs=in_specs,
        out_specs=pl.BlockSpec((tm, d), lambda i, f: (i, 0)),
        out_shape=jax.ShapeDtypeStruct((m, d), F32),
        scratch_shapes=[pltpu.VMEM((tm, d), BF16)],
        compiler_params=_params("parallel", "arbitrary"),
        name="ffn_final" if final else "ffn",
    )(*args)


def _rope_tables(seq):
    quarter = HEAD_DIM // 4
    t = np.arange(seq)
    pos = np.stack([t // GRID_W, t % GRID_W], axis=1).astype(np.float32)
    d = np.arange(LANES) % HEAD_DIM
    axis = d // (HEAD_DIM // 2)
    first_half = (d // quarter) % 2 == 0
    freqs = jnp.asarray(ROPE_THETA, F32) ** (-jnp.arange(quarter, dtype=F32) / quarter)
    ang = jnp.asarray(pos)[:, axis] * freqs[d % quarter][None, :]
    cos, sin = jnp.cos(ang), jnp.sin(ang)
    fh = jnp.asarray(first_half)[None, :]
    return cos, jnp.where(fh, -sin, 0.0), jnp.where(fh, 0.0, sin)


def _tile(n, pref):
    return pref if n % pref == 0 else n


def kernel(x, c, ctx, c_ctx, ada_w, ada_b, norm1_g, norm2_g, w_in, conv_w, conv_b, sink,
           w_attn_out, w_conv_out, w_o, w_ffn_in, w_ffn_out, final_g):
    batch, seq, d = x.shape
    n_ctx = ctx.shape[1]
    depth = ada_w.shape[0]
    d_ff = w_ffn_out.shape[1]
    assert batch + 1 <= MOD_ROWS and seq % ATT_BLOCK == 0 and n_ctx % ATT_BLOCK == 0

    cc = jnp.zeros((MOD_ROWS, d), F32).at[:batch].set(c).at[batch].set(c_ctx)
    mods = _ada(cc, ada_w, ada_b).reshape(depth, MOD_ROWS, 6, 1, d)

    w_in_bf = w_in.astype(BF16)
    wa_bf = w_attn_out.astype(BF16)
    wc_bf = w_conv_out.astype(BF16)
    wo_bf = w_o.astype(BF16)
    wfi_bf = w_ffn_in.astype(BF16)
    wfo_bf = w_ffn_out.astype(BF16)
    n1 = norm1_g.reshape(depth, 1, d)
    n2 = norm2_g.reshape(depth, 1, d)
    conv_b3 = conv_b.reshape(depth, 1, -1)
    fg = final_g.reshape(1, d)
    rope_tabs = _rope_tables(seq)

    tm = _tile(seq, 1024)
    tm_out = _tile(seq, 512)
    tn = 2 * N_KV_HEADS * HEAD_DIM
    tf = _tile(d_ff, tn)
    tq = _tile(seq, 2048)
    m_ctx = batch * n_ctx
    lat_row = lambda t: (lambda i: i // (seq // t))
    ctx_row = lambda i: batch

    xl = x.reshape(batch * seq, d)
    xc = ctx.reshape(m_ctx, d)
    for l in range(depth):
        last = l == depth - 1
        q_c, kvx_c, h_c = _inproj(xc, n1, mods, w_in_bf, l, tm=m_ctx, row_of_tile=ctx_row, rope_tabs=None,
                                  seq=n_ctx, with_q=not last)
        q, kvx, h = _inproj(xl, n1, mods, w_in_bf, l, tm=tm, row_of_tile=lat_row(tm),
                            rope_tabs=rope_tabs, seq=seq, with_q=True)
        y = _conv_branch(h, w_in_bf, conv_w, conv_b3, l, tm=tm, tn=tn, seq=seq)
        attn = _attention(q, kvx, kvx_c, sink[l], batch=batch, seq=seq, n_ctx=n_ctx, tq=tq)
        mm = _merge(attn, y, h, w_in_bf, wa_bf, wc_bf, l, tm=tm, tn=tn)
        xl = _outproj(mm, xl, mods, wo_bf, l, tm=tm_out, tn=d, row_of_tile=lat_row(tm_out))

        if not last:
            y_c = _conv_branch(h_c, w_in_bf, conv_w, conv_b3, l, tm=n_ctx, tn=tn, seq=n_ctx)
            attn_c = _ctx_attention(q_c, kvx_c, sink[l], batch=batch, n_ctx=n_ctx)
            mm_c = _merge(attn_c, y_c, h_c, w_in_bf, wa_bf, wc_bf, l, tm=m_ctx, tn=tn)
            xc = _outproj(mm_c, xc, mods, wo_bf, l, tm=m_ctx, tn=d, row_of_tile=ctx_row)
            xc = _ffn(xc, n2, mods, wfi_bf, wfo_bf, l, tm=m_ctx, tf=tf, row_of_tile=ctx_row, final_g=None)

        xl = _ffn(xl, n2, mods, wfi_bf, wfo_bf, l, tm=tm, tf=tf, row_of_tile=lat_row(tm),
                  final_g=fg if last else None)
    return xl.reshape(batch, seq, d)
```

```python
import functools

import jax
import jax.numpy as jnp
import numpy as np
from jax import lax
from jax.experimental import pallas as pl
from jax.experimental.pallas import tpu as pltpu

GRID_W = 64
N_HEADS = 32
N_KV_HEADS = 4
HEAD_DIM = 64
WINDOW = 128
ROPE_THETA = 10000.0
EPS = 1e-6
NEG = -1e30
LOG2E = 1.4426950408889634
Q_SCALE = HEAD_DIM ** -0.5 * LOG2E

LANES = 128
BF16_ROWS = 16
ROW_CHUNK = 256
V7X_VMEM_BYTES = 64 * 1024 * 1024
VMEM_LIMIT_BYTES = V7X_VMEM_BYTES * 15 // 16
ATT_BLOCK = 128
MOD_ROWS = 8
FFN_OUT_CHUNK = 512
FFN_ROW_CHUNK = 512

F32 = jnp.float32
BF16 = jnp.bfloat16


def _params(*sem):
    return pltpu.CompilerParams(dimension_semantics=sem, vmem_limit_bytes=VMEM_LIMIT_BYTES)


def _sigmoid(v):
    return 1.0 / (1.0 + jnp.exp(-v))


def _norm_mod(x, g, shift, scale):
    y = x * lax.rsqrt(jnp.mean(x * x, axis=-1, keepdims=True) + EPS)
    return (y * g) * (1.0 + scale) + shift


def _dot(a, b):
    return jnp.dot(a, b, preferred_element_type=F32)


def _dot_nt(a, b):
    return lax.dot_general(a, b, (((1,), (1,)), ((), ())), preferred_element_type=F32)


def _ada_kernel(c_ref, w_ref, b_ref, o_ref):
    c = c_ref[...]
    s = c * _sigmoid(c)
    o_ref[...] = _dot(s.astype(BF16), w_ref[...].astype(BF16)) + b_ref[...]


def _ada(cc, ada_w, ada_b):
    depth, d, n = ada_w.shape
    tn = _tile(n, 1024)
    return pl.pallas_call(
        _ada_kernel,
        grid=(depth, n // tn),
        in_specs=[
            pl.BlockSpec((MOD_ROWS, d), lambda l, j: (0, 0)),
            pl.BlockSpec((None, d, tn), lambda l, j: (l, 0, j)),
            pl.BlockSpec((None, 1, tn), lambda l, j: (l, 0, j)),
        ],
        out_specs=pl.BlockSpec((None, MOD_ROWS, tn), lambda l, j: (l, 0, j)),
        out_shape=jax.ShapeDtypeStruct((depth, MOD_ROWS, n), F32),
        compiler_params=_params("parallel", "parallel"),
        name="ada_mod",
    )(cc, ada_w, ada_b.reshape(depth, 1, n))


def _mod_spec(layer, which, row_of_tile):
    def index(i, j):
        return (layer, row_of_tile(i), which, 0, 0)
    return index


def _rope(v, cos, sin_a, sin_b):
    return v * cos + pltpu.roll(v, LANES - 16, 1) * sin_a + pltpu.roll(v, 16, 1) * sin_b


def _inproj_kernel(*refs, use_rope, with_q, nx):
    refs = list(refs)
    x_refs = refs[:nx]
    del refs[:nx]
    g_ref, sh_ref, sc_ref, wkv_ref = refs[:4]
    del refs[:4]
    wq_ref = refs.pop(0) if with_q else None
    if use_rope:
        cos_ref, sa_ref, sb_ref = refs[:3]
        del refs[:3]
    q_ref = refs.pop(0) if with_q else None
    kvx_ref, h_ref = refs
    j = pl.program_id(1)

    rows = h_ref.shape[0]
    rc = _tile(rows, ROW_CHUNK)

    @pl.when(j == 0)
    def _():
        g, sh, sc = g_ref[...], sh_ref[...], sc_ref[...]
        lo_mask = lax.broadcasted_iota(jnp.int32, (rc, LANES), 1) < HEAD_DIM
        for ci, r0 in enumerate(range(0, rows, rc)):
            rs = slice(r0, r0 + rc)
            h = _norm_mod(x_refs[ci][...], g, sh, sc).astype(BF16)
            h_ref[rs] = h
            acc = _dot(h, wkv_ref[...])
            kvw = acc.shape[1] // 2
            for part in range(2):
                for pr in range(kvw // LANES):
                    v = acc[:, part * kvw + pr * LANES: part * kvw + (pr + 1) * LANES]
                    if part == 0 and use_rope:
                        v = _rope(v, cos_ref[rs], sa_ref[rs], sb_ref[rs])
                    r = pltpu.roll(v, HEAD_DIM, 1)
                    tiles = ((jnp.where(lo_mask, v, 0.0), jnp.where(lo_mask, 0.0, r)),
                             (jnp.where(lo_mask, r, 0.0), jnp.where(lo_mask, 0.0, v)))
                    for odd in range(2):
                        head = 2 * pr + odd
                        for hi in range(2):
                            col = head * 4 * LANES + (2 * part + hi) * LANES
                            kvx_ref[rs, col:col + LANES] = tiles[odd][hi].astype(BF16)

    if with_q:
        for r0 in range(0, rows, rc):
            acc = _dot(h_ref[r0:r0 + rc], wq_ref[...])
            for cg in range(acc.shape[1] // LANES):
                v = acc[:, cg * LANES:(cg + 1) * LANES]
                if use_rope:
                    v = _rope(v, cos_ref[r0:r0 + rc], sa_ref[r0:r0 + rc], sb_ref[r0:r0 + rc])
                q_ref[r0:r0 + rc, cg * LANES:(cg + 1) * LANES] = (v * Q_SCALE).astype(BF16)


def _inproj(xr, norm_g, mods, w_bf, layer, *, tm, row_of_tile, rope_tabs, seq, with_q):
    m, d = xr.shape
    q_w = N_HEADS * HEAD_DIM
    kv_w = N_KV_HEADS * HEAD_DIM
    tnq = _tile(q_w, 512)
    use_rope = rope_tabs is not None
    kvx_w = N_KV_HEADS * 4 * LANES

    rc = _tile(tm, ROW_CHUNK)
    nx = tm // rc
    last_tile = m // tm - 1

    def x_spec(k):
        def index(i, j):
            return (jnp.minimum(i + (j > k).astype(jnp.int32), last_tile) * nx + k, 0)
        return pl.BlockSpec((rc, d), index)

    in_specs = [x_spec(k) for k in range(nx)] + [
        pl.BlockSpec((None, 1, d), lambda i, j: (layer, 0, 0)),
        pl.BlockSpec((None, None, None, 1, d), _mod_spec(layer, 0, row_of_tile)),
        pl.BlockSpec((None, None, None, 1, d), _mod_spec(layer, 1, row_of_tile)),
        pl.BlockSpec((d, 2 * kv_w), lambda i, j: (0, q_w // (2 * kv_w))),
    ]
    args = [xr] * nx + [norm_g, mods, mods, w_bf]
    if with_q:
        in_specs.append(pl.BlockSpec((d, tnq), lambda i, j: (0, j)))
        args.append(w_bf)
    if use_rope:
        tps = seq // tm
        for t in rope_tabs:
            in_specs.append(pl.BlockSpec((tm, LANES), lambda i, j: (i % tps, 0)))
            args.append(t)

    out_specs = [pl.BlockSpec((tm, kvx_w), lambda i, j: (i, 0)),
                 pl.BlockSpec((tm, d), lambda i, j: (i, 0))]
    out_shape = [jax.ShapeDtypeStruct((m, kvx_w), BF16), jax.ShapeDtypeStruct((m, d), BF16)]
    if with_q:
        out_specs.insert(0, pl.BlockSpec((tm, tnq), lambda i, j: (i, j)))
        out_shape.insert(0, jax.ShapeDtypeStruct((m, q_w), BF16))
    out = pl.pallas_call(
        functools.partial(_inproj_kernel, use_rope=use_rope, with_q=with_q, nx=nx),
        grid=(m // tm, q_w // tnq if with_q else 1),
        in_specs=in_specs,
        out_specs=out_specs,
        out_shape=out_shape,
        compiler_params=_params("parallel", "arbitrary"),
        name="inproj" if with_q else "inproj_kv",
    )(*args)
    return out if with_q else (None,) + tuple(out)


def _cast_plan(weights, n_steps, step_of):
    in_specs, out_specs, out_shape = [], [], []
    for w in weights:
        rows, cols = w.shape
        nblk = n_steps
        while rows % (nblk * BF16_ROWS):
            assert nblk % 2 == 0, (rows, n_steps)
            nblk //= 2
        per = n_steps // nblk

        def index(*g, per=per):
            return (step_of(*g) // per, 0)
        in_specs.append(pl.BlockSpec((rows // nblk, cols), index))
        out_specs.append(pl.BlockSpec((rows // nblk, cols), index))
        out_shape.append(jax.ShapeDtypeStruct((rows, cols), BF16))
    return in_specs, out_specs, out_shape


def _run_casts(src_refs, dst_refs):
    for src, dst in zip(src_refs, dst_refs):
        dst[...] = src[...].astype(BF16)


def _conv_kernel(*refs, tm, seq, ncast):
    hp_ref, h_ref, hn_ref, wb_ref, wc_ref, wx_ref, cw_ref, cb_ref = refs[:8]
    y_ref, h_scr = refs[8 + ncast], refs[-1]
    _run_casts(refs[8:8 + ncast], refs[9 + ncast:-1])
    i = pl.program_id(0)
    j = pl.program_id(1)
    halo = BF16_ROWS

    @pl.when(j == 0)
    def _():
        h_scr[0:halo] = hp_ref[...]
        h_scr[halo:halo + tm] = h_ref[...]
        h_scr[halo + tm:] = hn_ref[...]

    hx = h_scr[...]
    u = _dot(hx, wc_ref[...]) * _dot(hx, wx_ref[...])
    rows = tm + 2 * halo
    u_up = pltpu.roll(u, 1, 0)[halo:halo + tm]
    u_dn = pltpu.roll(u, rows - 1, 0)[halo:halo + tm]
    row = lax.broadcasted_iota(jnp.int32, (tm, 1), 0)
    seq_start = (i * tm) % seq == 0
    seq_end = ((i + 1) * tm) % seq == 0
    u_up = jnp.where(jnp.logical_and(row == 0, seq_start), 0.0, u_up)
    u_dn = jnp.where(jnp.logical_and(row == tm - 1, seq_end), 0.0, u_dn)
    w = cw_ref[...]
    conv = u_up * w[0:1] + u[halo:halo + tm] * w[1:2] + u_dn * w[2:3] + cb_ref[...]
    y_ref[...] = (_dot(h_ref[...], wb_ref[...]) * conv).astype(BF16)


def _conv_branch(h, w_bf, conv_w, conv_b, layer, *, tm, tn, seq, cast=()):
    m, d = h.shape
    d_conv = conv_w.shape[-1]
    halo = BF16_ROWS
    off = (N_HEADS * HEAD_DIM + 2 * N_KV_HEADS * HEAD_DIM) // tn
    nc = d_conv // tn
    hb = tm // halo
    last_hb = m // halo - 1

    def wspec(k):
        return pl.BlockSpec((d, tn), lambda i, j: (0, off + k * nc + j))

    c_in, c_out, c_shape = _cast_plan(cast, (m // tm) * nc, lambda i, j: i * nc + j)
    out = pl.pallas_call(
        functools.partial(_conv_kernel, tm=tm, seq=seq, ncast=len(cast)),
        grid=(m // tm, nc),
        in_specs=[
            pl.BlockSpec((halo, d), lambda i, j: (jnp.maximum(i * hb - 1, 0), 0)),
            pl.BlockSpec((tm, d), lambda i, j: (i, 0)),
            pl.BlockSpec((halo, d), lambda i, j: (jnp.minimum((i + 1) * hb, last_hb), 0)),
            wspec(0), wspec(1), wspec(2),
            pl.BlockSpec((None, 3, tn), lambda i, j: (layer, 0, j)),
            pl.BlockSpec((None, 1, tn), lambda i, j: (layer, 0, j)),
        ] + c_in,
        out_specs=[pl.BlockSpec((tm, tn), lambda i, j: (i, j))] + c_out,
        out_shape=[jax.ShapeDtypeStruct((m, d_conv), BF16)] + c_shape,
        scratch_shapes=[pltpu.VMEM((tm + 2 * halo, d), BF16)],
        compiler_params=_params("arbitrary", "arbitrary"),
        name="conv_branch",
    )(h, h, h, w_bf, w_bf, w_bf, conv_w, conv_b, *cast)
    return out[0], tuple(out[1:])


def _group_attention(q_ref, rows, pieces, sinks, ones_sel, o_ref):
    blk = ATT_BLOCK
    npair = len(sinks) // 2
    qs = jnp.concatenate([q_ref[rows, jp * LANES:(jp + 1) * LANES] for jp in range(npair)], axis=0)
    probs, vals, sink_p = [], [], []
    for par in range(2):
        chunks = []
        for kvx, biases in pieces:
            s = _dot_nt(qs, kvx[:, par * LANES:(par + 1) * LANES])
            for ci in range(s.shape[1] // LANES):
                ch = s[:, ci * LANES:(ci + 1) * LANES]
                if biases is not None and biases[ci] is not None:
                    ch = ch + jnp.concatenate([biases[ci]] * npair, axis=0)
                chunks.append(ch)
            vals.append(kvx[:, (2 + par) * LANES:(3 + par) * LANES])
        sink = jnp.concatenate([jnp.full((blk, LANES), sinks[2 * jp + par], F32) for jp in range(npair)], axis=0)
        mx = jnp.maximum(jnp.max(functools.reduce(jnp.maximum, chunks), axis=-1, keepdims=True), sink)
        probs.extend(jnp.exp2(ch - mx).astype(BF16) for ch in chunks)
        sink_p.append(jnp.exp2(sink - mx))
    rhs = jnp.concatenate([jnp.concatenate(vals, axis=0), ones_sel], axis=1)
    out = _dot(jnp.concatenate(probs, axis=1), rhs)
    lane = lax.broadcasted_iota(jnp.int32, (npair * blk, LANES), 1)
    den = out[:, LANES:] + jnp.where(lane < HEAD_DIM, sink_p[0], sink_p[1])
    res = out[:, :LANES] / den
    for jp in range(npair):
        o_ref[rows, jp * LANES:(jp + 1) * LANES] = res[jp * blk:(jp + 1) * blk].astype(BF16)


def _ones_selector(n):
    row = lax.broadcasted_iota(jnp.int32, (2 * n, LANES), 0)
    lane = lax.broadcasted_iota(jnp.int32, (2 * n, LANES), 1)
    return jnp.where((row < n) == (lane < HEAD_DIM), 1.0, 0.0).astype(BF16)


def _attn_kernel(*refs, tq, nblk_seq, ncast):
    sink_ref, q_ref, kvp_ref, kvc_ref, kvn_ref, ctx_ref = refs[:6]
    o_ref, band_scr = refs[6 + ncast], refs[-1]
    _run_casts(refs[6:6 + ncast], refs[7 + ncast:-1])
    i = pl.program_id(1)
    h = pl.program_id(2)
    blk = ATT_BLOCK
    groups = N_HEADS // N_KV_HEADS
    band_scr[0:blk] = kvp_ref[...]
    band_scr[blk:blk + tq] = kvc_ref[...]
    band_scr[blk + tq:] = kvn_ref[...]
    ctx = ctx_ref[...]
    sinks = [sink_ref[h * groups + g] * LOG2E for g in range(groups)]
    r = lax.broadcasted_iota(jnp.int32, (blk, blk), 0)
    c = lax.broadcasted_iota(jnp.int32, (blk, blk), 1)
    bias_prev = jnp.where(c >= r, 0.0, NEG)
    bias_next = jnp.where(c <= r, 0.0, NEG)
    ones_sel = _ones_selector(ctx.shape[0] + 3 * blk)

    for sb in range(tq // blk):
        r0 = sb * blk
        gblk = i * (tq // blk) + sb
        pen_first = jnp.where(gblk == 0, NEG, 0.0)
        pen_last = jnp.where(gblk == nblk_seq - 1, NEG, 0.0)
        biases = (bias_prev + pen_first, None, bias_next + pen_last)
        band = band_scr[r0:r0 + 3 * blk, :]
        _group_attention(q_ref, pl.ds(r0, blk), ((ctx, None), (band, biases)), sinks, ones_sel, o_ref)


def _attention(q, kvx, kvx_ctx, sink, *, batch, seq, n_ctx, tq, cast=()):
    m, q_w = q.shape
    blk = ATT_BLOCK
    hw = q_w // N_KV_HEADS
    kw = 4 * LANES
    nb = seq // blk
    tpb = tq // blk

    nt = seq // tq
    c_in, c_out, c_shape = _cast_plan(cast, batch * nt * N_KV_HEADS,
                                      lambda b, i, h: (b * nt + i) * N_KV_HEADS + h)
    out = pl.pallas_call(
        functools.partial(_attn_kernel, tq=tq, nblk_seq=nb, ncast=len(cast)),
        grid=(batch, nt, N_KV_HEADS),
        in_specs=[
            pl.BlockSpec(memory_space=pltpu.SMEM),
            pl.BlockSpec((tq, hw), lambda b, i, h: (b * nt + i, h)),
            pl.BlockSpec((blk, kw), lambda b, i, h: (b * nb + jnp.maximum(i * tpb - 1, 0), h)),
            pl.BlockSpec((tq, kw), lambda b, i, h: (b * nt + i, h)),
            pl.BlockSpec((blk, kw), lambda b, i, h: (b * nb + jnp.minimum((i + 1) * tpb, nb - 1), h)),
            pl.BlockSpec((n_ctx, kw), lambda b, i, h: (b, h)),
        ] + c_in,
        out_specs=[pl.BlockSpec((tq, hw), lambda b, i, h: (b * nt + i, h))] + c_out,
        out_shape=[jax.ShapeDtypeStruct((m, q_w), BF16)] + c_shape,
        scratch_shapes=[pltpu.VMEM((tq + 2 * blk, kw), BF16)],
        compiler_params=_params("arbitrary", "arbitrary", "arbitrary"),
        name="band_attention",
    )(sink, q, kvx, kvx, kvx, kvx_ctx, *cast)
    return out[0], tuple(out[1:])


def _ctx_attn_kernel(sink_ref, q_ref, ctx_ref, o_ref, *, n_ctx):
    h = pl.program_id(1)
    groups = N_HEADS // N_KV_HEADS
    ctx = ctx_ref[...]
    sinks = [sink_ref[h * groups + g] * LOG2E for g in range(groups)]
    ones_sel = _ones_selector(n_ctx)
    for r0 in range(0, n_ctx, ATT_BLOCK):
        _group_attention(q_ref, pl.ds(r0, ATT_BLOCK), ((ctx, None),), sinks, ones_sel, o_ref)


def _ctx_attention(q, kvx_ctx, sink, *, batch, n_ctx):
    m, q_w = q.shape
    hw = q_w // N_KV_HEADS
    kw = 4 * LANES
    return pl.pallas_call(
        functools.partial(_ctx_attn_kernel, n_ctx=n_ctx),
        grid=(batch, N_KV_HEADS),
        in_specs=[
            pl.BlockSpec(memory_space=pltpu.SMEM),
            pl.BlockSpec((n_ctx, hw), lambda b, h: (b, h)),
            pl.BlockSpec((n_ctx, kw), lambda b, h: (b, h)),
        ],
        out_specs=pl.BlockSpec((n_ctx, hw), lambda b, h: (b, h)),
        out_shape=jax.ShapeDtypeStruct((m, q_w), BF16),
        compiler_params=_params("parallel", "arbitrary"),
        name="ctx_attention",
    )(sink, q, kvx_ctx)


def _merge_kernel(a_ref, y_ref, h_ref, wa_ref, wc_ref, wga_ref, wgc_ref, m_ref):
    h = h_ref[...]
    m = (_sigmoid(_dot(h, wga_ref[...])) * _dot(a_ref[...], wa_ref[...])
         + _sigmoid(_dot(h, wgc_ref[...])) * _dot(y_ref[...], wc_ref[...]))
    m_ref[...] = m.astype(BF16)


def _merge(attn, y, h, w_in_bf, wa_bf, wc_bf, layer, *, tm, tn):
    m, q_w = attn.shape
    d_conv = y.shape[1]
    d_in = h.shape[1]
    d = wa_bf.shape[-1]
    nn = d // tn
    gate_off = (q_w + 2 * N_KV_HEADS * HEAD_DIM + 3 * d_conv) // tn
    return pl.pallas_call(
        _merge_kernel,
        grid=(m // tm, nn),
        in_specs=[
            pl.BlockSpec((tm, q_w), lambda i, j: (i, 0)),
            pl.BlockSpec((tm, d_conv), lambda i, j: (i, 0)),
            pl.BlockSpec((tm, d_in), lambda i, j: (i, 0)),
            pl.BlockSpec((q_w, tn), lambda i, j: (0, j)),
            pl.BlockSpec((d_conv, tn), lambda i, j: (0, j)),
            pl.BlockSpec((d_in, tn), lambda i, j: (0, gate_off + j)),
            pl.BlockSpec((d_in, tn), lambda i, j: (0, gate_off + nn + j)),
        ],
        out_specs=pl.BlockSpec((tm, tn), lambda i, j: (i, j)),
        out_shape=jax.ShapeDtypeStruct((m, d), BF16),
        compiler_params=_params("parallel", "arbitrary"),
        name="merge",
    )(attn, y, h, wa_bf, wc_bf, w_in_bf, w_in_bf)


def _outproj_kernel(m_ref, w_ref, x_ref, g_ref, o_ref):
    o_ref[...] = x_ref[...] + g_ref[...] * _dot(m_ref[...], w_ref[...])


def _outproj(mm, xr, mods, wo_bf, layer, *, tm, tn, row_of_tile):
    m, d = xr.shape
    return pl.pallas_call(
        _outproj_kernel,
        grid=(m // tm, d // tn),
        in_specs=[
            pl.BlockSpec((tm, d), lambda i, j: (i, 0)),
            pl.BlockSpec((d, tn), lambda i, j: (0, j)),
            pl.BlockSpec((tm, tn), lambda i, j: (i, j)),
            pl.BlockSpec((None, None, None, 1, tn), lambda i, j: (layer, row_of_tile(i), 2, 0, j)),
        ],
        out_specs=pl.BlockSpec((tm, tn), lambda i, j: (i, j)),
        out_shape=jax.ShapeDtypeStruct((m, d), F32),
        compiler_params=_params("parallel", "arbitrary"),
        name="outproj",
    )(mm, wo_bf, xr, mods)


def _ffn_kernel(*refs, final, nf):
    if final:
        x_ref, g_ref, sh_ref, sc_ref, gt_ref, wg_ref, wu_ref, wo_ref, fg_ref, o_ref, h_scr = refs
    else:
        x_ref, g_ref, sh_ref, sc_ref, gt_ref, wg_ref, wu_ref, wo_ref, o_ref, h_scr = refs
    f = pl.program_id(1)
    rows, d = o_ref.shape
    rc = _tile(rows, FFN_ROW_CHUNK)
    cw = _tile(d, FFN_OUT_CHUNK)

    def step(first, last):
        for r0 in range(0, rows, rc):
            rs = slice(r0, r0 + rc)
            if first:
                h = _norm_mod(x_ref[rs], g_ref[...], sh_ref[...], sc_ref[...]).astype(BF16)
                h_scr[rs] = h
            else:
                h = h_scr[rs]
            gate = _dot(h, wg_ref[...])
            act = (gate * _sigmoid(gate) * _dot(h, wu_ref[...])).astype(BF16)
            for c0 in range(0, d, cw):
                part = _dot(act, wo_ref[:, c0:c0 + cw])
                if first:
                    o_ref[rs, c0:c0 + cw] = part
                else:
                    o_ref[rs, c0:c0 + cw] += part
            if last:
                xn = x_ref[rs] + gt_ref[...] * o_ref[rs]
                if final:
                    xn = xn * lax.rsqrt(jnp.mean(xn * xn, axis=-1, keepdims=True) + EPS) * fg_ref[...]
                o_ref[rs] = xn

    if nf == 1:
        step(True, True)
    else:
        pl.when(f == 0)(lambda: step(True, False))
        pl.when(jnp.logical_and(f > 0, f < nf - 1))(lambda: step(False, False))
        pl.when(f == nf - 1)(lambda: step(False, True))


def _ffn(xr, norm_g, mods, wi_bf, wo_bf, layer, *, tm, tf, row_of_tile, final_g):
    m, d = xr.shape
    d_ff = wo_bf.shape[0]
    nf = d_ff // tf
    final = final_g is not None
    in_specs = [
        pl.BlockSpec((tm, d), lambda i, f: (i, 0)),
        pl.BlockSpec((None, 1, d), lambda i, f: (layer, 0, 0)),
        pl.BlockSpec((None, None, None, 1, d), _mod_spec(layer, 3, row_of_tile)),
        pl.BlockSpec((None, None, None, 1, d), _mod_spec(layer, 4, row_of_tile)),
        pl.BlockSpec((None, None, None, 1, d), _mod_spec(layer, 5, row_of_tile)),
        pl.BlockSpec((d, tf), lambda i, f: (0, f)),
        pl.BlockSpec((d, tf), lambda i, f: (0, nf + f)),
        pl.BlockSpec((tf, d), lambda i, f: (f, 0)),
    ]
    args = [xr, norm_g, mods, mods, mods, wi_bf, wi_bf, wo_bf]
    if final:
        in_specs.append(pl.BlockSpec((1, d), lambda i, f: (0, 0)))
        args.append(final_g)
    return pl.pallas_call(
        functools.partial(_ffn_kernel, final=final, nf=nf),
        grid=(m // tm, nf),
        in_specs=in_specs,
        out_specs=pl.BlockSpec((tm, d), lambda i, f: (i, 0)),
        out_shape=jax.ShapeDtypeStruct((m, d), F32),
        scratch_shapes=[pltpu.VMEM((tm, d), BF16)],
        compiler_params=_params("parallel", "arbitrary"),
        name="ffn_final" if final else "ffn",
    )(*args)


def _rope_tables(seq):
    quarter = HEAD_DIM // 4
    t = np.arange(seq)
    pos = np.stack([t // GRID_W, t % GRID_W], axis=1).astype(np.float32)
    d = np.arange(LANES) % HEAD_DIM
    axis = d // (HEAD_DIM // 2)
    first_half = (d // quarter) % 2 == 0
    freqs = np.float32(ROPE_THETA) ** (-np.arange(quarter, dtype=np.float32) / np.float32(quarter))
    ang = (pos[:, axis] * freqs[d % quarter][None, :]).astype(np.float32)
    cos, sin = np.cos(ang), np.sin(ang)
    fh = first_half[None, :]
    zero = np.float32(0.0)
    return tuple(jnp.asarray(t, F32) for t in (cos, np.where(fh, -sin, zero), np.where(fh, zero, sin)))


def _tile(n, pref):
    return pref if n % pref == 0 else n


def kernel(x, c, ctx, c_ctx, ada_w, ada_b, norm1_g, norm2_g, w_in, conv_w, conv_b, sink,
           w_attn_out, w_conv_out, w_o, w_ffn_in, w_ffn_out, final_g):
    batch, seq, d = x.shape
    n_ctx = ctx.shape[1]
    depth = ada_w.shape[0]
    d_ff = w_ffn_out.shape[1]
    assert batch + 1 <= MOD_ROWS and seq % ATT_BLOCK == 0 and n_ctx % ATT_BLOCK == 0

    cc = jnp.zeros((MOD_ROWS, d), F32).at[:batch].set(c).at[batch].set(c_ctx)
    mods = _ada(cc, ada_w, ada_b).reshape(depth, MOD_ROWS, 6, 1, d)

    n1 = norm1_g.reshape(depth, 1, d)
    n2 = norm2_g.reshape(depth, 1, d)
    conv_b3 = conv_b.reshape(depth, 1, -1)
    fg = final_g.reshape(1, d)
    rope_tabs = _rope_tables(seq)

    tm = _tile(seq, 1024)
    tm_out = _tile(seq, 512)
    tn = 2 * N_KV_HEADS * HEAD_DIM
    tf = _tile(d_ff, tn)
    tq = _tile(seq, 2048)
    m_ctx = batch * n_ctx
    lat_row = lambda t: (lambda i: i // (seq // t))
    ctx_row = lambda i: batch

    late = lambda l: (w_attn_out[l], w_conv_out[l], w_o[l], w_ffn_in[l], w_ffn_out[l])
    wts = {0: (w_in[0].astype(BF16),)}

    xl = x.reshape(batch * seq, d)
    xc = ctx.reshape(m_ctx, d)
    for l in range(depth):
        last = l == depth - 1
        w_in_bf = wts[l][0]
        q_c, kvx_c, h_c = _inproj(xc, n1, mods, w_in_bf, l, tm=m_ctx, row_of_tile=ctx_row, rope_tabs=None,
                                  seq=n_ctx, with_q=not last)
        q, kvx, h = _inproj(xl, n1, mods, w_in_bf, l, tm=tm, row_of_tile=lat_row(tm),
                            rope_tabs=rope_tabs, seq=seq, with_q=True)
        y, cast0 = _conv_branch(h, w_in_bf, conv_w, conv_b3, l, tm=tm, tn=tn, seq=seq,
                                cast=late(0) if l == 0 else ())
        if l == 0:
            wts[0] = wts[0] + cast0
        attn, cast1 = _attention(q, kvx, kvx_c, sink[l], batch=batch, seq=seq, n_ctx=n_ctx, tq=tq,
                                 cast=() if last else (w_in[l + 1],) + late(l + 1))
        if not last:
            wts[l + 1] = cast1
        _, wa_bf, wc_bf, wo_bf, wfi_bf, wfo_bf = wts[l]
        mm = _merge(attn, y, h, w_in_bf, wa_bf, wc_bf, l, tm=tm, tn=tn)
        xl = _outproj(mm, xl, mods, wo_bf, l, tm=tm_out, tn=d, row_of_tile=lat_row(tm_out))

        if not last:
            y_c, _ = _conv_branch(h_c, w_in_bf, conv_w, conv_b3, l, tm=n_ctx, tn=tn, seq=n_ctx)
            attn_c = _ctx_attention(q_c, kvx_c, sink[l], batch=batch, n_ctx=n_ctx)
            mm_c = _merge(attn_c, y_c, h_c, w_in_bf, wa_bf, wc_bf, l, tm=m_ctx, tn=tn)
            xc = _outproj(mm_c, xc, mods, wo_bf, l, tm=m_ctx, tn=d, row_of_tile=ctx_row)
            xc = _ffn(xc, n2, mods, wfi_bf, wfo_bf, l, tm=m_ctx, tf=tf, row_of_tile=ctx_row, final_g=None)

        xl = _ffn(xl, n2, mods, wfi_bf, wfo_bf, l, tm=tm, tf=tf, row_of_tile=lat_row(tm),
                  final_g=fg if last else None)
    return xl.reshape(batch, seq, d)
```

```python
import functools

import jax
import jax.numpy as jnp
import numpy as np
from jax import lax
from jax.experimental import pallas as pl
from jax.experimental.pallas import tpu as pltpu

GRID_W = 64
N_HEADS = 32
N_KV_HEADS = 4
HEAD_DIM = 64
WINDOW = 128
ROPE_THETA = 10000.0
EPS = 1e-6
NEG = -1e30
LOG2E = 1.4426950408889634
Q_SCALE = HEAD_DIM ** -0.5 * LOG2E

LANES = 128
BF16_ROWS = 16
ROW_CHUNK = 256
V7X_VMEM_BYTES = 64 * 1024 * 1024
VMEM_LIMIT_BYTES = V7X_VMEM_BYTES * 15 // 16
ATT_BLOCK = 128
MOD_ROWS = 8
FFN_OUT_CHUNK = 512
FFN_ROW_CHUNK = 512

F32 = jnp.float32
BF16 = jnp.bfloat16


def _params(*sem):
    return pltpu.CompilerParams(dimension_semantics=sem, vmem_limit_bytes=VMEM_LIMIT_BYTES)


def _sigmoid(v):
    return 1.0 / (1.0 + jnp.exp(-v))


def _norm_mod(x, g, shift, scale):
    y = x * lax.rsqrt(jnp.mean(x * x, axis=-1, keepdims=True) + EPS)
    return (y * g) * (1.0 + scale) + shift


def _dot(a, b):
    return jnp.dot(a, b, preferred_element_type=F32)


def _dot_nt(a, b):
    return lax.dot_general(a, b, (((1,), (1,)), ((), ())), preferred_element_type=F32)


def _ada_kernel(c_ref, w_ref, b_ref, o_ref):
    c = c_ref[...]
    s = c * _sigmoid(c)
    o_ref[...] = _dot(s.astype(BF16), w_ref[...].astype(BF16)) + b_ref[...]


def _ada(cc, ada_w, ada_b):
    depth, d, n = ada_w.shape
    tn = _tile(n, 1024)
    return pl.pallas_call(
        _ada_kernel,
        grid=(depth, n // tn),
        in_specs=[
            pl.BlockSpec((MOD_ROWS, d), lambda l, j: (0, 0)),
            pl.BlockSpec((None, d, tn), lambda l, j: (l, 0, j)),
            pl.BlockSpec((None, 1, tn), lambda l, j: (l, 0, j)),
        ],
        out_specs=pl.BlockSpec((None, MOD_ROWS, tn), lambda l, j: (l, 0, j)),
        out_shape=jax.ShapeDtypeStruct((depth, MOD_ROWS, n), F32),
        compiler_params=_params("parallel", "parallel"),
        name="ada_mod",
    )(cc, ada_w, ada_b.reshape(depth, 1, n))


def _mod_spec(layer, which, row_of_tile):
    def index(i, j):
        return (layer, row_of_tile(i), which, 0, 0)
    return index


def _rope(v, cos, sin_a, sin_b):
    return v * cos + pltpu.roll(v, LANES - 16, 1) * sin_a + pltpu.roll(v, 16, 1) * sin_b


def _inproj_kernel(*refs, use_rope, with_q, nx):
    refs = list(refs)
    x_refs = refs[:nx]
    del refs[:nx]
    g_ref, sh_ref, sc_ref, wkv_ref = refs[:4]
    del refs[:4]
    wq_ref = refs.pop(0) if with_q else None
    if use_rope:
        cos_ref, sa_ref, sb_ref = refs[:3]
        del refs[:3]
    q_ref = refs.pop(0) if with_q else None
    kvx_ref, h_ref = refs
    j = pl.program_id(1)

    rows = h_ref.shape[0]
    rc = _tile(rows, ROW_CHUNK)

    @pl.when(j == 0)
    def _():
        g, sh, sc = g_ref[...], sh_ref[...], sc_ref[...]
        lo_mask = lax.broadcasted_iota(jnp.int32, (rc, LANES), 1) < HEAD_DIM
        for ci, r0 in enumerate(range(0, rows, rc)):
            rs = slice(r0, r0 + rc)
            h = _norm_mod(x_refs[ci][...], g, sh, sc).astype(BF16)
            h_ref[rs] = h
            acc = _dot(h, wkv_ref[...])
            kvw = acc.shape[1] // 2
            for part in range(2):
                for pr in range(kvw // LANES):
                    v = acc[:, part * kvw + pr * LANES: part * kvw + (pr + 1) * LANES]
                    if part == 0 and use_rope:
                        v = _rope(v, cos_ref[rs], sa_ref[rs], sb_ref[rs])
                    r = pltpu.roll(v, HEAD_DIM, 1)
                    tiles = ((jnp.where(lo_mask, v, 0.0), jnp.where(lo_mask, 0.0, r)),
                             (jnp.where(lo_mask, r, 0.0), jnp.where(lo_mask, 0.0, v)))
                    for odd in range(2):
                        head = 2 * pr + odd
                        for hi in range(2):
                            col = head * 4 * LANES + (2 * part + hi) * LANES
                            kvx_ref[rs, col:col + LANES] = tiles[odd][hi].astype(BF16)

    if with_q:
        for r0 in range(0, rows, rc):
            acc = _dot(h_ref[r0:r0 + rc], wq_ref[...])
            for cg in range(acc.shape[1] // LANES):
                v = acc[:, cg * LANES:(cg + 1) * LANES]
                if use_rope:
                    v = _rope(v, cos_ref[r0:r0 + rc], sa_ref[r0:r0 + rc], sb_ref[r0:r0 + rc])
                q_ref[r0:r0 + rc, cg * LANES:(cg + 1) * LANES] = (v * Q_SCALE).astype(BF16)


def _inproj(xr, norm_g, mods, w_bf, layer, *, tm, row_of_tile, rope_tabs, seq, with_q):
    m, d = xr.shape
    q_w = N_HEADS * HEAD_DIM
    kv_w = N_KV_HEADS * HEAD_DIM
    tnq = _tile(q_w, 512)
    use_rope = rope_tabs is not None
    kvx_w = N_KV_HEADS * 4 * LANES

    rc = _tile(tm, ROW_CHUNK)
    nx = tm // rc
    last_tile = m // tm - 1

    def x_spec(k):
        def index(i, j):
            return (jnp.minimum(i + (j > k).astype(jnp.int32), last_tile) * nx + k, 0)
        return pl.BlockSpec((rc, d), index)

    in_specs = [x_spec(k) for k in range(nx)] + [
        pl.BlockSpec((None, 1, d), lambda i, j: (layer, 0, 0)),
        pl.BlockSpec((None, None, None, 1, d), _mod_spec(layer, 0, row_of_tile)),
        pl.BlockSpec((None, None, None, 1, d), _mod_spec(layer, 1, row_of_tile)),
        pl.BlockSpec((d, 2 * kv_w), lambda i, j: (0, q_w // (2 * kv_w))),
    ]
    args = [xr] * nx + [norm_g, mods, mods, w_bf]
    if with_q:
        in_specs.append(pl.BlockSpec((d, tnq), lambda i, j: (0, j)))
        args.append(w_bf)
    if use_rope:
        tps = seq // tm
        for t in rope_tabs:
            in_specs.append(pl.BlockSpec((tm, LANES), lambda i, j: (i % tps, 0)))
            args.append(t)

    out_specs = [pl.BlockSpec((tm, kvx_w), lambda i, j: (i, 0)),
                 pl.BlockSpec((tm, d), lambda i, j: (i, 0))]
    out_shape = [jax.ShapeDtypeStruct((m, kvx_w), BF16), jax.ShapeDtypeStruct((m, d), BF16)]
    if with_q:
        out_specs.insert(0, pl.BlockSpec((tm, tnq), lambda i, j: (i, j)))
        out_shape.insert(0, jax.ShapeDtypeStruct((m, q_w), BF16))
    out = pl.pallas_call(
        functools.partial(_inproj_kernel, use_rope=use_rope, with_q=with_q, nx=nx),
        grid=(m // tm, q_w // tnq if with_q else 1),
        in_specs=in_specs,
        out_specs=out_specs,
        out_shape=out_shape,
        compiler_params=_params("parallel", "arbitrary"),
        name="inproj" if with_q else "inproj_kv",
    )(*args)
    return out if with_q else (None,) + tuple(out)


def _cast_plan(weights, n_steps, step_of):
    in_specs, out_specs, out_shape = [], [], []
    for w, layer in weights:
        _, rows, cols = w.shape
        nblk = n_steps
        while rows % (nblk * BF16_ROWS):
            assert nblk % 2 == 0, (rows, n_steps)
            nblk //= 2
        per = n_steps // nblk

        def src_index(*g, per=per, layer=layer):
            return (layer, step_of(*g) // per, 0)

        def dst_index(*g, per=per):
            return (step_of(*g) // per, 0)
        in_specs.append(pl.BlockSpec((None, rows // nblk, cols), src_index))
        out_specs.append(pl.BlockSpec((rows // nblk, cols), dst_index))
        out_shape.append(jax.ShapeDtypeStruct((rows, cols), BF16))
    return in_specs, out_specs, out_shape


def _run_casts(src_refs, dst_refs):
    for src, dst in zip(src_refs, dst_refs):
        dst[...] = src[...].astype(BF16)


def _conv_kernel(*refs, tm, seq, ncast):
    hp_ref, h_ref, hn_ref, wb_ref, wc_ref, wx_ref, cw_ref, cb_ref = refs[:8]
    y_ref, h_scr = refs[8 + ncast], refs[-1]
    _run_casts(refs[8:8 + ncast], refs[9 + ncast:-1])
    i = pl.program_id(0)
    j = pl.program_id(1)
    halo = BF16_ROWS

    @pl.when(j == 0)
    def _():
        h_scr[0:halo] = hp_ref[...]
        h_scr[halo:halo + tm] = h_ref[...]
        h_scr[halo + tm:] = hn_ref[...]

    hx = h_scr[...]
    u = _dot(hx, wc_ref[...]) * _dot(hx, wx_ref[...])
    rows = tm + 2 * halo
    u_up = pltpu.roll(u, 1, 0)[halo:halo + tm]
    u_dn = pltpu.roll(u, rows - 1, 0)[halo:halo + tm]
    row = lax.broadcasted_iota(jnp.int32, (tm, 1), 0)
    seq_start = (i * tm) % seq == 0
    seq_end = ((i + 1) * tm) % seq == 0
    u_up = jnp.where(jnp.logical_and(row == 0, seq_start), 0.0, u_up)
    u_dn = jnp.where(jnp.logical_and(row == tm - 1, seq_end), 0.0, u_dn)
    w = cw_ref[...]
    conv = u_up * w[0:1] + u[halo:halo + tm] * w[1:2] + u_dn * w[2:3] + cb_ref[...]
    y_ref[...] = (_dot(h_ref[...], wb_ref[...]) * conv).astype(BF16)


def _conv_branch(h, w_bf, conv_w, conv_b, layer, *, tm, tn, seq, cast=()):
    m, d = h.shape
    d_conv = conv_w.shape[-1]
    halo = BF16_ROWS
    off = (N_HEADS * HEAD_DIM + 2 * N_KV_HEADS * HEAD_DIM) // tn
    nc = d_conv // tn
    hb = tm // halo
    last_hb = m // halo - 1

    def wspec(k):
        return pl.BlockSpec((d, tn), lambda i, j: (0, off + k * nc + j))

    c_in, c_out, c_shape = _cast_plan(cast, (m // tm) * nc, lambda i, j: i * nc + j)
    out = pl.pallas_call(
        functools.partial(_conv_kernel, tm=tm, seq=seq, ncast=len(cast)),
        grid=(m // tm, nc),
        in_specs=[
            pl.BlockSpec((halo, d), lambda i, j: (jnp.maximum(i * hb - 1, 0), 0)),
            pl.BlockSpec((tm, d), lambda i, j: (i, 0)),
            pl.BlockSpec((halo, d), lambda i, j: (jnp.minimum((i + 1) * hb, last_hb), 0)),
            wspec(0), wspec(1), wspec(2),
            pl.BlockSpec((None, 3, tn), lambda i, j: (layer, 0, j)),
            pl.BlockSpec((None, 1, tn), lambda i, j: (layer, 0, j)),
        ] + c_in,
        out_specs=[pl.BlockSpec((tm, tn), lambda i, j: (i, j))] + c_out,
        out_shape=[jax.ShapeDtypeStruct((m, d_conv), BF16)] + c_shape,
        scratch_shapes=[pltpu.VMEM((tm + 2 * halo, d), BF16)],
        compiler_params=_params("arbitrary", "arbitrary"),
        name="conv_branch",
    )(h, h, h, w_bf, w_bf, w_bf, conv_w, conv_b, *(w for w, _ in cast))
    return out[0], tuple(out[1:])


def _group_attention(q_ref, rows, pieces, sinks, ones_sel, o_ref):
    blk = ATT_BLOCK
    npair = len(sinks) // 2
    qs = jnp.concatenate([q_ref[rows, jp * LANES:(jp + 1) * LANES] for jp in range(npair)], axis=0)
    probs, vals, sink_p = [], [], []
    for par in range(2):
        chunks = []
        for kvx, biases in pieces:
            s = _dot_nt(qs, kvx[:, par * LANES:(par + 1) * LANES])
            for ci in range(s.shape[1] // LANES):
                ch = s[:, ci * LANES:(ci + 1) * LANES]
                if biases is not None and biases[ci] is not None:
                    ch = ch + jnp.concatenate([biases[ci]] * npair, axis=0)
                chunks.append(ch)
            vals.append(kvx[:, (2 + par) * LANES:(3 + par) * LANES])
        sink = jnp.concatenate([jnp.full((blk, LANES), sinks[2 * jp + par], F32) for jp in range(npair)], axis=0)
        mx = jnp.maximum(jnp.max(functools.reduce(jnp.maximum, chunks), axis=-1, keepdims=True), sink)
        probs.extend(jnp.exp2(ch - mx).astype(BF16) for ch in chunks)
        sink_p.append(jnp.exp2(sink - mx))
    rhs = jnp.concatenate([jnp.concatenate(vals, axis=0), ones_sel], axis=1)
    out = _dot(jnp.concatenate(probs, axis=1), rhs)
    lane = lax.broadcasted_iota(jnp.int32, (npair * blk, LANES), 1)
    den = out[:, LANES:] + jnp.where(lane < HEAD_DIM, sink_p[0], sink_p[1])
    res = out[:, :LANES] / den
    for jp in range(npair):
        o_ref[rows, jp * LANES:(jp + 1) * LANES] = res[jp * blk:(jp + 1) * blk].astype(BF16)


def _ones_selector(n):
    row = lax.broadcasted_iota(jnp.int32, (2 * n, LANES), 0)
    lane = lax.broadcasted_iota(jnp.int32, (2 * n, LANES), 1)
    return jnp.where((row < n) == (lane < HEAD_DIM), 1.0, 0.0).astype(BF16)


def _attn_kernel(*refs, tq, nblk_seq, ncast):
    sink_ref, q_ref, kvp_ref, kvc_ref, kvn_ref, ctx_ref = refs[:6]
    o_ref, band_scr = refs[6 + ncast], refs[-1]
    _run_casts(refs[6:6 + ncast], refs[7 + ncast:-1])
    i = pl.program_id(1)
    h = pl.program_id(2)
    blk = ATT_BLOCK
    groups = N_HEADS // N_KV_HEADS
    band_scr[0:blk] = kvp_ref[...]
    band_scr[blk:blk + tq] = kvc_ref[...]
    band_scr[blk + tq:] = kvn_ref[...]
    ctx = ctx_ref[...]
    sinks = [sink_ref[h * groups + g] * LOG2E for g in range(groups)]
    r = lax.broadcasted_iota(jnp.int32, (blk, blk), 0)
    c = lax.broadcasted_iota(jnp.int32, (blk, blk), 1)
    bias_prev = jnp.where(c >= r, 0.0, NEG)
    bias_next = jnp.where(c <= r, 0.0, NEG)
    ones_sel = _ones_selector(ctx.shape[0] + 3 * blk)

    for sb in range(tq // blk):
        r0 = sb * blk
        gblk = i * (tq // blk) + sb
        pen_first = jnp.where(gblk == 0, NEG, 0.0)
        pen_last = jnp.where(gblk == nblk_seq - 1, NEG, 0.0)
        biases = (bias_prev + pen_first, None, bias_next + pen_last)
        band = band_scr[r0:r0 + 3 * blk, :]
        _group_attention(q_ref, pl.ds(r0, blk), ((ctx, None), (band, biases)), sinks, ones_sel, o_ref)


def _attention(q, kvx, kvx_ctx, sink, *, batch, seq, n_ctx, tq, cast=()):
    m, q_w = q.shape
    blk = ATT_BLOCK
    hw = q_w // N_KV_HEADS
    kw = 4 * LANES
    nb = seq // blk
    tpb = tq // blk

    nt = seq // tq
    c_in, c_out, c_shape = _cast_plan(cast, batch * nt * N_KV_HEADS,
                                      lambda b, i, h: (b * nt + i) * N_KV_HEADS + h)
    out = pl.pallas_call(
        functools.partial(_attn_kernel, tq=tq, nblk_seq=nb, ncast=len(cast)),
        grid=(batch, nt, N_KV_HEADS),
        in_specs=[
            pl.BlockSpec(memory_space=pltpu.SMEM),
            pl.BlockSpec((tq, hw), lambda b, i, h: (b * nt + i, h)),
            pl.BlockSpec((blk, kw), lambda b, i, h: (b * nb + jnp.maximum(i * tpb - 1, 0), h)),
            pl.BlockSpec((tq, kw), lambda b, i, h: (b * nt + i, h)),
            pl.BlockSpec((blk, kw), lambda b, i, h: (b * nb + jnp.minimum((i + 1) * tpb, nb - 1), h)),
            pl.BlockSpec((n_ctx, kw), lambda b, i, h: (b, h)),
        ] + c_in,
        out_specs=[pl.BlockSpec((tq, hw), lambda b, i, h: (b * nt + i, h))] + c_out,
        out_shape=[jax.ShapeDtypeStruct((m, q_w), BF16)] + c_shape,
        scratch_shapes=[pltpu.VMEM((tq + 2 * blk, kw), BF16)],
        compiler_params=_params("arbitrary", "arbitrary", "arbitrary"),
        name="band_attention",
    )(sink, q, kvx, kvx, kvx, kvx_ctx, *(w for w, _ in cast))
    return out[0], tuple(out[1:])


def _ctx_attn_kernel(sink_ref, q_ref, ctx_ref, o_ref, *, n_ctx):
    h = pl.program_id(1)
    groups = N_HEADS // N_KV_HEADS
    ctx = ctx_ref[...]
    sinks = [sink_ref[h * groups + g] * LOG2E for g in range(groups)]
    ones_sel = _ones_selector(n_ctx)
    for r0 in range(0, n_ctx, ATT_BLOCK):
        _group_attention(q_ref, pl.ds(r0, ATT_BLOCK), ((ctx, None),), sinks, ones_sel, o_ref)


def _ctx_attention(q, kvx_ctx, sink, *, batch, n_ctx):
    m, q_w = q.shape
    hw = q_w // N_KV_HEADS
    kw = 4 * LANES
    return pl.pallas_call(
        functools.partial(_ctx_attn_kernel, n_ctx=n_ctx),
        grid=(batch, N_KV_HEADS),
        in_specs=[
            pl.BlockSpec(memory_space=pltpu.SMEM),
            pl.BlockSpec((n_ctx, hw), lambda b, h: (b, h)),
            pl.BlockSpec((n_ctx, kw), lambda b, h: (b, h)),
        ],
        out_specs=pl.BlockSpec((n_ctx, hw), lambda b, h: (b, h)),
        out_shape=jax.ShapeDtypeStruct((m, q_w), BF16),
        compiler_params=_params("parallel", "arbitrary"),
        name="ctx_attention",
    )(sink, q, kvx_ctx)


def _merge_kernel(a_ref, y_ref, h_ref, wa_ref, wc_ref, wga_ref, wgc_ref, m_ref):
    h = h_ref[...]
    m = (_sigmoid(_dot(h, wga_ref[...])) * _dot(a_ref[...], wa_ref[...])
         + _sigmoid(_dot(h, wgc_ref[...])) * _dot(y_ref[...], wc_ref[...]))
    m_ref[...] = m.astype(BF16)


def _merge(attn, y, h, w_in_bf, wa_bf, wc_bf, layer, *, tm, tn):
    m, q_w = attn.shape
    d_conv = y.shape[1]
    d_in = h.shape[1]
    d = wa_bf.shape[-1]
    nn = d // tn
    gate_off = (q_w + 2 * N_KV_HEADS * HEAD_DIM + 3 * d_conv) // tn
    return pl.pallas_call(
        _merge_kernel,
        grid=(m // tm, nn),
        in_specs=[
            pl.BlockSpec((tm, q_w), lambda i, j: (i, 0)),
            pl.BlockSpec((tm, d_conv), lambda i, j: (i, 0)),
            pl.BlockSpec((tm, d_in), lambda i, j: (i, 0)),
            pl.BlockSpec((q_w, tn), lambda i, j: (0, j)),
            pl.BlockSpec((d_conv, tn), lambda i, j: (0, j)),
            pl.BlockSpec((d_in, tn), lambda i, j: (0, gate_off + j)),
            pl.BlockSpec((d_in, tn), lambda i, j: (0, gate_off + nn + j)),
        ],
        out_specs=pl.BlockSpec((tm, tn), lambda i, j: (i, j)),
        out_shape=jax.ShapeDtypeStruct((m, d), BF16),
        compiler_params=_params("parallel", "arbitrary"),
        name="merge",
    )(attn, y, h, wa_bf, wc_bf, w_in_bf, w_in_bf)


def _outproj_kernel(m_ref, w_ref, x_ref, g_ref, o_ref):
    o_ref[...] = x_ref[...] + g_ref[...] * _dot(m_ref[...], w_ref[...])


def _outproj(mm, xr, mods, wo_bf, layer, *, tm, tn, row_of_tile):
    m, d = xr.shape
    return pl.pallas_call(
        _outproj_kernel,
        grid=(m // tm, d // tn),
        in_specs=[
            pl.BlockSpec((tm, d), lambda i, j: (i, 0)),
            pl.BlockSpec((d, tn), lambda i, j: (0, j)),
            pl.BlockSpec((tm, tn), lambda i, j: (i, j)),
            pl.BlockSpec((None, None, None, 1, tn), lambda i, j: (layer, row_of_tile(i), 2, 0, j)),
        ],
        out_specs=pl.BlockSpec((tm, tn), lambda i, j: (i, j)),
        out_shape=jax.ShapeDtypeStruct((m, d), F32),
        compiler_params=_params("parallel", "arbitrary"),
        name="outproj",
    )(mm, wo_bf, xr, mods)


def _ffn_kernel(*refs, final, nf):
    if final:
        x_ref, g_ref, sh_ref, sc_ref, gt_ref, wg_ref, wu_ref, wo_ref, fg_ref, o_ref, h_scr = refs
    else:
        x_ref, g_ref, sh_ref, sc_ref, gt_ref, wg_ref, wu_ref, wo_ref, o_ref, h_scr = refs
    f = pl.program_id(1)
    rows, d = o_ref.shape
    rc = _tile(rows, FFN_ROW_CHUNK)
    cw = _tile(d, FFN_OUT_CHUNK)

    def step(first, last):
        for r0 in range(0, rows, rc):
            rs = slice(r0, r0 + rc)
            if first:
                h = _norm_mod(x_ref[rs], g_ref[...], sh_ref[...], sc_ref[...]).astype(BF16)
                h_scr[rs] = h
            else:
                h = h_scr[rs]
            gate = _dot(h, wg_ref[...])
            act = (gate * _sigmoid(gate) * _dot(h, wu_ref[...])).astype(BF16)
            for c0 in range(0, d, cw):
                part = _dot(act, wo_ref[:, c0:c0 + cw])
                if first:
                    o_ref[rs, c0:c0 + cw] = part
                else:
                    o_ref[rs, c0:c0 + cw] += part
            if last:
                xn = x_ref[rs] + gt_ref[...] * o_ref[rs]
                if final:
                    xn = xn * lax.rsqrt(jnp.mean(xn * xn, axis=-1, keepdims=True) + EPS) * fg_ref[...]
                o_ref[rs] = xn

    if nf == 1:
        step(True, True)
    else:
        pl.when(f == 0)(lambda: step(True, False))
        pl.when(jnp.logical_and(f > 0, f < nf - 1))(lambda: step(False, False))
        pl.when(f == nf - 1)(lambda: step(False, True))


def _ffn(xr, norm_g, mods, wi_bf, wo_bf, layer, *, tm, tf, row_of_tile, final_g):
    m, d = xr.shape
    d_ff = wo_bf.shape[0]
    nf = d_ff // tf
    final = final_g is not None
    in_specs = [
        pl.BlockSpec((tm, d), lambda i, f: (i, 0)),
        pl.BlockSpec((None, 1, d), lambda i, f: (layer, 0, 0)),
        pl.BlockSpec((None, None, None, 1, d), _mod_spec(layer, 3, row_of_tile)),
        pl.BlockSpec((None, None, None, 1, d), _mod_spec(layer, 4, row_of_tile)),
        pl.BlockSpec((None, None, None, 1, d), _mod_spec(layer, 5, row_of_tile)),
        pl.BlockSpec((d, tf), lambda i, f: (0, f)),
        pl.BlockSpec((d, tf), lambda i, f: (0, nf + f)),
        pl.BlockSpec((tf, d), lambda i, f: (f, 0)),
    ]
    args = [xr, norm_g, mods, mods, mods, wi_bf, wi_bf, wo_bf]
    if final:
        in_specs.append(pl.BlockSpec((1, d), lambda i, f: (0, 0)))
        args.append(final_g)
    return pl.pallas_call(
        functools.partial(_ffn_kernel, final=final, nf=nf),
        grid=(m // tm, nf),
        in_specs=in_specs,
        out_specs=pl.BlockSpec((tm, d), lambda i, f: (i, 0)),
        out_shape=jax.ShapeDtypeStruct((m, d), F32),
        scratch_shapes=[pltpu.VMEM((tm, d), BF16)],
        compiler_params=_params("parallel", "arbitrary"),
        name="ffn_final" if final else "ffn",
    )(*args)


def _rope_tables(seq):
    quarter = HEAD_DIM // 4
    t = np.arange(seq)
    pos = np.stack([t // GRID_W, t % GRID_W], axis=1).astype(np.float32)
    d = np.arange(LANES) % HEAD_DIM
    axis = d // (HEAD_DIM // 2)
    first_half = (d // quarter) % 2 == 0
    freqs = np.float32(ROPE_THETA) ** (-np.arange(quarter, dtype=np.float32) / np.float32(quarter))
    ang = (pos[:, axis] * freqs[d % quarter][None, :]).astype(np.float32)
    cos, sin = np.cos(ang), np.sin(ang)
    fh = first_half[None, :]
    zero = np.float32(0.0)
    return tuple(jnp.asarray(t, F32) for t in (cos, np.where(fh, -sin, zero), np.where(fh, zero, sin)))


def _tile(n, pref):
    return pref if n % pref == 0 else n


def kernel(x, c, ctx, c_ctx, ada_w, ada_b, norm1_g, norm2_g, w_in, conv_w, conv_b, sink,
           w_attn_out, w_conv_out, w_o, w_ffn_in, w_ffn_out, final_g):
    batch, seq, d = x.shape
    n_ctx = ctx.shape[1]
    depth = ada_w.shape[0]
    d_ff = w_ffn_out.shape[1]
    assert batch + 1 <= MOD_ROWS and seq % ATT_BLOCK == 0 and n_ctx % ATT_BLOCK == 0

    cc = jnp.zeros((MOD_ROWS, d), F32).at[:batch].set(c).at[batch].set(c_ctx)
    mods = _ada(cc, ada_w, ada_b).reshape(depth, MOD_ROWS, 6, 1, d)

    n1 = norm1_g.reshape(depth, 1, d)
    n2 = norm2_g.reshape(depth, 1, d)
    conv_b3 = conv_b.reshape(depth, 1, -1)
    fg = final_g.reshape(1, d)
    rope_tabs = _rope_tables(seq)

    tm = _tile(seq, 1024)
    tm_out = _tile(seq, 512)
    tn = 2 * N_KV_HEADS * HEAD_DIM
    tf = _tile(d_ff, tn)
    tq = _tile(seq, 2048)
    m_ctx = batch * n_ctx
    lat_row = lambda t: (lambda i: i // (seq // t))
    ctx_row = lambda i: batch

    late = lambda l: tuple((w, l) for w in (w_attn_out, w_conv_out, w_o, w_ffn_in, w_ffn_out))
    wts = {0: (w_in[0].astype(BF16),)}

    xl = x.reshape(batch * seq, d)
    xc = ctx.reshape(m_ctx, d)
    for l in range(depth):
        last = l == depth - 1
        w_in_bf = wts[l][0]
        q_c, kvx_c, h_c = _inproj(xc, n1, mods, w_in_bf, l, tm=m_ctx, row_of_tile=ctx_row, rope_tabs=None,
                                  seq=n_ctx, with_q=not last)
        q, kvx, h = _inproj(xl, n1, mods, w_in_bf, l, tm=tm, row_of_tile=lat_row(tm),
                            rope_tabs=rope_tabs, seq=seq, with_q=True)
        y, cast0 = _conv_branch(h, w_in_bf, conv_w, conv_b3, l, tm=tm, tn=tn, seq=seq,
                                cast=late(0) if l == 0 else ())
        if l == 0:
            wts[0] = wts[0] + cast0
        attn, cast1 = _attention(q, kvx, kvx_c, sink[l], batch=batch, seq=seq, n_ctx=n_ctx, tq=tq,
                                 cast=() if last else ((w_in, l + 1),) + late(l + 1))
        if not last:
            wts[l + 1] = cast1
        _, wa_bf, wc_bf, wo_bf, wfi_bf, wfo_bf = wts[l]
        mm = _merge(attn, y, h, w_in_bf, wa_bf, wc_bf, l, tm=tm, tn=tn)
        xl = _outproj(mm, xl, mods, wo_bf, l, tm=tm_out, tn=d, row_of_tile=lat_row(tm_out))

        if not last:
            y_c, _ = _conv_branch(h_c, w_in_bf, conv_w, conv_b3, l, tm=n_ctx, tn=tn, seq=n_ctx)
            attn_c = _ctx_attention(q_c, kvx_c, sink[l], batch=batch, n_ctx=n_ctx)
            mm_c = _merge(attn_c, y_c, h_c, w_in_bf, wa_bf, wc_bf, l, tm=m_ctx, tn=tn)
            xc = _outproj(mm_c, xc, mods, wo_bf, l, tm=m_ctx, tn=d, row_of_tile=ctx_row)
            xc = _ffn(xc, n2, mods, wfi_bf, wfo_bf, l, tm=m_ctx, tf=tf, row_of_tile=ctx_row, final_g=None)

        xl = _ffn(xl, n2, mods, wfi_bf, wfo_bf, l, tm=tm, tf=tf, row_of_tile=lat_row(tm),
                  final_g=fg if last else None)
    return xl.reshape(batch, seq, d)
```

```python
import functools

import jax
import jax.numpy as jnp
import numpy as np
from jax import lax
from jax.experimental import pallas as pl
from jax.experimental.pallas import tpu as pltpu

GRID_W = 64
N_HEADS = 32
N_KV_HEADS = 4
HEAD_DIM = 64
WINDOW = 128
ROPE_THETA = 10000.0
EPS = 1e-6
NEG = -1e30
LOG2E = 1.4426950408889634
Q_SCALE = HEAD_DIM ** -0.5 * LOG2E

LANES = 128
BF16_ROWS = 16
ROW_CHUNK = 256
V7X_VMEM_BYTES = 64 * 1024 * 1024
VMEM_LIMIT_BYTES = V7X_VMEM_BYTES * 15 // 16
ATT_BLOCK = 128
MOD_ROWS = 8
FFN_OUT_CHUNK = 512
FFN_ROW_CHUNK = 512

F32 = jnp.float32
BF16 = jnp.bfloat16


def _params(*sem):
    return pltpu.CompilerParams(dimension_semantics=sem, vmem_limit_bytes=VMEM_LIMIT_BYTES)


def _sigmoid(v):
    return 1.0 / (1.0 + jnp.exp(-v))


def _norm_mod(x, gain, shift):
    y = x * lax.rsqrt(jnp.mean(x * x, axis=-1, keepdims=True) + EPS)
    return y * gain + shift


def _dot(a, b):
    return jnp.dot(a, b, preferred_element_type=F32)


def _dot_nt(a, b):
    return lax.dot_general(a, b, (((1,), (1,)), ((), ())), preferred_element_type=F32)


def _ada_kernel(c_ref, w_ref, b_ref, o_ref):
    c = c_ref[...]
    s = c * _sigmoid(c)
    o_ref[...] = _dot(s.astype(BF16), w_ref[...].astype(BF16)) + b_ref[...]


def _ada(cc, ada_w, ada_b):
    depth, d, n = ada_w.shape
    tn = _tile(n, 1024)
    return pl.pallas_call(
        _ada_kernel,
        grid=(depth, n // tn),
        in_specs=[
            pl.BlockSpec((MOD_ROWS, d), lambda l, j: (0, 0)),
            pl.BlockSpec((None, d, tn), lambda l, j: (l, 0, j)),
            pl.BlockSpec((None, 1, tn), lambda l, j: (l, 0, j)),
        ],
        out_specs=pl.BlockSpec((None, MOD_ROWS, tn), lambda l, j: (l, 0, j)),
        out_shape=jax.ShapeDtypeStruct((depth, MOD_ROWS, n), F32),
        compiler_params=_params("parallel", "parallel"),
        name="ada_mod",
    )(cc, ada_w, ada_b.reshape(depth, 1, n))


def _mod_spec(layer, which, row_of_tile):
    def index(i, j):
        return (layer, row_of_tile(i), which, 0, 0)
    return index


def _rope(v, cos, sin_a, sin_b):
    return v * cos + pltpu.roll(v, LANES - 16, 1) * sin_a + pltpu.roll(v, 16, 1) * sin_b


def _inproj_kernel(*refs, use_rope, with_q, nx):
    refs = list(refs)
    x_refs = refs[:nx]
    del refs[:nx]
    g_ref, sh_ref, sc_ref, wkv_ref = refs[:4]
    del refs[:4]
    wq_ref = refs.pop(0) if with_q else None
    if use_rope:
        cos_ref, sa_ref, sb_ref = refs[:3]
        del refs[:3]
    q_ref = refs.pop(0) if with_q else None
    kvx_ref, h_ref = refs
    j = pl.program_id(1)

    rows = h_ref.shape[0]
    rc = _tile(rows, ROW_CHUNK)

    @pl.when(j == 0)
    def _():
        gain, sh = g_ref[...] * (1.0 + sc_ref[...]), sh_ref[...]
        lo_mask = lax.broadcasted_iota(jnp.int32, (rc, LANES), 1) < HEAD_DIM
        for ci, r0 in enumerate(range(0, rows, rc)):
            rs = slice(r0, r0 + rc)
            h = _norm_mod(x_refs[ci][...], gain, sh).astype(BF16)
            h_ref[rs] = h
            acc = _dot(h, wkv_ref[...])
            kvw = acc.shape[1] // 2
            for part in range(2):
                for pr in range(kvw // LANES):
                    v = acc[:, part * kvw + pr * LANES: part * kvw + (pr + 1) * LANES]
                    if part == 0 and use_rope:
                        v = _rope(v, cos_ref[rs], sa_ref[rs], sb_ref[rs])
                    r = pltpu.roll(v, HEAD_DIM, 1)
                    tiles = ((jnp.where(lo_mask, v, 0.0), jnp.where(lo_mask, 0.0, r)),
                             (jnp.where(lo_mask, r, 0.0), jnp.where(lo_mask, 0.0, v)))
                    for odd in range(2):
                        head = 2 * pr + odd
                        for hi in range(2):
                            col = head * 4 * LANES + (2 * part + hi) * LANES
                            kvx_ref[rs, col:col + LANES] = tiles[odd][hi].astype(BF16)

    if with_q:
        for r0 in range(0, rows, rc):
            acc = _dot(h_ref[r0:r0 + rc], wq_ref[...])
            for cg in range(acc.shape[1] // LANES):
                v = acc[:, cg * LANES:(cg + 1) * LANES]
                if use_rope:
                    v = _rope(v, cos_ref[r0:r0 + rc], sa_ref[r0:r0 + rc], sb_ref[r0:r0 + rc])
                q_ref[r0:r0 + rc, cg * LANES:(cg + 1) * LANES] = (v * Q_SCALE).astype(BF16)


def _inproj(xr, norm_g, mods, w_bf, layer, *, tm, row_of_tile, rope_tabs, seq, with_q):
    m, d = xr.shape
    q_w = N_HEADS * HEAD_DIM
    kv_w = N_KV_HEADS * HEAD_DIM
    tnq = _tile(q_w, 512)
    use_rope = rope_tabs is not None
    kvx_w = N_KV_HEADS * 4 * LANES

    rc = _tile(tm, ROW_CHUNK)
    nx = tm // rc
    last_tile = m // tm - 1

    def x_spec(k):
        def index(i, j):
            return (jnp.minimum(i + (j > k).astype(jnp.int32), last_tile) * nx + k, 0)
        return pl.BlockSpec((rc, d), index)

    in_specs = [x_spec(k) for k in range(nx)] + [
        pl.BlockSpec((None, 1, d), lambda i, j: (layer, 0, 0)),
        pl.BlockSpec((None, None, None, 1, d), _mod_spec(layer, 0, row_of_tile)),
        pl.BlockSpec((None, None, None, 1, d), _mod_spec(layer, 1, row_of_tile)),
        pl.BlockSpec((d, 2 * kv_w), lambda i, j: (0, q_w // (2 * kv_w))),
    ]
    args = [xr] * nx + [norm_g, mods, mods, w_bf]
    if with_q:
        in_specs.append(pl.BlockSpec((d, tnq), lambda i, j: (0, j)))
        args.append(w_bf)
    if use_rope:
        tps = seq // tm
        for t in rope_tabs:
            in_specs.append(pl.BlockSpec((tm, LANES), lambda i, j: (i % tps, 0)))
            args.append(t)

    out_specs = [pl.BlockSpec((tm, kvx_w), lambda i, j: (i, 0)),
                 pl.BlockSpec((tm, d), lambda i, j: (i, 0))]
    out_shape = [jax.ShapeDtypeStruct((m, kvx_w), BF16), jax.ShapeDtypeStruct((m, d), BF16)]
    if with_q:
        out_specs.insert(0, pl.BlockSpec((tm, tnq), lambda i, j: (i, j)))
        out_shape.insert(0, jax.ShapeDtypeStruct((m, q_w), BF16))
    out = pl.pallas_call(
        functools.partial(_inproj_kernel, use_rope=use_rope, with_q=with_q, nx=nx),
        grid=(m // tm, q_w // tnq if with_q else 1),
        in_specs=in_specs,
        out_specs=out_specs,
        out_shape=out_shape,
        compiler_params=_params("parallel", "arbitrary"),
        name="inproj" if with_q else "inproj_kv",
    )(*args)
    return out if with_q else (None,) + tuple(out)


def _cast_plan(weights, n_steps, step_of):
    in_specs, out_specs, out_shape = [], [], []
    for w, layer in weights:
        _, rows, cols = w.shape
        nblk = n_steps
        while rows % (nblk * BF16_ROWS):
            assert nblk % 2 == 0, (rows, n_steps)
            nblk //= 2
        per = n_steps // nblk

        def src_index(*g, per=per, layer=layer):
            return (layer, step_of(*g) // per, 0)

        def dst_index(*g, per=per):
            return (step_of(*g) // per, 0)
        in_specs.append(pl.BlockSpec((None, rows // nblk, cols), src_index))
        out_specs.append(pl.BlockSpec((rows // nblk, cols), dst_index))
        out_shape.append(jax.ShapeDtypeStruct((rows, cols), BF16))
    return in_specs, out_specs, out_shape


def _run_casts(src_refs, dst_refs):
    for src, dst in zip(src_refs, dst_refs):
        dst[...] = src[...].astype(BF16)


def _conv_kernel(*refs, tm, seq, ncast):
    hp_ref, h_ref, hn_ref, wb_ref, wc_ref, wx_ref, cw_ref, cb_ref = refs[:8]
    y_ref, h_scr = refs[8 + ncast], refs[-1]
    _run_casts(refs[8:8 + ncast], refs[9 + ncast:-1])
    i = pl.program_id(0)
    j = pl.program_id(1)
    halo = BF16_ROWS

    @pl.when(j == 0)
    def _():
        h_scr[0:halo] = hp_ref[...]
        h_scr[halo:halo + tm] = h_ref[...]
        h_scr[halo + tm:] = hn_ref[...]

    hx = h_scr[...]
    u = _dot(hx, wc_ref[...]) * _dot(hx, wx_ref[...])
    rows = tm + 2 * halo
    u_up = pltpu.roll(u, 1, 0)[halo:halo + tm]
    u_dn = pltpu.roll(u, rows - 1, 0)[halo:halo + tm]
    row = lax.broadcasted_iota(jnp.int32, (tm, 1), 0)
    seq_start = (i * tm) % seq == 0
    seq_end = ((i + 1) * tm) % seq == 0
    u_up = jnp.where(jnp.logical_and(row == 0, seq_start), 0.0, u_up)
    u_dn = jnp.where(jnp.logical_and(row == tm - 1, seq_end), 0.0, u_dn)
    w = cw_ref[...]
    conv = u_up * w[0:1] + u[halo:halo + tm] * w[1:2] + u_dn * w[2:3] + cb_ref[...]
    y_ref[...] = (_dot(h_ref[...], wb_ref[...]) * conv).astype(BF16)


def _conv_branch(h, w_bf, conv_w, conv_b, layer, *, tm, tn, seq, cast=()):
    m, d = h.shape
    d_conv = conv_w.shape[-1]
    halo = BF16_ROWS
    off = (N_HEADS * HEAD_DIM + 2 * N_KV_HEADS * HEAD_DIM) // tn
    nc = d_conv // tn
    hb = tm // halo
    last_hb = m // halo - 1

    def wspec(k):
        return pl.BlockSpec((d, tn), lambda i, j: (0, off + k * nc + j))

    c_in, c_out, c_shape = _cast_plan(cast, (m // tm) * nc, lambda i, j: i * nc + j)
    out = pl.pallas_call(
        functools.partial(_conv_kernel, tm=tm, seq=seq, ncast=len(cast)),
        grid=(m // tm, nc),
        in_specs=[
            pl.BlockSpec((halo, d), lambda i, j: (jnp.maximum(i * hb - 1, 0), 0)),
            pl.BlockSpec((tm, d), lambda i, j: (i, 0)),
            pl.BlockSpec((halo, d), lambda i, j: (jnp.minimum((i + 1) * hb, last_hb), 0)),
            wspec(0), wspec(1), wspec(2),
            pl.BlockSpec((None, 3, tn), lambda i, j: (layer, 0, j)),
            pl.BlockSpec((None, 1, tn), lambda i, j: (layer, 0, j)),
        ] + c_in,
        out_specs=[pl.BlockSpec((tm, tn), lambda i, j: (i, j))] + c_out,
        out_shape=[jax.ShapeDtypeStruct((m, d_conv), BF16)] + c_shape,
        scratch_shapes=[pltpu.VMEM((tm + 2 * halo, d), BF16)],
        compiler_params=_params("arbitrary", "arbitrary"),
        name="conv_branch",
    )(h, h, h, w_bf, w_bf, w_bf, conv_w, conv_b, *(w for w, _ in cast))
    return out[0], tuple(out[1:])


def _group_attention(q_ref, rows, pieces, sinks, ones_sel, o_ref):
    blk = ATT_BLOCK
    npair = len(sinks) // 2
    qs = jnp.concatenate([q_ref[rows, jp * LANES:(jp + 1) * LANES] for jp in range(npair)], axis=0)
    probs, vals, sink_p = [], [], []
    for par in range(2):
        chunks = []
        for kvx, biases in pieces:
            s = _dot_nt(qs, kvx[:, par * LANES:(par + 1) * LANES])
            for ci in range(s.shape[1] // LANES):
                ch = s[:, ci * LANES:(ci + 1) * LANES]
                if biases is not None and biases[ci] is not None:
                    ch = ch + jnp.concatenate([biases[ci]] * npair, axis=0)
                chunks.append(ch)
            vals.append(kvx[:, (2 + par) * LANES:(3 + par) * LANES])
        sink = jnp.concatenate([jnp.full((blk, LANES), sinks[2 * jp + par], F32) for jp in range(npair)], axis=0)
        mx = jnp.maximum(jnp.max(functools.reduce(jnp.maximum, chunks), axis=-1, keepdims=True), sink)
        probs.extend(jnp.exp2(ch - mx).astype(BF16) for ch in chunks)
        sink_p.append(jnp.exp2(sink - mx))
    rhs = jnp.concatenate([jnp.concatenate(vals, axis=0), ones_sel], axis=1)
    out = _dot(jnp.concatenate(probs, axis=1), rhs)
    lane = lax.broadcasted_iota(jnp.int32, (npair * blk, LANES), 1)
    den = out[:, LANES:] + jnp.where(lane < HEAD_DIM, sink_p[0], sink_p[1])
    res = out[:, :LANES] / den
    for jp in range(npair):
        o_ref[rows, jp * LANES:(jp + 1) * LANES] = res[jp * blk:(jp + 1) * blk].astype(BF16)


def _ones_selector(n):
    row = lax.broadcasted_iota(jnp.int32, (2 * n, LANES), 0)
    lane = lax.broadcasted_iota(jnp.int32, (2 * n, LANES), 1)
    return jnp.where((row < n) == (lane < HEAD_DIM), 1.0, 0.0).astype(BF16)


def _attn_kernel(*refs, tq, nblk_seq, ncast):
    sink_ref, q_ref, kvp_ref, kvc_ref, kvn_ref, ctx_ref = refs[:6]
    o_ref, band_scr = refs[6 + ncast], refs[-1]
    _run_casts(refs[6:6 + ncast], refs[7 + ncast:-1])
    i = pl.program_id(1)
    h = pl.program_id(2)
    blk = ATT_BLOCK
    groups = N_HEADS // N_KV_HEADS
    band_scr[0:blk] = kvp_ref[...]
    band_scr[blk:blk + tq] = kvc_ref[...]
    band_scr[blk + tq:] = kvn_ref[...]
    ctx = ctx_ref[...]
    sinks = [sink_ref[h * groups + g] * LOG2E for g in range(groups)]
    r = lax.broadcasted_iota(jnp.int32, (blk, blk), 0)
    c = lax.broadcasted_iota(jnp.int32, (blk, blk), 1)
    bias_prev = jnp.where(c >= r, 0.0, NEG)
    bias_next = jnp.where(c <= r, 0.0, NEG)
    ones_sel = _ones_selector(ctx.shape[0] + 3 * blk)

    for sb in range(tq // blk):
        r0 = sb * blk
        gblk = i * (tq // blk) + sb
        pen_first = jnp.where(gblk == 0, NEG, 0.0)
        pen_last = jnp.where(gblk == nblk_seq - 1, NEG, 0.0)
        biases = (bias_prev + pen_first, None, bias_next + pen_last)
        band = band_scr[r0:r0 + 3 * blk, :]
        _group_attention(q_ref, pl.ds(r0, blk), ((ctx, None), (band, biases)), sinks, ones_sel, o_ref)


def _attention(q, kvx, kvx_ctx, sink, *, batch, seq, n_ctx, tq, cast=()):
    m, q_w = q.shape
    blk = ATT_BLOCK
    hw = q_w // N_KV_HEADS
    kw = 4 * LANES
    nb = seq // blk
    tpb = tq // blk

    nt = seq // tq
    c_in, c_out, c_shape = _cast_plan(cast, batch * nt * N_KV_HEADS,
                                      lambda b, i, h: (b * nt + i) * N_KV_HEADS + h)
    out = pl.pallas_call(
        functools.partial(_attn_kernel, tq=tq, nblk_seq=nb, ncast=len(cast)),
        grid=(batch, nt, N_KV_HEADS),
        in_specs=[
            pl.BlockSpec(memory_space=pltpu.SMEM),
            pl.BlockSpec((tq, hw), lambda b, i, h: (b * nt + i, h)),
            pl.BlockSpec((blk, kw), lambda b, i, h: (b * nb + jnp.maximum(i * tpb - 1, 0), h)),
            pl.BlockSpec((tq, kw), lambda b, i, h: (b * nt + i, h)),
            pl.BlockSpec((blk, kw), lambda b, i, h: (b * nb + jnp.minimum((i + 1) * tpb, nb - 1), h)),
            pl.BlockSpec((n_ctx, kw), lambda b, i, h: (b, h)),
        ] + c_in,
        out_specs=[pl.BlockSpec((tq, hw), lambda b, i, h: (b * nt + i, h))] + c_out,
        out_shape=[jax.ShapeDtypeStruct((m, q_w), BF16)] + c_shape,
        scratch_shapes=[pltpu.VMEM((tq + 2 * blk, kw), BF16)],
        compiler_params=_params("arbitrary", "arbitrary", "arbitrary"),
        name="band_attention",
    )(sink, q, kvx, kvx, kvx, kvx_ctx, *(w for w, _ in cast))
    return out[0], tuple(out[1:])


def _ctx_attn_kernel(sink_ref, q_ref, ctx_ref, o_ref, *, n_ctx):
    h = pl.program_id(1)
    groups = N_HEADS // N_KV_HEADS
    ctx = ctx_ref[...]
    sinks = [sink_ref[h * groups + g] * LOG2E for g in range(groups)]
    ones_sel = _ones_selector(n_ctx)
    for r0 in range(0, n_ctx, ATT_BLOCK):
        _group_attention(q_ref, pl.ds(r0, ATT_BLOCK), ((ctx, None),), sinks, ones_sel, o_ref)


def _ctx_attention(q, kvx_ctx, sink, *, batch, n_ctx):
    m, q_w = q.shape
    hw = q_w // N_KV_HEADS
    kw = 4 * LANES
    return pl.pallas_call(
        functools.partial(_ctx_attn_kernel, n_ctx=n_ctx),
        grid=(batch, N_KV_HEADS),
        in_specs=[
            pl.BlockSpec(memory_space=pltpu.SMEM),
            pl.BlockSpec((n_ctx, hw), lambda b, h: (b, h)),
            pl.BlockSpec((n_ctx, kw), lambda b, h: (b, h)),
        ],
        out_specs=pl.BlockSpec((n_ctx, hw), lambda b, h: (b, h)),
        out_shape=jax.ShapeDtypeStruct((m, q_w), BF16),
        compiler_params=_params("parallel", "arbitrary"),
        name="ctx_attention",
    )(sink, q, kvx_ctx)


def _merge_kernel(a_ref, y_ref, h_ref, wa_ref, wc_ref, wga_ref, wgc_ref, m_ref):
    h = h_ref[...]
    m = (_sigmoid(_dot(h, wga_ref[...])) * _dot(a_ref[...], wa_ref[...])
         + _sigmoid(_dot(h, wgc_ref[...])) * _dot(y_ref[...], wc_ref[...]))
    m_ref[...] = m.astype(BF16)


def _merge(attn, y, h, w_in_bf, wa_bf, wc_bf, layer, *, tm, tn):
    m, q_w = attn.shape
    d_conv = y.shape[1]
    d_in = h.shape[1]
    d = wa_bf.shape[-1]
    nn = d // tn
    gate_off = (q_w + 2 * N_KV_HEADS * HEAD_DIM + 3 * d_conv) // tn
    return pl.pallas_call(
        _merge_kernel,
        grid=(m // tm, nn),
        in_specs=[
            pl.BlockSpec((tm, q_w), lambda i, j: (i, 0)),
            pl.BlockSpec((tm, d_conv), lambda i, j: (i, 0)),
            pl.BlockSpec((tm, d_in), lambda i, j: (i, 0)),
            pl.BlockSpec((q_w, tn), lambda i, j: (0, j)),
            pl.BlockSpec((d_conv, tn), lambda i, j: (0, j)),
            pl.BlockSpec((d_in, tn), lambda i, j: (0, gate_off + j)),
            pl.BlockSpec((d_in, tn), lambda i, j: (0, gate_off + nn + j)),
        ],
        out_specs=pl.BlockSpec((tm, tn), lambda i, j: (i, j)),
        out_shape=jax.ShapeDtypeStruct((m, d), BF16),
        compiler_params=_params("parallel", "arbitrary"),
        name="merge",
    )(attn, y, h, wa_bf, wc_bf, w_in_bf, w_in_bf)


def _outproj_kernel(m_ref, w_ref, x_ref, g_ref, o_ref):
    o_ref[...] = x_ref[...] + g_ref[...] * _dot(m_ref[...], w_ref[...])


def _outproj(mm, xr, mods, wo_bf, layer, *, tm, tn, row_of_tile):
    m, d = xr.shape
    return pl.pallas_call(
        _outproj_kernel,
        grid=(m // tm, d // tn),
        in_specs=[
            pl.BlockSpec((tm, d), lambda i, j: (i, 0)),
            pl.BlockSpec((d, tn), lambda i, j: (0, j)),
            pl.BlockSpec((tm, tn), lambda i, j: (i, j)),
            pl.BlockSpec((None, None, None, 1, tn), lambda i, j: (layer, row_of_tile(i), 2, 0, j)),
        ],
        out_specs=pl.BlockSpec((tm, tn), lambda i, j: (i, j)),
        out_shape=jax.ShapeDtypeStruct((m, d), F32),
        compiler_params=_params("parallel", "arbitrary"),
        name="outproj",
    )(mm, wo_bf, xr, mods)


def _ffn_kernel(*refs, final, nf):
    if final:
        x_ref, g_ref, sh_ref, sc_ref, gt_ref, wg_ref, wu_ref, wo_ref, fg_ref, o_ref, h_scr = refs
    else:
        x_ref, g_ref, sh_ref, sc_ref, gt_ref, wg_ref, wu_ref, wo_ref, o_ref, h_scr = refs
    f = pl.program_id(1)
    rows, d = o_ref.shape
    cw = _tile(d, FFN_OUT_CHUNK)

    def step(first, last):
        rc = _tile(rows, FFN_ROW_CHUNK) if first or last else rows
        for r0 in range(0, rows, rc):
            rs = slice(r0, r0 + rc)
            if first:
                h = _norm_mod(x_ref[rs], g_ref[...] * (1.0 + sc_ref[...]), sh_ref[...]).astype(BF16)
                h_scr[rs] = h
            else:
                h = h_scr[rs]
            gate = _dot(h, wg_ref[...])
            act = (gate * _sigmoid(gate) * _dot(h, wu_ref[...])).astype(BF16)
            for c0 in range(0, d, cw):
                part = _dot(act, wo_ref[:, c0:c0 + cw])
                if first:
                    o_ref[rs, c0:c0 + cw] = part
                else:
                    o_ref[rs, c0:c0 + cw] += part
            if last:
                xn = x_ref[rs] + gt_ref[...] * o_ref[rs]
                if final:
                    xn = xn * lax.rsqrt(jnp.mean(xn * xn, axis=-1, keepdims=True) + EPS) * fg_ref[...]
                o_ref[rs] = xn

    if nf == 1:
        step(True, True)
    else:
        pl.when(f == 0)(lambda: step(True, False))
        pl.when(jnp.logical_and(f > 0, f < nf - 1))(lambda: step(False, False))
        pl.when(f == nf - 1)(lambda: step(False, True))


def _ffn(xr, norm_g, mods, wi_bf, wo_bf, layer, *, tm, tf, row_of_tile, final_g):
    m, d = xr.shape
    d_ff = wo_bf.shape[0]
    nf = d_ff // tf
    final = final_g is not None
    in_specs = [
        pl.BlockSpec((tm, d), lambda i, f: (i, 0)),
        pl.BlockSpec((None, 1, d), lambda i, f: (layer, 0, 0)),
        pl.BlockSpec((None, None, None, 1, d), _mod_spec(layer, 3, row_of_tile)),
        pl.BlockSpec((None, None, None, 1, d), _mod_spec(layer, 4, row_of_tile)),
        pl.BlockSpec((None, None, None, 1, d), _mod_spec(layer, 5, row_of_tile)),
        pl.BlockSpec((d, tf), lambda i, f: (0, f)),
        pl.BlockSpec((d, tf), lambda i, f: (0, nf + f)),
        pl.BlockSpec((tf, d), lambda i, f: (f, 0)),
    ]
    args = [xr, norm_g, mods, mods, mods, wi_bf, wi_bf, wo_bf]
    if final:
        in_specs.append(pl.BlockSpec((1, d), lambda i, f: (0, 0)))
        args.append(final_g)
    return pl.pallas_call(
        functools.partial(_ffn_kernel, final=final, nf=nf),
        grid=(m // tm, nf),
        in_specs=in_specs,
        out_specs=pl.BlockSpec((tm, d), lambda i, f: (i, 0)),
        out_shape=jax.ShapeDtypeStruct((m, d), F32),
        scratch_shapes=[pltpu.VMEM((tm, d), BF16)],
        compiler_params=_params("parallel", "arbitrary"),
        name="ffn_final" if final else "ffn",
    )(*args)


def _rope_tables(seq):
    quarter = HEAD_DIM // 4
    t = np.arange(seq)
    pos = np.stack([t // GRID_W, t % GRID_W], axis=1).astype(np.float32)
    d = np.arange(LANES) % HEAD_DIM
    axis = d // (HEAD_DIM // 2)
    first_half = (d // quarter) % 2 == 0
    freqs = np.float32(ROPE_THETA) ** (-np.arange(quarter, dtype=np.float32) / np.float32(quarter))
    ang = (pos[:, axis] * freqs[d % quarter][None, :]).astype(np.float32)
    cos, sin = np.cos(ang), np.sin(ang)
    fh = first_half[None, :]
    zero = np.float32(0.0)
    return tuple(jnp.asarray(t, F32) for t in (cos, np.where(fh, -sin, zero), np.where(fh, zero, sin)))


def _tile(n, pref):
    return pref if n % pref == 0 else n


def kernel(x, c, ctx, c_ctx, ada_w, ada_b, norm1_g, norm2_g, w_in, conv_w, conv_b, sink,
           w_attn_out, w_conv_out, w_o, w_ffn_in, w_ffn_out, final_g):
    batch, seq, d = x.shape
    n_ctx = ctx.shape[1]
    depth = ada_w.shape[0]
    d_ff = w_ffn_out.shape[1]
    assert batch + 1 <= MOD_ROWS and seq % ATT_BLOCK == 0 and n_ctx % ATT_BLOCK == 0

    cc = jnp.zeros((MOD_ROWS, d), F32).at[:batch].set(c).at[batch].set(c_ctx)
    mods = _ada(cc, ada_w, ada_b).reshape(depth, MOD_ROWS, 6, 1, d)

    n1 = norm1_g.reshape(depth, 1, d)
    n2 = norm2_g.reshape(depth, 1, d)
    conv_b3 = conv_b.reshape(depth, 1, -1)
    fg = final_g.reshape(1, d)
    rope_tabs = _rope_tables(seq)

    tm = _tile(seq, 1024)
    tm_out = _tile(seq, 512)
    tn = 2 * N_KV_HEADS * HEAD_DIM
    tf = _tile(d_ff, tn)
    tq = _tile(seq, 2048)
    m_ctx = batch * n_ctx
    lat_row = lambda t: (lambda i: i // (seq // t))
    ctx_row = lambda i: batch

    late = lambda l: tuple((w, l) for w in (w_attn_out, w_conv_out, w_o, w_ffn_in, w_ffn_out))
    wts = {0: (w_in[0].astype(BF16),)}

    xl = x.reshape(batch * seq, d)
    xc = ctx.reshape(m_ctx, d)
    for l in range(depth):
        last = l == depth - 1
        w_in_bf = wts[l][0]
        q_c, kvx_c, h_c = _inproj(xc, n1, mods, w_in_bf, l, tm=m_ctx, row_of_tile=ctx_row, rope_tabs=None,
                                  seq=n_ctx, with_q=not last)
        q, kvx, h = _inproj(xl, n1, mods, w_in_bf, l, tm=tm, row_of_tile=lat_row(tm),
                            rope_tabs=rope_tabs, seq=seq, with_q=True)
        y, cast0 = _conv_branch(h, w_in_bf, conv_w, conv_b3, l, tm=tm, tn=tn, seq=seq,
                                cast=late(0) if l == 0 else ())
        if l == 0:
            wts[0] = wts[0] + cast0
        attn, cast1 = _attention(q, kvx, kvx_c, sink[l], batch=batch, seq=seq, n_ctx=n_ctx, tq=tq,
                                 cast=() if last else ((w_in, l + 1),) + late(l + 1))
        if not last:
            wts[l + 1] = cast1
        _, wa_bf, wc_bf, wo_bf, wfi_bf, wfo_bf = wts[l]
        mm = _merge(attn, y, h, w_in_bf, wa_bf, wc_bf, l, tm=tm, tn=tn)
        xl = _outproj(mm, xl, mods, wo_bf, l, tm=tm_out, tn=d, row_of_tile=lat_row(tm_out))

        if not last:
            y_c, _ = _conv_branch(h_c, w_in_bf, conv_w, conv_b3, l, tm=n_ctx, tn=tn, seq=n_ctx)
            attn_c = _ctx_attention(q_c, kvx_c, sink[l], batch=batch, n_ctx=n_ctx)
            mm_c = _merge(attn_c, y_c, h_c, w_in_bf, wa_bf, wc_bf, l, tm=m_ctx, tn=tn)
            xc = _outproj(mm_c, xc, mods, wo_bf, l, tm=m_ctx, tn=d, row_of_tile=ctx_row)
            xc = _ffn(xc, n2, mods, wfi_bf, wfo_bf, l, tm=m_ctx, tf=tf, row_of_tile=ctx_row, final_g=None)

        xl = _ffn(xl, n2, mods, wfi_bf, wfo_bf, l, tm=tm, tf=tf, row_of_tile=lat_row(tm),
                  final_g=fg if last else None)
    return xl.reshape(batch, seq, d)
```

```python
import functools

import jax
import jax.numpy as jnp
import numpy as np
from jax import lax
from jax.experimental import pallas as pl
from jax.experimental.pallas import tpu as pltpu

GRID_W = 64
N_HEADS = 32
N_KV_HEADS = 4
HEAD_DIM = 64
WINDOW = 128
ROPE_THETA = 10000.0
EPS = 1e-6
NEG = -1e30
LOG2E = 1.4426950408889634
Q_SCALE = HEAD_DIM ** -0.5 * LOG2E

LANES = 128
BF16_ROWS = 16
ROW_CHUNK = 256
V7X_VMEM_BYTES = 64 * 1024 * 1024
VMEM_LIMIT_BYTES = V7X_VMEM_BYTES * 15 // 16
ATT_BLOCK = 128
MOD_ROWS = 8
FFN_OUT_CHUNK = 512
FFN_ROW_CHUNK = 512

F32 = jnp.float32
BF16 = jnp.bfloat16


def _params(*sem):
    return pltpu.CompilerParams(dimension_semantics=sem, vmem_limit_bytes=VMEM_LIMIT_BYTES)


def _sigmoid(v):
    return 1.0 / (1.0 + jnp.exp(-v))


def _norm_mod(x, gain, shift):
    y = x * lax.rsqrt(jnp.mean(x * x, axis=-1, keepdims=True) + EPS)
    return y * gain + shift


def _dot(a, b):
    return jnp.dot(a, b, preferred_element_type=F32)


def _dot_nt(a, b):
    return lax.dot_general(a, b, (((1,), (1,)), ((), ())), preferred_element_type=F32)


def _ada_kernel(c_ref, w_ref, b_ref, o_ref):
    c = c_ref[...]
    s = c * _sigmoid(c)
    o_ref[...] = _dot(s.astype(BF16), w_ref[...].astype(BF16)) + b_ref[...]


def _ada(cc, ada_w, ada_b):
    depth, d, n = ada_w.shape
    tn = _tile(n, 1024)
    return pl.pallas_call(
        _ada_kernel,
        grid=(depth, n // tn),
        in_specs=[
            pl.BlockSpec((MOD_ROWS, d), lambda l, j: (0, 0)),
            pl.BlockSpec((None, d, tn), lambda l, j: (l, 0, j)),
            pl.BlockSpec((None, 1, tn), lambda l, j: (l, 0, j)),
        ],
        out_specs=pl.BlockSpec((None, MOD_ROWS, tn), lambda l, j: (l, 0, j)),
        out_shape=jax.ShapeDtypeStruct((depth, MOD_ROWS, n), F32),
        compiler_params=_params("parallel", "parallel"),
        name="ada_mod",
    )(cc, ada_w, ada_b.reshape(depth, 1, n))


def _mod_spec(layer, which, row_of_tile):
    def index(i, j):
        return (layer, row_of_tile(i), which, 0, 0)
    return index


def _rope(v, cos, sin_a, sin_b):
    return v * cos + pltpu.roll(v, LANES - 16, 1) * sin_a + pltpu.roll(v, 16, 1) * sin_b


def _inproj_kernel(*refs, use_rope, with_q):
    refs = list(refs)
    x_ref, g_ref, sh_ref, sc_ref, wkv_ref = refs[:5]
    del refs[:5]
    wq_ref = refs.pop(0) if with_q else None
    if use_rope:
        cos_ref, sa_ref, sb_ref = refs[:3]
        del refs[:3]
    q_ref = refs.pop(0) if with_q else None
    kvx_ref, h_ref = refs

    rows = h_ref.shape[0]
    rc = _tile(rows, ROW_CHUNK)
    gain, sh = g_ref[...] * (1.0 + sc_ref[...]), sh_ref[...]
    lo_mask = lax.broadcasted_iota(jnp.int32, (rc, LANES), 1) < HEAD_DIM
    for r0 in range(0, rows, rc):
        rs = slice(r0, r0 + rc)
        rope = (lambda v: _rope(v, cos_ref[rs], sa_ref[rs], sb_ref[rs])) if use_rope else (lambda v: v)
        h = _norm_mod(x_ref[rs], gain, sh).astype(BF16)
        h_ref[rs] = h
        acc = _dot(h, wkv_ref[...])
        kvw = acc.shape[1] // 2
        for part in range(2):
            for pr in range(kvw // LANES):
                v = acc[:, part * kvw + pr * LANES: part * kvw + (pr + 1) * LANES]
                if part == 0:
                    v = rope(v)
                r = pltpu.roll(v, HEAD_DIM, 1)
                tiles = ((jnp.where(lo_mask, v, 0.0), jnp.where(lo_mask, 0.0, r)),
                         (jnp.where(lo_mask, r, 0.0), jnp.where(lo_mask, 0.0, v)))
                for odd in range(2):
                    head = 2 * pr + odd
                    for hi in range(2):
                        col = head * 4 * LANES + (2 * part + hi) * LANES
                        kvx_ref[rs, col:col + LANES] = tiles[odd][hi].astype(BF16)
        if with_q:
            acc = _dot(h, wq_ref[...])
            for cg in range(acc.shape[1] // LANES):
                v = rope(acc[:, cg * LANES:(cg + 1) * LANES])
                q_ref[rs, cg * LANES:(cg + 1) * LANES] = (v * Q_SCALE).astype(BF16)


def _inproj(xr, norm_g, mods, w_bf, layer, *, tm, row_of_tile, rope_tabs, seq, with_q):
    m, d = xr.shape
    q_w = N_HEADS * HEAD_DIM
    kv_w = N_KV_HEADS * HEAD_DIM
    use_rope = rope_tabs is not None
    kvx_w = N_KV_HEADS * 4 * LANES

    in_specs = [
        pl.BlockSpec((tm, d), lambda i: (i, 0)),
        pl.BlockSpec((None, 1, d), lambda i: (layer, 0, 0)),
        pl.BlockSpec((None, None, None, 1, d), lambda i: (layer, row_of_tile(i), 0, 0, 0)),
        pl.BlockSpec((None, None, None, 1, d), lambda i: (layer, row_of_tile(i), 1, 0, 0)),
        pl.BlockSpec((d, 2 * kv_w), lambda i: (0, q_w // (2 * kv_w))),
    ]
    args = [xr, norm_g, mods, mods, w_bf]
    if with_q:
        in_specs.append(pl.BlockSpec((d, q_w), lambda i: (0, 0)))
        args.append(w_bf)
    if use_rope:
        tps = seq // tm
        for t in rope_tabs:
            in_specs.append(pl.BlockSpec((tm, LANES), lambda i: (i % tps, 0)))
            args.append(t)

    out_specs = [pl.BlockSpec((tm, kvx_w), lambda i: (i, 0)),
                 pl.BlockSpec((tm, d), lambda i: (i, 0))]
    out_shape = [jax.ShapeDtypeStruct((m, kvx_w), BF16), jax.ShapeDtypeStruct((m, d), BF16)]
    if with_q:
        out_specs.insert(0, pl.BlockSpec((tm, q_w), lambda i: (i, 0)))
        out_shape.insert(0, jax.ShapeDtypeStruct((m, q_w), BF16))
    out = pl.pallas_call(
        functools.partial(_inproj_kernel, use_rope=use_rope, with_q=with_q),
        grid=(m // tm,),
        in_specs=in_specs,
        out_specs=out_specs,
        out_shape=out_shape,
        compiler_params=_params("parallel"),
        name="inproj" if with_q else "inproj_kv",
    )(*args)
    return out if with_q else (None,) + tuple(out)


def _cast_plan(weights, n_steps, step_of):
    in_specs, out_specs, out_shape = [], [], []
    for w, layer in weights:
        _, rows, cols = w.shape
        nblk = n_steps
        while rows % (nblk * BF16_ROWS):
            assert nblk % 2 == 0, (rows, n_steps)
            nblk //= 2
        per = n_steps // nblk

        def src_index(*g, per=per, layer=layer):
            return (layer, step_of(*g) // per, 0)

        def dst_index(*g, per=per):
            return (step_of(*g) // per, 0)
        in_specs.append(pl.BlockSpec((None, rows // nblk, cols), src_index))
        out_specs.append(pl.BlockSpec((rows // nblk, cols), dst_index))
        out_shape.append(jax.ShapeDtypeStruct((rows, cols), BF16))
    return in_specs, out_specs, out_shape


def _run_casts(src_refs, dst_refs):
    for src, dst in zip(src_refs, dst_refs):
        dst[...] = src[...].astype(BF16)


def _conv_kernel(*refs, tm, seq, ncast):
    hp_ref, h_ref, hn_ref, wb_ref, wc_ref, wx_ref, cw_ref, cb_ref = refs[:8]
    y_ref, h_scr = refs[8 + ncast], refs[-1]
    _run_casts(refs[8:8 + ncast], refs[9 + ncast:-1])
    i = pl.program_id(0)
    j = pl.program_id(1)
    halo = BF16_ROWS

    @pl.when(j == 0)
    def _():
        h_scr[0:halo] = hp_ref[...]
        h_scr[halo:halo + tm] = h_ref[...]
        h_scr[halo + tm:] = hn_ref[...]

    hx = h_scr[...]
    u = _dot(hx, wc_ref[...]) * _dot(hx, wx_ref[...])
    rows = tm + 2 * halo
    u_up = pltpu.roll(u, 1, 0)[halo:halo + tm]
    u_dn = pltpu.roll(u, rows - 1, 0)[halo:halo + tm]
    row = lax.broadcasted_iota(jnp.int32, (tm, 1), 0)
    seq_start = (i * tm) % seq == 0
    seq_end = ((i + 1) * tm) % seq == 0
    u_up = jnp.where(jnp.logical_and(row == 0, seq_start), 0.0, u_up)
    u_dn = jnp.where(jnp.logical_and(row == tm - 1, seq_end), 0.0, u_dn)
    w = cw_ref[...]
    conv = u_up * w[0:1] + u[halo:halo + tm] * w[1:2] + u_dn * w[2:3] + cb_ref[...]
    y_ref[...] = (_dot(h_ref[...], wb_ref[...]) * conv).astype(BF16)


def _conv_branch(h, w_bf, conv_w, conv_b, layer, *, tm, tn, seq, cast=()):
    m, d = h.shape
    d_conv = conv_w.shape[-1]
    halo = BF16_ROWS
    off = (N_HEADS * HEAD_DIM + 2 * N_KV_HEADS * HEAD_DIM) // tn
    nc = d_conv // tn
    hb = tm // halo
    last_hb = m // halo - 1

    def wspec(k):
        return pl.BlockSpec((d, tn), lambda i, j: (0, off + k * nc + j))

    c_in, c_out, c_shape = _cast_plan(cast, (m // tm) * nc, lambda i, j: i * nc + j)
    out = pl.pallas_call(
        functools.partial(_conv_kernel, tm=tm, seq=seq, ncast=len(cast)),
        grid=(m // tm, nc),
        in_specs=[
            pl.BlockSpec((halo, d), lambda i, j: (jnp.maximum(i * hb - 1, 0), 0)),
            pl.BlockSpec((tm, d), lambda i, j: (i, 0)),
            pl.BlockSpec((halo, d), lambda i, j: (jnp.minimum((i + 1) * hb, last_hb), 0)),
            wspec(0), wspec(1), wspec(2),
            pl.BlockSpec((None, 3, tn), lambda i, j: (layer, 0, j)),
            pl.BlockSpec((None, 1, tn), lambda i, j: (layer, 0, j)),
        ] + c_in,
        out_specs=[pl.BlockSpec((tm, tn), lambda i, j: (i, j))] + c_out,
        out_shape=[jax.ShapeDtypeStruct((m, d_conv), BF16)] + c_shape,
        scratch_shapes=[pltpu.VMEM((tm + 2 * halo, d), BF16)],
        compiler_params=_params("arbitrary", "arbitrary"),
        name="conv_branch",
    )(h, h, h, w_bf, w_bf, w_bf, conv_w, conv_b, *(w for w, _ in cast))
    return out[0], tuple(out[1:])


def _group_attention(q_ref, rows, pieces, sinks, ones_sel, o_ref):
    blk = ATT_BLOCK
    npair = len(sinks) // 2
    qs = jnp.concatenate([q_ref[rows, jp * LANES:(jp + 1) * LANES] for jp in range(npair)], axis=0)
    probs, vals, sink_p = [], [], []
    for par in range(2):
        chunks = []
        for kvx, biases in pieces:
            s = _dot_nt(qs, kvx[:, par * LANES:(par + 1) * LANES])
            for ci in range(s.shape[1] // LANES):
                ch = s[:, ci * LANES:(ci + 1) * LANES]
                if biases is not None and biases[ci] is not None:
                    ch = ch + jnp.concatenate([biases[ci]] * npair, axis=0)
                chunks.append(ch)
            vals.append(kvx[:, (2 + par) * LANES:(3 + par) * LANES])
        sink = jnp.concatenate([jnp.full((blk, LANES), sinks[2 * jp + par], F32) for jp in range(npair)], axis=0)
        mx = jnp.maximum(jnp.max(functools.reduce(jnp.maximum, chunks), axis=-1, keepdims=True), sink)
        probs.extend(jnp.exp2(ch - mx).astype(BF16) for ch in chunks)
        sink_p.append(jnp.exp2(sink - mx))
    rhs = jnp.concatenate([jnp.concatenate(vals, axis=0), ones_sel], axis=1)
    out = _dot(jnp.concatenate(probs, axis=1), rhs)
    lane = lax.broadcasted_iota(jnp.int32, (npair * blk, LANES), 1)
    den = out[:, LANES:] + jnp.where(lane < HEAD_DIM, sink_p[0], sink_p[1])
    res = out[:, :LANES] / den
    for jp in range(npair):
        o_ref[rows, jp * LANES:(jp + 1) * LANES] = res[jp * blk:(jp + 1) * blk].astype(BF16)


def _ones_selector(n):
    row = lax.broadcasted_iota(jnp.int32, (2 * n, LANES), 0)
    lane = lax.broadcasted_iota(jnp.int32, (2 * n, LANES), 1)
    return jnp.where((row < n) == (lane < HEAD_DIM), 1.0, 0.0).astype(BF16)


def _attn_kernel(*refs, tq, nblk_seq, ncast):
    sink_ref, q_ref, kvp_ref, kvc_ref, kvn_ref, ctx_ref = refs[:6]
    o_ref, band_scr = refs[6 + ncast], refs[-1]
    _run_casts(refs[6:6 + ncast], refs[7 + ncast:-1])
    i = pl.program_id(1)
    h = pl.program_id(2)
    blk = ATT_BLOCK
    groups = N_HEADS // N_KV_HEADS
    band_scr[0:blk] = kvp_ref[...]
    band_scr[blk:blk + tq] = kvc_ref[...]
    band_scr[blk + tq:] = kvn_ref[...]
    ctx = ctx_ref[...]
    sinks = [sink_ref[h * groups + g] * LOG2E for g in range(groups)]
    r = lax.broadcasted_iota(jnp.int32, (blk, blk), 0)
    c = lax.broadcasted_iota(jnp.int32, (blk, blk), 1)
    bias_prev = jnp.where(c >= r, 0.0, NEG)
    bias_next = jnp.where(c <= r, 0.0, NEG)
    ones_sel = _ones_selector(ctx.shape[0] + 3 * blk)

    for sb in range(tq // blk):
        r0 = sb * blk
        gblk = i * (tq // blk) + sb
        pen_first = jnp.where(gblk == 0, NEG, 0.0)
        pen_last = jnp.where(gblk == nblk_seq - 1, NEG, 0.0)
        biases = (bias_prev + pen_first, None, bias_next + pen_last)
        band = band_scr[r0:r0 + 3 * blk, :]
        _group_attention(q_ref, pl.ds(r0, blk), ((ctx, None), (band, biases)), sinks, ones_sel, o_ref)


def _attention(q, kvx, kvx_ctx, sink, *, batch, seq, n_ctx, tq, cast=()):
    m, q_w = q.shape
    blk = ATT_BLOCK
    hw = q_w // N_KV_HEADS
    kw = 4 * LANES
    nb = seq // blk
    tpb = tq // blk

    nt = seq // tq
    c_in, c_out, c_shape = _cast_plan(cast, batch * nt * N_KV_HEADS,
                                      lambda b, i, h: (b * nt + i) * N_KV_HEADS + h)
    out = pl.pallas_call(
        functools.partial(_attn_kernel, tq=tq, nblk_seq=nb, ncast=len(cast)),
        grid=(batch, nt, N_KV_HEADS),
        in_specs=[
            pl.BlockSpec(memory_space=pltpu.SMEM),
            pl.BlockSpec((tq, hw), lambda b, i, h: (b * nt + i, h)),
            pl.BlockSpec((blk, kw), lambda b, i, h: (b * nb + jnp.maximum(i * tpb - 1, 0), h)),
            pl.BlockSpec((tq, kw), lambda b, i, h: (b * nt + i, h)),
            pl.BlockSpec((blk, kw), lambda b, i, h: (b * nb + jnp.minimum((i + 1) * tpb, nb - 1), h)),
            pl.BlockSpec((n_ctx, kw), lambda b, i, h: (b, h)),
        ] + c_in,
        out_specs=[pl.BlockSpec((tq, hw), lambda b, i, h: (b * nt + i, h))] + c_out,
        out_shape=[jax.ShapeDtypeStruct((m, q_w), BF16)] + c_shape,
        scratch_shapes=[pltpu.VMEM((tq + 2 * blk, kw), BF16)],
        compiler_params=_params("arbitrary", "arbitrary", "arbitrary"),
        name="band_attention",
    )(sink, q, kvx, kvx, kvx, kvx_ctx, *(w for w, _ in cast))
    return out[0], tuple(out[1:])


def _ctx_attn_kernel(sink_ref, q_ref, ctx_ref, o_ref, *, n_ctx):
    h = pl.program_id(1)
    groups = N_HEADS // N_KV_HEADS
    ctx = ctx_ref[...]
    sinks = [sink_ref[h * groups + g] * LOG2E for g in range(groups)]
    ones_sel = _ones_selector(n_ctx)
    for r0 in range(0, n_ctx, ATT_BLOCK):
        _group_attention(q_ref, pl.ds(r0, ATT_BLOCK), ((ctx, None),), sinks, ones_sel, o_ref)


def _ctx_attention(q, kvx_ctx, sink, *, batch, n_ctx):
    m, q_w = q.shape
    hw = q_w // N_KV_HEADS
    kw = 4 * LANES
    return pl.pallas_call(
        functools.partial(_ctx_attn_kernel, n_ctx=n_ctx),
        grid=(batch, N_KV_HEADS),
        in_specs=[
            pl.BlockSpec(memory_space=pltpu.SMEM),
            pl.BlockSpec((n_ctx, hw), lambda b, h: (b, h)),
            pl.BlockSpec((n_ctx, kw), lambda b, h: (b, h)),
        ],
        out_specs=pl.BlockSpec((n_ctx, hw), lambda b, h: (b, h)),
        out_shape=jax.ShapeDtypeStruct((m, q_w), BF16),
        compiler_params=_params("parallel", "arbitrary"),
        name="ctx_attention",
    )(sink, q, kvx_ctx)


def _merge_kernel(a_ref, y_ref, h_ref, wa_ref, wc_ref, wga_ref, wgc_ref, m_ref):
    h = h_ref[...]
    m = (_sigmoid(_dot(h, wga_ref[...])) * _dot(a_ref[...], wa_ref[...])
         + _sigmoid(_dot(h, wgc_ref[...])) * _dot(y_ref[...], wc_ref[...]))
    m_ref[...] = m.astype(BF16)


def _merge(attn, y, h, w_in_bf, wa_bf, wc_bf, layer, *, tm, tn):
    m, q_w = attn.shape
    d_conv = y.shape[1]
    d_in = h.shape[1]
    d = wa_bf.shape[-1]
    nn = d // tn
    gate_off = (q_w + 2 * N_KV_HEADS * HEAD_DIM + 3 * d_conv) // tn
    return pl.pallas_call(
        _merge_kernel,
        grid=(m // tm, nn),
        in_specs=[
            pl.BlockSpec((tm, q_w), lambda i, j: (i, 0)),
            pl.BlockSpec((tm, d_conv), lambda i, j: (i, 0)),
            pl.BlockSpec((tm, d_in), lambda i, j: (i, 0)),
            pl.BlockSpec((q_w, tn), lambda i, j: (0, j)),
            pl.BlockSpec((d_conv, tn), lambda i, j: (0, j)),
            pl.BlockSpec((d_in, tn), lambda i, j: (0, gate_off + j)),
            pl.BlockSpec((d_in, tn), lambda i, j: (0, gate_off + nn + j)),
        ],
        out_specs=pl.BlockSpec((tm, tn), lambda i, j: (i, j)),
        out_shape=jax.ShapeDtypeStruct((m, d), BF16),
        compiler_params=_params("parallel", "arbitrary"),
        name="merge",
    )(attn, y, h, wa_bf, wc_bf, w_in_bf, w_in_bf)


def _outproj_kernel(m_ref, w_ref, x_ref, g_ref, o_ref):
    o_ref[...] = x_ref[...] + g_ref[...] * _dot(m_ref[...], w_ref[...])


def _outproj(mm, xr, mods, wo_bf, layer, *, tm, tn, row_of_tile):
    m, d = xr.shape
    return pl.pallas_call(
        _outproj_kernel,
        grid=(m // tm, d // tn),
        in_specs=[
            pl.BlockSpec((tm, d), lambda i, j: (i, 0)),
            pl.BlockSpec((d, tn), lambda i, j: (0, j)),
            pl.BlockSpec((tm, tn), lambda i, j: (i, j)),
            pl.BlockSpec((None, None, None, 1, tn), lambda i, j: (layer, row_of_tile(i), 2, 0, j)),
        ],
        out_specs=pl.BlockSpec((tm, tn), lambda i, j: (i, j)),
        out_shape=jax.ShapeDtypeStruct((m, d), F32),
        compiler_params=_params("parallel", "arbitrary"),
        name="outproj",
    )(mm, wo_bf, xr, mods)


def _ffn_kernel(*refs, final, nf):
    if final:
        x_ref, g_ref, sh_ref, sc_ref, gt_ref, wg_ref, wu_ref, wo_ref, fg_ref, o_ref, h_scr = refs
    else:
        x_ref, g_ref, sh_ref, sc_ref, gt_ref, wg_ref, wu_ref, wo_ref, o_ref, h_scr = refs
    f = pl.program_id(1)
    rows, d = o_ref.shape
    cw = _tile(d, FFN_OUT_CHUNK)

    def step(first, last):
        rc = _tile(rows, FFN_ROW_CHUNK) if first or last else rows
        for r0 in range(0, rows, rc):
            rs = slice(r0, r0 + rc)
            if first:
                h = _norm_mod(x_ref[rs], g_ref[...] * (1.0 + sc_ref[...]), sh_ref[...]).astype(BF16)
                h_scr[rs] = h
            else:
                h = h_scr[rs]
            gate = _dot(h, wg_ref[...])
            act = (gate * _sigmoid(gate) * _dot(h, wu_ref[...])).astype(BF16)
            for c0 in range(0, d, cw):
                part = _dot(act, wo_ref[:, c0:c0 + cw])
                if first:
                    o_ref[rs, c0:c0 + cw] = part
                else:
                    o_ref[rs, c0:c0 + cw] += part
            if last:
                xn = x_ref[rs] + gt_ref[...] * o_ref[rs]
                if final:
                    xn = xn * lax.rsqrt(jnp.mean(xn * xn, axis=-1, keepdims=True) + EPS) * fg_ref[...]
                o_ref[rs] = xn

    if nf == 1:
        step(True, True)
    else:
        pl.when(f == 0)(lambda: step(True, False))
        pl.when(jnp.logical_and(f > 0, f < nf - 1))(lambda: step(False, False))
        pl.when(f == nf - 1)(lambda: step(False, True))


def _ffn(xr, norm_g, mods, wi_bf, wo_bf, layer, *, tm, tf, row_of_tile, final_g):
    m, d = xr.shape
    d_ff = wo_bf.shape[0]
    nf = d_ff // tf
    final = final_g is not None
    in_specs = [
        pl.BlockSpec((tm, d), lambda i, f: (i, 0)),
        pl.BlockSpec((None, 1, d), lambda i, f: (layer, 0, 0)),
        pl.BlockSpec((None, None, None, 1, d), _mod_spec(layer, 3, row_of_tile)),
        pl.BlockSpec((None, None, None, 1, d), _mod_spec(layer, 4, row_of_tile)),
        pl.BlockSpec((None, None, None, 1, d), _mod_spec(layer, 5, row_of_tile)),
        pl.BlockSpec((d, tf), lambda i, f: (0, f)),
        pl.BlockSpec((d, tf), lambda i, f: (0, nf + f)),
        pl.BlockSpec((tf, d), lambda i, f: (f, 0)),
    ]
    args = [xr, norm_g, mods, mods, mods, wi_bf, wi_bf, wo_bf]
    if final:
        in_specs.append(pl.BlockSpec((1, d), lambda i, f: (0, 0)))
        args.append(final_g)
    return pl.pallas_call(
        functools.partial(_ffn_kernel, final=final, nf=nf),
        grid=(m // tm, nf),
        in_specs=in_specs,
        out_specs=pl.BlockSpec((tm, d), lambda i, f: (i, 0)),
        out_shape=jax.ShapeDtypeStruct((m, d), F32),
        scratch_shapes=[pltpu.VMEM((tm, d), BF16)],
        compiler_params=_params("parallel", "arbitrary"),
        name="ffn_final" if final else "ffn",
    )(*args)


def _rope_tables(seq):
    quarter = HEAD_DIM // 4
    t = np.arange(seq)
    pos = np.stack([t // GRID_W, t % GRID_W], axis=1).astype(np.float32)
    d = np.arange(LANES) % HEAD_DIM
    axis = d // (HEAD_DIM // 2)
    first_half = (d // quarter) % 2 == 0
    freqs = np.float32(ROPE_THETA) ** (-np.arange(quarter, dtype=np.float32) / np.float32(quarter))
    ang = (pos[:, axis] * freqs[d % quarter][None, :]).astype(np.float32)
    cos, sin = np.cos(ang), np.sin(ang)
    fh = first_half[None, :]
    zero = np.float32(0.0)
    return tuple(jnp.asarray(t, F32) for t in (cos, np.where(fh, -sin, zero), np.where(fh, zero, sin)))


def _tile(n, pref):
    return pref if n % pref == 0 else n


def kernel(x, c, ctx, c_ctx, ada_w, ada_b, norm1_g, norm2_g, w_in, conv_w, conv_b, sink,
           w_attn_out, w_conv_out, w_o, w_ffn_in, w_ffn_out, final_g):
    batch, seq, d = x.shape
    n_ctx = ctx.shape[1]
    depth = ada_w.shape[0]
    d_ff = w_ffn_out.shape[1]
    assert batch + 1 <= MOD_ROWS and seq % ATT_BLOCK == 0 and n_ctx % ATT_BLOCK == 0

    cc = jnp.zeros((MOD_ROWS, d), F32).at[:batch].set(c).at[batch].set(c_ctx)
    mods = _ada(cc, ada_w, ada_b).reshape(depth, MOD_ROWS, 6, 1, d)

    n1 = norm1_g.reshape(depth, 1, d)
    n2 = norm2_g.reshape(depth, 1, d)
    conv_b3 = conv_b.reshape(depth, 1, -1)
    fg = final_g.reshape(1, d)
    rope_tabs = _rope_tables(seq)

    tm = _tile(seq, 1024)
    tm_out = _tile(seq, 512)
    tn = 2 * N_KV_HEADS * HEAD_DIM
    tf = _tile(d_ff, tn)
    tq = _tile(seq, 2048)
    m_ctx = batch * n_ctx
    lat_row = lambda t: (lambda i: i // (seq // t))
    ctx_row = lambda i: batch

    late = lambda l: tuple((w, l) for w in (w_attn_out, w_conv_out, w_o, w_ffn_in, w_ffn_out))
    wts = {0: (w_in[0].astype(BF16),)}

    xl = x.reshape(batch * seq, d)
    xc = ctx.reshape(m_ctx, d)
    for l in range(depth):
        last = l == depth - 1
        w_in_bf = wts[l][0]
        q_c, kvx_c, h_c = _inproj(xc, n1, mods, w_in_bf, l, tm=m_ctx, row_of_tile=ctx_row, rope_tabs=None,
                                  seq=n_ctx, with_q=not last)
        q, kvx, h = _inproj(xl, n1, mods, w_in_bf, l, tm=tm, row_of_tile=lat_row(tm),
                            rope_tabs=rope_tabs, seq=seq, with_q=True)
        y, cast0 = _conv_branch(h, w_in_bf, conv_w, conv_b3, l, tm=tm, tn=tn, seq=seq,
                                cast=late(0) if l == 0 else ())
        if l == 0:
            wts[0] = wts[0] + cast0
        attn, cast1 = _attention(q, kvx, kvx_c, sink[l], batch=batch, seq=seq, n_ctx=n_ctx, tq=tq,
                                 cast=() if last else ((w_in, l + 1),) + late(l + 1))
        if not last:
            wts[l + 1] = cast1
        _, wa_bf, wc_bf, wo_bf, wfi_bf, wfo_bf = wts[l]
        mm = _merge(attn, y, h, w_in_bf, wa_bf, wc_bf, l, tm=tm, tn=tn)
        xl = _outproj(mm, xl, mods, wo_bf, l, tm=tm_out, tn=d, row_of_tile=lat_row(tm_out))

        if not last:
            y_c, _ = _conv_branch(h_c, w_in_bf, conv_w, conv_b3, l, tm=n_ctx, tn=tn, seq=n_ctx)
            attn_c = _ctx_attention(q_c, kvx_c, sink[l], batch=batch, n_ctx=n_ctx)
            mm_c = _merge(attn_c, y_c, h_c, w_in_bf, wa_bf, wc_bf, l, tm=m_ctx, tn=tn)
            xc = _outproj(mm_c, xc, mods, wo_bf, l, tm=m_ctx, tn=d, row_of_tile=ctx_row)
            xc = _ffn(xc, n2, mods, wfi_bf, wfo_bf, l, tm=m_ctx, tf=tf, row_of_tile=ctx_row, final_g=None)

        xl = _ffn(xl, n2, mods, wfi_bf, wfo_bf, l, tm=tm, tf=tf, row_of_tile=lat_row(tm),
                  final_g=fg if last else None)
    return xl.reshape(batch, seq, d)
```

```python
import functools

import jax
import jax.numpy as jnp
import numpy as np
from jax import lax
from jax.experimental import pallas as pl
from jax.experimental.pallas import tpu as pltpu

GRID_W = 64
N_HEADS = 32
N_KV_HEADS = 4
HEAD_DIM = 64
WINDOW = 128
ROPE_THETA = 10000.0
EPS = 1e-6
NEG = -1e30
LOG2E = 1.4426950408889634
Q_SCALE = HEAD_DIM ** -0.5 * LOG2E

LANES = 128
BF16_ROWS = 16
ROW_CHUNK = 256
V7X_VMEM_BYTES = 64 * 1024 * 1024
VMEM_LIMIT_BYTES = V7X_VMEM_BYTES * 15 // 16
ATT_BLOCK = 128
MOD_ROWS = 8
FFN_OUT_CHUNK = 512
OUT_ROW_CHUNK = 512
FFN_ROW_CHUNK = 512

F32 = jnp.float32
BF16 = jnp.bfloat16


def _params(*sem):
    return pltpu.CompilerParams(dimension_semantics=sem, vmem_limit_bytes=VMEM_LIMIT_BYTES)


def _sigmoid(v):
    return 1.0 / (1.0 + jnp.exp(-v))


def _norm_mod(x, gain, shift):
    y = x * lax.rsqrt(jnp.mean(x * x, axis=-1, keepdims=True) + EPS)
    return y * gain + shift


def _dot(a, b):
    return jnp.dot(a, b, preferred_element_type=F32)


def _dot_nt(a, b):
    return lax.dot_general(a, b, (((1,), (1,)), ((), ())), preferred_element_type=F32)


def _ada_kernel(c_ref, w_ref, b_ref, o_ref):
    c = c_ref[...]
    s = c * _sigmoid(c)
    o_ref[...] = _dot(s.astype(BF16), w_ref[...].astype(BF16)) + b_ref[...]


def _ada(cc, ada_w, ada_b):
    depth, d, n = ada_w.shape
    tn = _tile(n, 1024)
    return pl.pallas_call(
        _ada_kernel,
        grid=(depth, n // tn),
        in_specs=[
            pl.BlockSpec((MOD_ROWS, d), lambda l, j: (0, 0)),
            pl.BlockSpec((None, d, tn), lambda l, j: (l, 0, j)),
            pl.BlockSpec((None, 1, tn), lambda l, j: (l, 0, j)),
        ],
        out_specs=pl.BlockSpec((None, MOD_ROWS, tn), lambda l, j: (l, 0, j)),
        out_shape=jax.ShapeDtypeStruct((depth, MOD_ROWS, n), F32),
        compiler_params=_params("parallel", "parallel"),
        name="ada_mod",
    )(cc, ada_w, ada_b.reshape(depth, 1, n))


def _mod_spec(layer, which, row_of_tile):
    def index(i, j):
        return (layer, row_of_tile(i), which, 0, 0)
    return index


def _rope(v, cos, sin_a, sin_b):
    return v * cos + pltpu.roll(v, LANES - 16, 1) * sin_a + pltpu.roll(v, 16, 1) * sin_b


def _inproj_kernel(*refs, use_rope, with_q):
    refs = list(refs)
    x_ref, g_ref, sh_ref, sc_ref, wkv_ref = refs[:5]
    del refs[:5]
    wq_ref = refs.pop(0) if with_q else None
    if use_rope:
        cos_ref, sa_ref, sb_ref = refs[:3]
        del refs[:3]
    q_ref = refs.pop(0) if with_q else None
    kvx_ref, h_ref = refs

    rows = h_ref.shape[0]
    rc = _tile(rows, ROW_CHUNK)
    gain, sh = g_ref[...] * (1.0 + sc_ref[...]), sh_ref[...]
    lo_mask = lax.broadcasted_iota(jnp.int32, (rc, LANES), 1) < HEAD_DIM
    for r0 in range(0, rows, rc):
        rs = slice(r0, r0 + rc)
        rope = (lambda v: _rope(v, cos_ref[rs], sa_ref[rs], sb_ref[rs])) if use_rope else (lambda v: v)
        h = _norm_mod(x_ref[rs], gain, sh).astype(BF16)
        h_ref[rs] = h
        acc = _dot(h, wkv_ref[...])
        kvw = acc.shape[1] // 2
        for part in range(2):
            for pr in range(kvw // LANES):
                v = acc[:, part * kvw + pr * LANES: part * kvw + (pr + 1) * LANES]
                if part == 0:
                    v = rope(v)
                r = pltpu.roll(v, HEAD_DIM, 1)
                tiles = ((jnp.where(lo_mask, v, 0.0), jnp.where(lo_mask, 0.0, r)),
                         (jnp.where(lo_mask, r, 0.0), jnp.where(lo_mask, 0.0, v)))
                for odd in range(2):
                    head = 2 * pr + odd
                    for hi in range(2):
                        col = head * 4 * LANES + (2 * part + hi) * LANES
                        kvx_ref[rs, col:col + LANES] = tiles[odd][hi].astype(BF16)
        if with_q:
            acc = _dot(h, wq_ref[...])
            for cg in range(acc.shape[1] // LANES):
                v = rope(acc[:, cg * LANES:(cg + 1) * LANES])
                q_ref[rs, cg * LANES:(cg + 1) * LANES] = (v * Q_SCALE).astype(BF16)


def _inproj(xr, norm_g, mods, w_bf, layer, *, tm, row_of_tile, rope_tabs, seq, with_q):
    m, d = xr.shape
    q_w = N_HEADS * HEAD_DIM
    kv_w = N_KV_HEADS * HEAD_DIM
    use_rope = rope_tabs is not None
    kvx_w = N_KV_HEADS * 4 * LANES

    in_specs = [
        pl.BlockSpec((tm, d), lambda i: (i, 0)),
        pl.BlockSpec((None, 1, d), lambda i: (layer, 0, 0)),
        pl.BlockSpec((None, None, None, 1, d), lambda i: (layer, row_of_tile(i), 0, 0, 0)),
        pl.BlockSpec((None, None, None, 1, d), lambda i: (layer, row_of_tile(i), 1, 0, 0)),
        pl.BlockSpec((d, 2 * kv_w), lambda i: (0, q_w // (2 * kv_w))),
    ]
    args = [xr, norm_g, mods, mods, w_bf]
    if with_q:
        in_specs.append(pl.BlockSpec((d, q_w), lambda i: (0, 0)))
        args.append(w_bf)
    if use_rope:
        tps = seq // tm
        for t in rope_tabs:
            in_specs.append(pl.BlockSpec((tm, LANES), lambda i: (i % tps, 0)))
            args.append(t)

    out_specs = [pl.BlockSpec((tm, kvx_w), lambda i: (i, 0)),
                 pl.BlockSpec((tm, d), lambda i: (i, 0))]
    out_shape = [jax.ShapeDtypeStruct((m, kvx_w), BF16), jax.ShapeDtypeStruct((m, d), BF16)]
    if with_q:
        out_specs.insert(0, pl.BlockSpec((tm, q_w), lambda i: (i, 0)))
        out_shape.insert(0, jax.ShapeDtypeStruct((m, q_w), BF16))
    out = pl.pallas_call(
        functools.partial(_inproj_kernel, use_rope=use_rope, with_q=with_q),
        grid=(m // tm,),
        in_specs=in_specs,
        out_specs=out_specs,
        out_shape=out_shape,
        compiler_params=_params("parallel"),
        name="inproj" if with_q else "inproj_kv",
    )(*args)
    return out if with_q else (None,) + tuple(out)


def _cast_plan(weights, n_steps, step_of):
    in_specs, out_specs, out_shape = [], [], []
    for w, layer in weights:
        _, rows, cols = w.shape
        nblk = n_steps
        while rows % (nblk * BF16_ROWS):
            assert nblk % 2 == 0, (rows, n_steps)
            nblk //= 2
        per = n_steps // nblk

        def src_index(*g, per=per, layer=layer):
            return (layer, step_of(*g) // per, 0)

        def dst_index(*g, per=per):
            return (step_of(*g) // per, 0)
        in_specs.append(pl.BlockSpec((None, rows // nblk, cols), src_index))
        out_specs.append(pl.BlockSpec((rows // nblk, cols), dst_index))
        out_shape.append(jax.ShapeDtypeStruct((rows, cols), BF16))
    return in_specs, out_specs, out_shape


def _run_casts(src_refs, dst_refs):
    for src, dst in zip(src_refs, dst_refs):
        dst[...] = src[...].astype(BF16)


def _conv_kernel(*refs, tm, seq, ncast):
    hp_ref, h_ref, hn_ref, wb_ref, wc_ref, wx_ref, cw_ref, cb_ref = refs[:8]
    y_ref, h_scr = refs[8 + ncast], refs[-1]
    _run_casts(refs[8:8 + ncast], refs[9 + ncast:-1])
    i = pl.program_id(0)
    j = pl.program_id(1)
    halo = BF16_ROWS

    @pl.when(j == 0)
    def _():
        h_scr[0:halo] = hp_ref[...]
        h_scr[halo:halo + tm] = h_ref[...]
        h_scr[halo + tm:] = hn_ref[...]

    hx = h_scr[...]
    u = _dot(hx, wc_ref[...]) * _dot(hx, wx_ref[...])
    rows = tm + 2 * halo
    u_up = pltpu.roll(u, 1, 0)[halo:halo + tm]
    u_dn = pltpu.roll(u, rows - 1, 0)[halo:halo + tm]
    row = lax.broadcasted_iota(jnp.int32, (tm, 1), 0)
    seq_start = (i * tm) % seq == 0
    seq_end = ((i + 1) * tm) % seq == 0
    u_up = jnp.where(jnp.logical_and(row == 0, seq_start), 0.0, u_up)
    u_dn = jnp.where(jnp.logical_and(row == tm - 1, seq_end), 0.0, u_dn)
    w = cw_ref[...]
    conv = u_up * w[0:1] + u[halo:halo + tm] * w[1:2] + u_dn * w[2:3] + cb_ref[...]
    y_ref[...] = (_dot(h_ref[...], wb_ref[...]) * conv).astype(BF16)


def _conv_branch(h, w_bf, conv_w, conv_b, layer, *, tm, tn, seq, cast=()):
    m, d = h.shape
    d_conv = conv_w.shape[-1]
    halo = BF16_ROWS
    off = (N_HEADS * HEAD_DIM + 2 * N_KV_HEADS * HEAD_DIM) // tn
    nc = d_conv // tn
    hb = tm // halo
    last_hb = m // halo - 1

    def wspec(k):
        return pl.BlockSpec((d, tn), lambda i, j: (0, off + k * nc + j))

    c_in, c_out, c_shape = _cast_plan(cast, (m // tm) * nc, lambda i, j: i * nc + j)
    out = pl.pallas_call(
        functools.partial(_conv_kernel, tm=tm, seq=seq, ncast=len(cast)),
        grid=(m // tm, nc),
        in_specs=[
            pl.BlockSpec((halo, d), lambda i, j: (jnp.maximum(i * hb - 1, 0), 0)),
            pl.BlockSpec((tm, d), lambda i, j: (i, 0)),
            pl.BlockSpec((halo, d), lambda i, j: (jnp.minimum((i + 1) * hb, last_hb), 0)),
            wspec(0), wspec(1), wspec(2),
            pl.BlockSpec((None, 3, tn), lambda i, j: (layer, 0, j)),
            pl.BlockSpec((None, 1, tn), lambda i, j: (layer, 0, j)),
        ] + c_in,
        out_specs=[pl.BlockSpec((tm, tn), lambda i, j: (i, j))] + c_out,
        out_shape=[jax.ShapeDtypeStruct((m, d_conv), BF16)] + c_shape,
        scratch_shapes=[pltpu.VMEM((tm + 2 * halo, d), BF16)],
        compiler_params=_params("arbitrary", "arbitrary"),
        name="conv_branch",
    )(h, h, h, w_bf, w_bf, w_bf, conv_w, conv_b, *(w for w, _ in cast))
    return out[0], tuple(out[1:])


def _group_attention(q_ref, rows, pieces, sinks, ones_sel, o_ref):
    blk = ATT_BLOCK
    npair = len(sinks) // 2
    qs = jnp.concatenate([q_ref[rows, jp * LANES:(jp + 1) * LANES] for jp in range(npair)], axis=0)
    probs, vals, sink_p = [], [], []
    for par in range(2):
        chunks = []
        for kvx, biases in pieces:
            s = _dot_nt(qs, kvx[:, par * LANES:(par + 1) * LANES])
            for ci in range(s.shape[1] // LANES):
                ch = s[:, ci * LANES:(ci + 1) * LANES]
                if biases is not None and biases[ci] is not None:
                    ch = ch + jnp.concatenate([biases[ci]] * npair, axis=0)
                chunks.append(ch)
            vals.append(kvx[:, (2 + par) * LANES:(3 + par) * LANES])
        sink = jnp.concatenate([jnp.full((blk, LANES), sinks[2 * jp + par], F32) for jp in range(npair)], axis=0)
        mx = jnp.maximum(jnp.max(functools.reduce(jnp.maximum, chunks), axis=-1, keepdims=True), sink)
        probs.extend(jnp.exp2(ch - mx).astype(BF16) for ch in chunks)
        sink_p.append(jnp.exp2(sink - mx))
    rhs = jnp.concatenate([jnp.concatenate(vals, axis=0), ones_sel], axis=1)
    out = _dot(jnp.concatenate(probs, axis=1), rhs)
    lane = lax.broadcasted_iota(jnp.int32, (npair * blk, LANES), 1)
    den = out[:, LANES:] + jnp.where(lane < HEAD_DIM, sink_p[0], sink_p[1])
    res = out[:, :LANES] / den
    for jp in range(npair):
        o_ref[rows, jp * LANES:(jp + 1) * LANES] = res[jp * blk:(jp + 1) * blk].astype(BF16)


def _ones_selector(n):
    row = lax.broadcasted_iota(jnp.int32, (2 * n, LANES), 0)
    lane = lax.broadcasted_iota(jnp.int32, (2 * n, LANES), 1)
    return jnp.where((row < n) == (lane < HEAD_DIM), 1.0, 0.0).astype(BF16)


def _attn_kernel(*refs, tq, nblk_seq, ncast):
    sink_ref, q_ref, kvp_ref, kvc_ref, kvn_ref, ctx_ref = refs[:6]
    o_ref, band_scr = refs[6 + ncast], refs[-1]
    _run_casts(refs[6:6 + ncast], refs[7 + ncast:-1])
    i = pl.program_id(1)
    h = pl.program_id(2)
    blk = ATT_BLOCK
    groups = N_HEADS // N_KV_HEADS
    band_scr[0:blk] = kvp_ref[...]
    band_scr[blk:blk + tq] = kvc_ref[...]
    band_scr[blk + tq:] = kvn_ref[...]
    ctx = ctx_ref[...]
    sinks = [sink_ref[h * groups + g] * LOG2E for g in range(groups)]
    r = lax.broadcasted_iota(jnp.int32, (blk, blk), 0)
    c = lax.broadcasted_iota(jnp.int32, (blk, blk), 1)
    bias_prev = jnp.where(c >= r, 0.0, NEG)
    bias_next = jnp.where(c <= r, 0.0, NEG)
    ones_sel = _ones_selector(ctx.shape[0] + 3 * blk)

    for sb in range(tq // blk):
        r0 = sb * blk
        gblk = i * (tq // blk) + sb
        pen_first = jnp.where(gblk == 0, NEG, 0.0)
        pen_last = jnp.where(gblk == nblk_seq - 1, NEG, 0.0)
        biases = (bias_prev + pen_first, None, bias_next + pen_last)
        band = band_scr[r0:r0 + 3 * blk, :]
        _group_attention(q_ref, pl.ds(r0, blk), ((ctx, None), (band, biases)), sinks, ones_sel, o_ref)


def _attention(q, kvx, kvx_ctx, sink, *, batch, seq, n_ctx, tq, cast=()):
    m, q_w = q.shape
    blk = ATT_BLOCK
    hw = q_w // N_KV_HEADS
    kw = 4 * LANES
    nb = seq // blk
    tpb = tq // blk

    nt = seq // tq
    c_in, c_out, c_shape = _cast_plan(cast, batch * nt * N_KV_HEADS,
                                      lambda b, i, h: (b * nt + i) * N_KV_HEADS + h)
    out = pl.pallas_call(
        functools.partial(_attn_kernel, tq=tq, nblk_seq=nb, ncast=len(cast)),
        grid=(batch, nt, N_KV_HEADS),
        in_specs=[
            pl.BlockSpec(memory_space=pltpu.SMEM),
            pl.BlockSpec((tq, hw), lambda b, i, h: (b * nt + i, h)),
            pl.BlockSpec((blk, kw), lambda b, i, h: (b * nb + jnp.maximum(i * tpb - 1, 0), h)),
            pl.BlockSpec((tq, kw), lambda b, i, h: (b * nt + i, h)),
            pl.BlockSpec((blk, kw), lambda b, i, h: (b * nb + jnp.minimum((i + 1) * tpb, nb - 1), h)),
            pl.BlockSpec((n_ctx, kw), lambda b, i, h: (b, h)),
        ] + c_in,
        out_specs=[pl.BlockSpec((tq, hw), lambda b, i, h: (b * nt + i, h))] + c_out,
        out_shape=[jax.ShapeDtypeStruct((m, q_w), BF16)] + c_shape,
        scratch_shapes=[pltpu.VMEM((tq + 2 * blk, kw), BF16)],
        compiler_params=_params("arbitrary", "arbitrary", "arbitrary"),
        name="band_attention",
    )(sink, q, kvx, kvx, kvx, kvx_ctx, *(w for w, _ in cast))
    return out[0], tuple(out[1:])


def _ctx_attn_kernel(sink_ref, q_ref, ctx_ref, o_ref, *, n_ctx):
    h = pl.program_id(1)
    groups = N_HEADS // N_KV_HEADS
    ctx = ctx_ref[...]
    sinks = [sink_ref[h * groups + g] * LOG2E for g in range(groups)]
    ones_sel = _ones_selector(n_ctx)
    for r0 in range(0, n_ctx, ATT_BLOCK):
        _group_attention(q_ref, pl.ds(r0, ATT_BLOCK), ((ctx, None),), sinks, ones_sel, o_ref)


def _ctx_attention(q, kvx_ctx, sink, *, batch, n_ctx):
    m, q_w = q.shape
    hw = q_w // N_KV_HEADS
    kw = 4 * LANES
    return pl.pallas_call(
        functools.partial(_ctx_attn_kernel, n_ctx=n_ctx),
        grid=(batch, N_KV_HEADS),
        in_specs=[
            pl.BlockSpec(memory_space=pltpu.SMEM),
            pl.BlockSpec((n_ctx, hw), lambda b, h: (b, h)),
            pl.BlockSpec((n_ctx, kw), lambda b, h: (b, h)),
        ],
        out_specs=pl.BlockSpec((n_ctx, hw), lambda b, h: (b, h)),
        out_shape=jax.ShapeDtypeStruct((m, q_w), BF16),
        compiler_params=_params("parallel", "arbitrary"),
        name="ctx_attention",
    )(sink, q, kvx_ctx)


def _merge_kernel(a_ref, y_ref, h_ref, wa_ref, wc_ref, wga_ref, wgc_ref, m_ref):
    h = h_ref[...]
    m = (_sigmoid(_dot(h, wga_ref[...])) * _dot(a_ref[...], wa_ref[...])
         + _sigmoid(_dot(h, wgc_ref[...])) * _dot(y_ref[...], wc_ref[...]))
    m_ref[...] = m.astype(BF16)


def _merge(attn, y, h, w_in_bf, wa_bf, wc_bf, layer, *, tm, tn):
    m, q_w = attn.shape
    d_conv = y.shape[1]
    d_in = h.shape[1]
    d = wa_bf.shape[-1]
    nn = d // tn
    gate_off = (q_w + 2 * N_KV_HEADS * HEAD_DIM + 3 * d_conv) // tn
    return pl.pallas_call(
        _merge_kernel,
        grid=(m // tm, nn),
        in_specs=[
            pl.BlockSpec((tm, q_w), lambda i, j: (i, 0)),
            pl.BlockSpec((tm, d_conv), lambda i, j: (i, 0)),
            pl.BlockSpec((tm, d_in), lambda i, j: (i, 0)),
            pl.BlockSpec((q_w, tn), lambda i, j: (0, j)),
            pl.BlockSpec((d_conv, tn), lambda i, j: (0, j)),
            pl.BlockSpec((d_in, tn), lambda i, j: (0, gate_off + j)),
            pl.BlockSpec((d_in, tn), lambda i, j: (0, gate_off + nn + j)),
        ],
        out_specs=pl.BlockSpec((tm, tn), lambda i, j: (i, j)),
        out_shape=jax.ShapeDtypeStruct((m, d), BF16),
        compiler_params=_params("parallel", "arbitrary"),
        name="merge",
    )(attn, y, h, wa_bf, wc_bf, w_in_bf, w_in_bf)


def _outproj_kernel(m_ref, w_ref, x_ref, g_ref, o_ref):
    rows = o_ref.shape[0]
    rc = _tile(rows, OUT_ROW_CHUNK)
    for r0 in range(0, rows, rc):
        rs = slice(r0, r0 + rc)
        o_ref[rs] = x_ref[rs] + g_ref[...] * _dot(m_ref[rs], w_ref[...])


def _outproj(mm, xr, mods, wo_bf, layer, *, tm, tn, row_of_tile):
    m, d = xr.shape
    return pl.pallas_call(
        _outproj_kernel,
        grid=(m // tm, d // tn),
        in_specs=[
            pl.BlockSpec((tm, d), lambda i, j: (i, 0)),
            pl.BlockSpec((d, tn), lambda i, j: (0, j)),
            pl.BlockSpec((tm, tn), lambda i, j: (i, j)),
            pl.BlockSpec((None, None, None, 1, tn), lambda i, j: (layer, row_of_tile(i), 2, 0, j)),
        ],
        out_specs=pl.BlockSpec((tm, tn), lambda i, j: (i, j)),
        out_shape=jax.ShapeDtypeStruct((m, d), F32),
        compiler_params=_params("parallel", "arbitrary"),
        name="outproj",
    )(mm, wo_bf, xr, mods)


def _ffn_kernel(*refs, final, nf):
    if final:
        x_ref, g_ref, sh_ref, sc_ref, gt_ref, wg_ref, wu_ref, wo_ref, fg_ref, o_ref, h_scr = refs
    else:
        x_ref, g_ref, sh_ref, sc_ref, gt_ref, wg_ref, wu_ref, wo_ref, o_ref, h_scr = refs
    f = pl.program_id(1)
    rows, d = o_ref.shape
    cw = _tile(d, FFN_OUT_CHUNK)

    def step(first, last):
        rc = _tile(rows, FFN_ROW_CHUNK) if first or last else rows
        for r0 in range(0, rows, rc):
            rs = slice(r0, r0 + rc)
            if first:
                h = _norm_mod(x_ref[rs], g_ref[...] * (1.0 + sc_ref[...]), sh_ref[...]).astype(BF16)
                h_scr[rs] = h
            else:
                h = h_scr[rs]
            gate = _dot(h, wg_ref[...])
            act = (gate * _sigmoid(gate) * _dot(h, wu_ref[...])).astype(BF16)
            for c0 in range(0, d, cw):
                part = _dot(act, wo_ref[:, c0:c0 + cw])
                if first:
                    o_ref[rs, c0:c0 + cw] = part
                else:
                    o_ref[rs, c0:c0 + cw] += part
            if last:
                xn = x_ref[rs] + gt_ref[...] * o_ref[rs]
                if final:
                    xn = xn * lax.rsqrt(jnp.mean(xn * xn, axis=-1, keepdims=True) + EPS) * fg_ref[...]
                o_ref[rs] = xn

    if nf == 1:
        step(True, True)
    else:
        pl.when(f == 0)(lambda: step(True, False))
        pl.when(jnp.logical_and(f > 0, f < nf - 1))(lambda: step(False, False))
        pl.when(f == nf - 1)(lambda: step(False, True))


def _ffn(xr, norm_g, mods, wi_bf, wo_bf, layer, *, tm, tf, row_of_tile, final_g):
    m, d = xr.shape
    d_ff = wo_bf.shape[0]
    nf = d_ff // tf
    final = final_g is not None
    in_specs = [
        pl.BlockSpec((tm, d), lambda i, f: (i, 0)),
        pl.BlockSpec((None, 1, d), lambda i, f: (layer, 0, 0)),
        pl.BlockSpec((None, None, None, 1, d), _mod_spec(layer, 3, row_of_tile)),
        pl.BlockSpec((None, None, None, 1, d), _mod_spec(layer, 4, row_of_tile)),
        pl.BlockSpec((None, None, None, 1, d), _mod_spec(layer, 5, row_of_tile)),
        pl.BlockSpec((d, tf), lambda i, f: (0, f)),
        pl.BlockSpec((d, tf), lambda i, f: (0, nf + f)),
        pl.BlockSpec((tf, d), lambda i, f: (f, 0)),
    ]
    args = [xr, norm_g, mods, mods, mods, wi_bf, wi_bf, wo_bf]
    if final:
        in_specs.append(pl.BlockSpec((1, d), lambda i, f: (0, 0)))
        args.append(final_g)
    return pl.pallas_call(
        functools.partial(_ffn_kernel, final=final, nf=nf),
        grid=(m // tm, nf),
        in_specs=in_specs,
        out_specs=pl.BlockSpec((tm, d), lambda i, f: (i, 0)),
        out_shape=jax.ShapeDtypeStruct((m, d), F32),
        scratch_shapes=[pltpu.VMEM((tm, d), BF16)],
        compiler_params=_params("parallel", "arbitrary"),
        name="ffn_final" if final else "ffn",
    )(*args)


def _rope_tables(seq):
    quarter = HEAD_DIM // 4
    t = np.arange(seq)
    pos = np.stack([t // GRID_W, t % GRID_W], axis=1).astype(np.float32)
    d = np.arange(LANES) % HEAD_DIM
    axis = d // (HEAD_DIM // 2)
    first_half = (d // quarter) % 2 == 0
    freqs = np.float32(ROPE_THETA) ** (-np.arange(quarter, dtype=np.float32) / np.float32(quarter))
    ang = (pos[:, axis] * freqs[d % quarter][None, :]).astype(np.float32)
    cos, sin = np.cos(ang), np.sin(ang)
    fh = first_half[None, :]
    zero = np.float32(0.0)
    return tuple(jnp.asarray(t, F32) for t in (cos, np.where(fh, -sin, zero), np.where(fh, zero, sin)))


def _tile(n, pref):
    return pref if n % pref == 0 else n


def kernel(x, c, ctx, c_ctx, ada_w, ada_b, norm1_g, norm2_g, w_in, conv_w, conv_b, sink,
           w_attn_out, w_conv_out, w_o, w_ffn_in, w_ffn_out, final_g):
    batch, seq, d = x.shape
    n_ctx = ctx.shape[1]
    depth = ada_w.shape[0]
    d_ff = w_ffn_out.shape[1]
    assert batch + 1 <= MOD_ROWS and seq % ATT_BLOCK == 0 and n_ctx % ATT_BLOCK == 0

    cc = jnp.zeros((MOD_ROWS, d), F32).at[:batch].set(c).at[batch].set(c_ctx)
    mods = _ada(cc, ada_w, ada_b).reshape(depth, MOD_ROWS, 6, 1, d)

    n1 = norm1_g.reshape(depth, 1, d)
    n2 = norm2_g.reshape(depth, 1, d)
    conv_b3 = conv_b.reshape(depth, 1, -1)
    fg = final_g.reshape(1, d)
    rope_tabs = _rope_tables(seq)

    tm = _tile(seq, 1024)
    tm_out = _tile(seq, 1024)
    tn = 2 * N_KV_HEADS * HEAD_DIM
    tf = _tile(d_ff, tn)
    tq = _tile(seq, 2048)
    m_ctx = batch * n_ctx
    lat_row = lambda t: (lambda i: i // (seq // t))
    ctx_row = lambda i: batch

    late = lambda l: tuple((w, l) for w in (w_attn_out, w_conv_out, w_o, w_ffn_in, w_ffn_out))
    wts = {0: (w_in[0].astype(BF16),)}

    xl = x.reshape(batch * seq, d)
    xc = ctx.reshape(m_ctx, d)
    for l in range(depth):
        last = l == depth - 1
        w_in_bf = wts[l][0]
        q_c, kvx_c, h_c = _inproj(xc, n1, mods, w_in_bf, l, tm=m_ctx, row_of_tile=ctx_row, rope_tabs=None,
                                  seq=n_ctx, with_q=not last)
        q, kvx, h = _inproj(xl, n1, mods, w_in_bf, l, tm=tm, row_of_tile=lat_row(tm),
                            rope_tabs=rope_tabs, seq=seq, with_q=True)
        y, cast0 = _conv_branch(h, w_in_bf, conv_w, conv_b3, l, tm=tm, tn=tn, seq=seq,
                                cast=late(0) if l == 0 else ())
        if l == 0:
            wts[0] = wts[0] + cast0
        attn, cast1 = _attention(q, kvx, kvx_c, sink[l], batch=batch, seq=seq, n_ctx=n_ctx, tq=tq,
                                 cast=() if last else ((w_in, l + 1),) + late(l + 1))
        if not last:
            wts[l + 1] = cast1
        _, wa_bf, wc_bf, wo_bf, wfi_bf, wfo_bf = wts[l]
        mm = _merge(attn, y, h, w_in_bf, wa_bf, wc_bf, l, tm=tm, tn=tn)
        xl = _outproj(mm, xl, mods, wo_bf, l, tm=tm_out, tn=d, row_of_tile=lat_row(tm_out))

        if not last:
            y_c, _ = _conv_branch(h_c, w_in_bf, conv_w, conv_b3, l, tm=n_ctx, tn=tn, seq=n_ctx)
            attn_c = _ctx_attention(q_c, kvx_c, sink[l], batch=batch, n_ctx=n_ctx)
            mm_c = _merge(attn_c, y_c, h_c, w_in_bf, wa_bf, wc_bf, l, tm=m_ctx, tn=tn)
            xc = _outproj(mm_c, xc, mods, wo_bf, l, tm=m_ctx, tn=d, row_of_tile=ctx_row)
            xc = _ffn(xc, n2, mods, wfi_bf, wfo_bf, l, tm=m_ctx, tf=tf, row_of_tile=ctx_row, final_g=None)

        xl = _ffn(xl, n2, mods, wfi_bf, wfo_bf, l, tm=tm, tf=tf, row_of_tile=lat_row(tm),
                  final_g=fg if last else None)
    return xl.reshape(batch, seq, d)
```

```python
import functools

import jax
import jax.numpy as jnp
import numpy as np
from jax import lax
from jax.experimental import pallas as pl
from jax.experimental.pallas import tpu as pltpu

GRID_W = 64
N_HEADS = 32
N_KV_HEADS = 4
HEAD_DIM = 64
WINDOW = 128
ROPE_THETA = 10000.0
EPS = 1e-6
NEG = -1e30
LOG2E = 1.4426950408889634
Q_SCALE = HEAD_DIM ** -0.5 * LOG2E

LANES = 128
BF16_ROWS = 16
ROW_CHUNK = 256
V7X_VMEM_BYTES = 64 * 1024 * 1024
VMEM_LIMIT_BYTES = V7X_VMEM_BYTES * 15 // 16
ATT_BLOCK = 128
MOD_ROWS = 8
FFN_OUT_CHUNK = 512
OUT_ROW_CHUNK = 512
FFN_ROW_CHUNK = 512

F32 = jnp.float32
BF16 = jnp.bfloat16


def _params(*sem):
    return pltpu.CompilerParams(dimension_semantics=sem, vmem_limit_bytes=VMEM_LIMIT_BYTES)


def _sigmoid(v):
    return 1.0 / (1.0 + jnp.exp(-v))


def _norm_mod(x, gain, shift):
    y = x * lax.rsqrt(jnp.mean(x * x, axis=-1, keepdims=True) + EPS)
    return y * gain + shift


def _dot(a, b):
    return jnp.dot(a, b, preferred_element_type=F32)


def _dot_nt(a, b):
    return lax.dot_general(a, b, (((1,), (1,)), ((), ())), preferred_element_type=F32)


def _ada_kernel(c_ref, w_ref, b_ref, o_ref):
    c = c_ref[...]
    s = c * _sigmoid(c)
    o_ref[...] = _dot(s.astype(BF16), w_ref[...].astype(BF16)) + b_ref[...]


def _ada(cc, ada_w, ada_b):
    depth, d, n = ada_w.shape
    tn = _tile(n, 1024)
    return pl.pallas_call(
        _ada_kernel,
        grid=(depth, n // tn),
        in_specs=[
            pl.BlockSpec((MOD_ROWS, d), lambda l, j: (0, 0)),
            pl.BlockSpec((None, d, tn), lambda l, j: (l, 0, j)),
            pl.BlockSpec((None, 1, tn), lambda l, j: (l, 0, j)),
        ],
        out_specs=pl.BlockSpec((None, MOD_ROWS, tn), lambda l, j: (l, 0, j)),
        out_shape=jax.ShapeDtypeStruct((depth, MOD_ROWS, n), F32),
        compiler_params=_params("parallel", "parallel"),
        name="ada_mod",
    )(cc, ada_w, ada_b.reshape(depth, 1, n))


def _mod_spec(layer, which, row_of_tile):
    def index(i, j):
        return (layer, row_of_tile(i), which, 0, 0)
    return index


def _rope(v, cos, sin_a, sin_b):
    return v * cos + pltpu.roll(v, LANES - 16, 1) * sin_a + pltpu.roll(v, 16, 1) * sin_b


def _inproj_kernel(*refs, use_rope, with_q):
    refs = list(refs)
    x_ref, g_ref, sh_ref, sc_ref, wkv_ref = refs[:5]
    del refs[:5]
    wq_ref = refs.pop(0) if with_q else None
    if use_rope:
        cos_ref, sa_ref, sb_ref = refs[:3]
        del refs[:3]
    q_ref = refs.pop(0) if with_q else None
    kvx_ref, h_ref = refs

    rows = h_ref.shape[0]
    rc = _tile(rows, ROW_CHUNK)
    gain, sh = g_ref[...] * (1.0 + sc_ref[...]), sh_ref[...]
    lo_mask = lax.broadcasted_iota(jnp.int32, (rc, LANES), 1) < HEAD_DIM
    for r0 in range(0, rows, rc):
        rs = slice(r0, r0 + rc)
        rope = (lambda v: _rope(v, cos_ref[rs], sa_ref[rs], sb_ref[rs])) if use_rope else (lambda v: v)
        h = _norm_mod(x_ref[rs], gain, sh).astype(BF16)
        h_ref[rs] = h
        acc = _dot(h, wkv_ref[...])
        kvw = acc.shape[1] // 2
        for part in range(2):
            for pr in range(kvw // LANES):
                v = acc[:, part * kvw + pr * LANES: part * kvw + (pr + 1) * LANES]
                if part == 0:
                    v = rope(v)
                r = pltpu.roll(v, HEAD_DIM, 1)
                tiles = ((jnp.where(lo_mask, v, 0.0), jnp.where(lo_mask, 0.0, r)),
                         (jnp.where(lo_mask, r, 0.0), jnp.where(lo_mask, 0.0, v)))
                for odd in range(2):
                    head = 2 * pr + odd
                    for hi in range(2):
                        col = head * 4 * LANES + (2 * part + hi) * LANES
                        kvx_ref[rs, col:col + LANES] = tiles[odd][hi].astype(BF16)
        if with_q:
            acc = _dot(h, wq_ref[...])
            for cg in range(acc.shape[1] // LANES):
                v = rope(acc[:, cg * LANES:(cg + 1) * LANES])
                q_ref[rs, cg * LANES:(cg + 1) * LANES] = (v * Q_SCALE).astype(BF16)


def _inproj(xr, norm_g, mods, w_bf, layer, *, tm, row_of_tile, rope_tabs, seq, with_q):
    m, d = xr.shape
    q_w = N_HEADS * HEAD_DIM
    kv_w = N_KV_HEADS * HEAD_DIM
    use_rope = rope_tabs is not None
    kvx_w = N_KV_HEADS * 4 * LANES

    in_specs = [
        pl.BlockSpec((tm, d), lambda i: (i, 0)),
        pl.BlockSpec((None, 1, d), lambda i: (layer, 0, 0)),
        pl.BlockSpec((None, None, None, 1, d), lambda i: (layer, row_of_tile(i), 0, 0, 0)),
        pl.BlockSpec((None, None, None, 1, d), lambda i: (layer, row_of_tile(i), 1, 0, 0)),
        pl.BlockSpec((d, 2 * kv_w), lambda i: (0, q_w // (2 * kv_w))),
    ]
    args = [xr, norm_g, mods, mods, w_bf]
    if with_q:
        in_specs.append(pl.BlockSpec((d, q_w), lambda i: (0, 0)))
        args.append(w_bf)
    if use_rope:
        tps = seq // tm
        for t in rope_tabs:
            in_specs.append(pl.BlockSpec((tm, LANES), lambda i: (i % tps, 0)))
            args.append(t)

    out_specs = [pl.BlockSpec((tm, kvx_w), lambda i: (i, 0)),
                 pl.BlockSpec((tm, d), lambda i: (i, 0))]
    out_shape = [jax.ShapeDtypeStruct((m, kvx_w), BF16), jax.ShapeDtypeStruct((m, d), BF16)]
    if with_q:
        out_specs.insert(0, pl.BlockSpec((tm, q_w), lambda i: (i, 0)))
        out_shape.insert(0, jax.ShapeDtypeStruct((m, q_w), BF16))
    out = pl.pallas_call(
        functools.partial(_inproj_kernel, use_rope=use_rope, with_q=with_q),
        grid=(m // tm,),
        in_specs=in_specs,
        out_specs=out_specs,
        out_shape=out_shape,
        compiler_params=_params("parallel"),
        name="inproj" if with_q else "inproj_kv",
    )(*args)
    return out if with_q else (None,) + tuple(out)


def _cast_plan(weights, n_steps, step_of):
    in_specs, out_specs, out_shape = [], [], []
    for w, layer in weights:
        _, rows, cols = w.shape
        nblk = n_steps
        while rows % (nblk * BF16_ROWS):
            assert nblk % 2 == 0, (rows, n_steps)
            nblk //= 2
        per = n_steps // nblk

        def src_index(*g, per=per, layer=layer):
            return (layer, step_of(*g) // per, 0)

        def dst_index(*g, per=per):
            return (step_of(*g) // per, 0)
        in_specs.append(pl.BlockSpec((None, rows // nblk, cols), src_index))
        out_specs.append(pl.BlockSpec((rows // nblk, cols), dst_index))
        out_shape.append(jax.ShapeDtypeStruct((rows, cols), BF16))
    return in_specs, out_specs, out_shape


def _run_casts(src_refs, dst_refs):
    for src, dst in zip(src_refs, dst_refs):
        dst[...] = src[...].astype(BF16)


def _conv_kernel(*refs, tm, seq, ncast, nchunk, resident):
    hp_ref, h_ref, hn_ref = refs[:3]
    w_refs = refs[3:3 + 3 * nchunk]
    n_in = 5 + 3 * nchunk
    cw_ref, cb_ref = refs[n_in - 2:n_in]
    y_ref, h_scr = refs[n_in + ncast], refs[-1]
    _run_casts(refs[n_in:n_in + ncast], refs[n_in + ncast + 1:-1])
    i = pl.program_id(0)
    halo = BF16_ROWS

    def fill_halo():
        h_scr[0:halo] = hp_ref[...]
        h_scr[halo:halo + tm] = h_ref[...]
        h_scr[halo + tm:] = hn_ref[...]

    if resident:
        fill_halo()
    else:
        pl.when(pl.program_id(1) == 0)(fill_halo)

    hx = h_scr[...]
    tn = w_refs[0].shape[1]
    rows = tm + 2 * halo
    row = lax.broadcasted_iota(jnp.int32, (tm, 1), 0)
    at_start = jnp.logical_and(row == 0, (i * tm) % seq == 0)
    at_end = jnp.logical_and(row == tm - 1, ((i + 1) * tm) % seq == 0)
    for c in range(nchunk):
        cs = slice(c * tn, (c + 1) * tn)
        u = _dot(hx, w_refs[nchunk + c][...]) * _dot(hx, w_refs[2 * nchunk + c][...])
        u_up = jnp.where(at_start, 0.0, pltpu.roll(u, 1, 0)[halo:halo + tm])
        u_dn = jnp.where(at_end, 0.0, pltpu.roll(u, rows - 1, 0)[halo:halo + tm])
        w = cw_ref[:, cs]
        conv = u_up * w[0:1] + u[halo:halo + tm] * w[1:2] + u_dn * w[2:3] + cb_ref[:, cs]
        y_ref[:, cs] = (_dot(h_ref[...], w_refs[c][...]) * conv).astype(BF16)


def _conv_branch(h, w_bf, conv_w, conv_b, layer, *, tm, tn, seq, cast=()):
    m, d = h.shape
    d_conv = conv_w.shape[-1]
    halo = BF16_ROWS
    off = (N_HEADS * HEAD_DIM + 2 * N_KV_HEADS * HEAD_DIM) // tn
    nc = d_conv // tn
    hb = tm // halo
    last_hb = m // halo - 1
    resident = not cast
    nchunk, yw = (nc, d_conv) if resident else (1, tn)
    col = (lambda g: 0) if resident else (lambda g: g[1])

    w_specs = [pl.BlockSpec((d, tn), lambda *g, k=k, c=c: (0, off + k * nc + c + col(g)))
               for k in range(3) for c in range(nchunk)]
    grid = (m // tm,) if resident else (m // tm, nc)
    c_in, c_out, c_shape = _cast_plan(cast, (m // tm) * nc, lambda i, j: i * nc + j)
    out = pl.pallas_call(
        functools.partial(_conv_kernel, tm=tm, seq=seq, ncast=len(cast), nchunk=nchunk, resident=resident),
        grid=grid,
        in_specs=[
            pl.BlockSpec((halo, d), lambda *g: (jnp.maximum(g[0] * hb - 1, 0), 0)),
            pl.BlockSpec((tm, d), lambda *g: (g[0], 0)),
            pl.BlockSpec((halo, d), lambda *g: (jnp.minimum((g[0] + 1) * hb, last_hb), 0)),
        ] + w_specs + [
            pl.BlockSpec((None, 3, yw), lambda *g: (layer, 0, col(g))),
            pl.BlockSpec((None, 1, yw), lambda *g: (layer, 0, col(g))),
        ] + c_in,
        out_specs=[pl.BlockSpec((tm, yw), lambda *g: (g[0], col(g)))] + c_out,
        out_shape=[jax.ShapeDtypeStruct((m, d_conv), BF16)] + c_shape,
        scratch_shapes=[pltpu.VMEM((tm + 2 * halo, d), BF16)],
        compiler_params=_params(*(["arbitrary"] * len(grid))),
        name="conv_branch",
    )(h, h, h, *([w_bf] * (3 * nchunk)), conv_w, conv_b, *(w for w, _ in cast))
    return out[0], tuple(out[1:])


def _group_attention(q_ref, rows, pieces, sinks, ones_sel, o_ref):
    blk = ATT_BLOCK
    npair = len(sinks) // 2
    qs = jnp.concatenate([q_ref[rows, jp * LANES:(jp + 1) * LANES] for jp in range(npair)], axis=0)
    probs, vals, sink_p = [], [], []
    for par in range(2):
        chunks = []
        for kvx, biases in pieces:
            s = _dot_nt(qs, kvx[:, par * LANES:(par + 1) * LANES])
            for ci in range(s.shape[1] // LANES):
                ch = s[:, ci * LANES:(ci + 1) * LANES]
                if biases is not None and biases[ci] is not None:
                    ch = ch + jnp.concatenate([biases[ci]] * npair, axis=0)
                chunks.append(ch)
            vals.append(kvx[:, (2 + par) * LANES:(3 + par) * LANES])
        sink = jnp.concatenate([jnp.full((blk, LANES), sinks[2 * jp + par], F32) for jp in range(npair)], axis=0)
        mx = jnp.maximum(jnp.max(functools.reduce(jnp.maximum, chunks), axis=-1, keepdims=True), sink)
        probs.extend(jnp.exp2(ch - mx).astype(BF16) for ch in chunks)
        sink_p.append(jnp.exp2(sink - mx))
    rhs = jnp.concatenate([jnp.concatenate(vals, axis=0), ones_sel], axis=1)
    out = _dot(jnp.concatenate(probs, axis=1), rhs)
    lane = lax.broadcasted_iota(jnp.int32, (npair * blk, LANES), 1)
    den = out[:, LANES:] + jnp.where(lane < HEAD_DIM, sink_p[0], sink_p[1])
    res = out[:, :LANES] / den
    for jp in range(npair):
        o_ref[rows, jp * LANES:(jp + 1) * LANES] = res[jp * blk:(jp + 1) * blk].astype(BF16)


def _ones_selector(n):
    row = lax.broadcasted_iota(jnp.int32, (2 * n, LANES), 0)
    lane = lax.broadcasted_iota(jnp.int32, (2 * n, LANES), 1)
    return jnp.where((row < n) == (lane < HEAD_DIM), 1.0, 0.0).astype(BF16)


def _attn_kernel(*refs, tq, nblk_seq, ncast):
    sink_ref, q_ref, kvp_ref, kvc_ref, kvn_ref, ctx_ref = refs[:6]
    o_ref, band_scr = refs[6 + ncast], refs[-1]
    _run_casts(refs[6:6 + ncast], refs[7 + ncast:-1])
    i = pl.program_id(1)
    h = pl.program_id(2)
    blk = ATT_BLOCK
    groups = N_HEADS // N_KV_HEADS
    band_scr[0:blk] = kvp_ref[...]
    band_scr[blk:blk + tq] = kvc_ref[...]
    band_scr[blk + tq:] = kvn_ref[...]
    ctx = ctx_ref[...]
    sinks = [sink_ref[h * groups + g] * LOG2E for g in range(groups)]
    r = lax.broadcasted_iota(jnp.int32, (blk, blk), 0)
    c = lax.broadcasted_iota(jnp.int32, (blk, blk), 1)
    bias_prev = jnp.where(c >= r, 0.0, NEG)
    bias_next = jnp.where(c <= r, 0.0, NEG)
    ones_sel = _ones_selector(ctx.shape[0] + 3 * blk)

    for sb in range(tq // blk):
        r0 = sb * blk
        gblk = i * (tq // blk) + sb
        pen_first = jnp.where(gblk == 0, NEG, 0.0)
        pen_last = jnp.where(gblk == nblk_seq - 1, NEG, 0.0)
        biases = (bias_prev + pen_first, None, bias_next + pen_last)
        band = band_scr[r0:r0 + 3 * blk, :]
        _group_attention(q_ref, pl.ds(r0, blk), ((ctx, None), (band, biases)), sinks, ones_sel, o_ref)


def _attention(q, kvx, kvx_ctx, sink, *, batch, seq, n_ctx, tq, cast=()):
    m, q_w = q.shape
    blk = ATT_BLOCK
    hw = q_w // N_KV_HEADS
    kw = 4 * LANES
    nb = seq // blk
    tpb = tq // blk

    nt = seq // tq
    c_in, c_out, c_shape = _cast_plan(cast, batch * nt * N_KV_HEADS,
                                      lambda b, i, h: (b * nt + i) * N_KV_HEADS + h)
    out = pl.pallas_call(
        functools.partial(_attn_kernel, tq=tq, nblk_seq=nb, ncast=len(cast)),
        grid=(batch, nt, N_KV_HEADS),
        in_specs=[
            pl.BlockSpec(memory_space=pltpu.SMEM),
            pl.BlockSpec((tq, hw), lambda b, i, h: (b * nt + i, h)),
            pl.BlockSpec((blk, kw), lambda b, i, h: (b * nb + jnp.maximum(i * tpb - 1, 0), h)),
            pl.BlockSpec((tq, kw), lambda b, i, h: (b * nt + i, h)),
            pl.BlockSpec((blk, kw), lambda b, i, h: (b * nb + jnp.minimum((i + 1) * tpb, nb - 1), h)),
            pl.BlockSpec((n_ctx, kw), lambda b, i, h: (b, h)),
        ] + c_in,
        out_specs=[pl.BlockSpec((tq, hw), lambda b, i, h: (b * nt + i, h))] + c_out,
        out_shape=[jax.ShapeDtypeStruct((m, q_w), BF16)] + c_shape,
        scratch_shapes=[pltpu.VMEM((tq + 2 * blk, kw), BF16)],
        compiler_params=_params("arbitrary", "arbitrary", "arbitrary"),
        name="band_attention",
    )(sink, q, kvx, kvx, kvx, kvx_ctx, *(w for w, _ in cast))
    return out[0], tuple(out[1:])


def _ctx_attn_kernel(sink_ref, q_ref, ctx_ref, o_ref, *, n_ctx):
    h = pl.program_id(1)
    groups = N_HEADS // N_KV_HEADS
    ctx = ctx_ref[...]
    sinks = [sink_ref[h * groups + g] * LOG2E for g in range(groups)]
    ones_sel = _ones_selector(n_ctx)
    for r0 in range(0, n_ctx, ATT_BLOCK):
        _group_attention(q_ref, pl.ds(r0, ATT_BLOCK), ((ctx, None),), sinks, ones_sel, o_ref)


def _ctx_attention(q, kvx_ctx, sink, *, batch, n_ctx):
    m, q_w = q.shape
    hw = q_w // N_KV_HEADS
    kw = 4 * LANES
    return pl.pallas_call(
        functools.partial(_ctx_attn_kernel, n_ctx=n_ctx),
        grid=(batch, N_KV_HEADS),
        in_specs=[
            pl.BlockSpec(memory_space=pltpu.SMEM),
            pl.BlockSpec((n_ctx, hw), lambda b, h: (b, h)),
            pl.BlockSpec((n_ctx, kw), lambda b, h: (b, h)),
        ],
        out_specs=pl.BlockSpec((n_ctx, hw), lambda b, h: (b, h)),
        out_shape=jax.ShapeDtypeStruct((m, q_w), BF16),
        compiler_params=_params("parallel", "arbitrary"),
        name="ctx_attention",
    )(sink, q, kvx_ctx)


def _merge_kernel(a_ref, y_ref, h_ref, wa_ref, wc_ref, wga_ref, wgc_ref, m_ref):
    h = h_ref[...]
    m = (_sigmoid(_dot(h, wga_ref[...])) * _dot(a_ref[...], wa_ref[...])
         + _sigmoid(_dot(h, wgc_ref[...])) * _dot(y_ref[...], wc_ref[...]))
    m_ref[...] = m.astype(BF16)


def _merge(attn, y, h, w_in_bf, wa_bf, wc_bf, layer, *, tm, tn):
    m, q_w = attn.shape
    d_conv = y.shape[1]
    d_in = h.shape[1]
    d = wa_bf.shape[-1]
    nn = d // tn
    gate_off = (q_w + 2 * N_KV_HEADS * HEAD_DIM + 3 * d_conv) // tn
    return pl.pallas_call(
        _merge_kernel,
        grid=(m // tm, nn),
        in_specs=[
            pl.BlockSpec((tm, q_w), lambda i, j: (i, 0)),
            pl.BlockSpec((tm, d_conv), lambda i, j: (i, 0)),
            pl.BlockSpec((tm, d_in), lambda i, j: (i, 0)),
            pl.BlockSpec((q_w, tn), lambda i, j: (0, j)),
            pl.BlockSpec((d_conv, tn), lambda i, j: (0, j)),
            pl.BlockSpec((d_in, tn), lambda i, j: (0, gate_off + j)),
            pl.BlockSpec((d_in, tn), lambda i, j: (0, gate_off + nn + j)),
        ],
        out_specs=pl.BlockSpec((tm, tn), lambda i, j: (i, j)),
        out_shape=jax.ShapeDtypeStruct((m, d), BF16),
        compiler_params=_params("parallel", "arbitrary"),
        name="merge",
    )(attn, y, h, wa_bf, wc_bf, w_in_bf, w_in_bf)


def _outproj_kernel(m_ref, w_ref, x_ref, g_ref, o_ref):
    rows = o_ref.shape[0]
    rc = _tile(rows, OUT_ROW_CHUNK)
    for r0 in range(0, rows, rc):
        rs = slice(r0, r0 + rc)
        o_ref[rs] = x_ref[rs] + g_ref[...] * _dot(m_ref[rs], w_ref[...])


def _outproj(mm, xr, mods, wo_bf, layer, *, tm, tn, row_of_tile):
    m, d = xr.shape
    return pl.pallas_call(
        _outproj_kernel,
        grid=(m // tm, d // tn),
        in_specs=[
            pl.BlockSpec((tm, d), lambda i, j: (i, 0)),
            pl.BlockSpec((d, tn), lambda i, j: (0, j)),
            pl.BlockSpec((tm, tn), lambda i, j: (i, j)),
            pl.BlockSpec((None, None, None, 1, tn), lambda i, j: (layer, row_of_tile(i), 2, 0, j)),
        ],
        out_specs=pl.BlockSpec((tm, tn), lambda i, j: (i, j)),
        out_shape=jax.ShapeDtypeStruct((m, d), F32),
        compiler_params=_params("parallel", "arbitrary"),
        name="outproj",
    )(mm, wo_bf, xr, mods)


def _ffn_kernel(*refs, final, nf):
    if final:
        x_ref, g_ref, sh_ref, sc_ref, gt_ref, wg_ref, wu_ref, wo_ref, fg_ref, o_ref, h_scr = refs
    else:
        x_ref, g_ref, sh_ref, sc_ref, gt_ref, wg_ref, wu_ref, wo_ref, o_ref, h_scr = refs
    f = pl.program_id(1)
    rows, d = o_ref.shape
    cw = _tile(d, FFN_OUT_CHUNK)

    def step(first, last):
        rc = _tile(rows, FFN_ROW_CHUNK) if first or last else rows
        for r0 in range(0, rows, rc):
            rs = slice(r0, r0 + rc)
            if first:
                h = _norm_mod(x_ref[rs], g_ref[...] * (1.0 + sc_ref[...]), sh_ref[...]).astype(BF16)
                h_scr[rs] = h
            else:
                h = h_scr[rs]
            gate = _dot(h, wg_ref[...])
            act = (gate * _sigmoid(gate) * _dot(h, wu_ref[...])).astype(BF16)
            for c0 in range(0, d, cw):
                part = _dot(act, wo_ref[:, c0:c0 + cw])
                if first:
                    o_ref[rs, c0:c0 + cw] = part
                else:
                    o_ref[rs, c0:c0 + cw] += part
            if last:
                xn = x_ref[rs] + gt_ref[...] * o_ref[rs]
                if final:
                    xn = xn * lax.rsqrt(jnp.mean(xn * xn, axis=-1, keepdims=True) + EPS) * fg_ref[...]
                o_ref[rs] = xn

    if nf == 1:
        step(True, True)
    else:
        pl.when(f == 0)(lambda: step(True, False))
        pl.when(jnp.logical_and(f > 0, f < nf - 1))(lambda: step(False, False))
        pl.when(f == nf - 1)(lambda: step(False, True))


def _ffn(xr, norm_g, mods, wi_bf, wo_bf, layer, *, tm, tf, row_of_tile, final_g):
    m, d = xr.shape
    d_ff = wo_bf.shape[0]
    nf = d_ff // tf
    final = final_g is not None
    in_specs = [
        pl.BlockSpec((tm, d), lambda i, f: (i, 0)),
        pl.BlockSpec((None, 1, d), lambda i, f: (layer, 0, 0)),
        pl.BlockSpec((None, None, None, 1, d), _mod_spec(layer, 3, row_of_tile)),
        pl.BlockSpec((None, None, None, 1, d), _mod_spec(layer, 4, row_of_tile)),
        pl.BlockSpec((None, None, None, 1, d), _mod_spec(layer, 5, row_of_tile)),
        pl.BlockSpec((d, tf), lambda i, f: (0, f)),
        pl.BlockSpec((d, tf), lambda i, f: (0, nf + f)),
        pl.BlockSpec((tf, d), lambda i, f: (f, 0)),
    ]
    args = [xr, norm_g, mods, mods, mods, wi_bf, wi_bf, wo_bf]
    if final:
        in_specs.append(pl.BlockSpec((1, d), lambda i, f: (0, 0)))
        args.append(final_g)
    return pl.pallas_call(
        functools.partial(_ffn_kernel, final=final, nf=nf),
        grid=(m // tm, nf),
        in_specs=in_specs,
        out_specs=pl.BlockSpec((tm, d), lambda i, f: (i, 0)),
        out_shape=jax.ShapeDtypeStruct((m, d), F32),
        scratch_shapes=[pltpu.VMEM((tm, d), BF16)],
        compiler_params=_params("parallel", "arbitrary"),
        name="ffn_final" if final else "ffn",
    )(*args)


def _rope_tables(seq):
    quarter = HEAD_DIM // 4
    t = np.arange(seq)
    pos = np.stack([t // GRID_W, t % GRID_W], axis=1).astype(np.float32)
    d = np.arange(LANES) % HEAD_DIM
    axis = d // (HEAD_DIM // 2)
    first_half = (d // quarter) % 2 == 0
    freqs = np.float32(ROPE_THETA) ** (-np.arange(quarter, dtype=np.float32) / np.float32(quarter))
    ang = (pos[:, axis] * freqs[d % quarter][None, :]).astype(np.float32)
    cos, sin = np.cos(ang), np.sin(ang)
    fh = first_half[None, :]
    zero = np.float32(0.0)
    return tuple(jnp.asarray(t, F32) for t in (cos, np.where(fh, -sin, zero), np.where(fh, zero, sin)))


def _tile(n, pref):
    return pref if n % pref == 0 else n


def kernel(x, c, ctx, c_ctx, ada_w, ada_b, norm1_g, norm2_g, w_in, conv_w, conv_b, sink,
           w_attn_out, w_conv_out, w_o, w_ffn_in, w_ffn_out, final_g):
    batch, seq, d = x.shape
    n_ctx = ctx.shape[1]
    depth = ada_w.shape[0]
    d_ff = w_ffn_out.shape[1]
    assert batch + 1 <= MOD_ROWS and seq % ATT_BLOCK == 0 and n_ctx % ATT_BLOCK == 0

    cc = jnp.zeros((MOD_ROWS, d), F32).at[:batch].set(c).at[batch].set(c_ctx)
    mods = _ada(cc, ada_w, ada_b).reshape(depth, MOD_ROWS, 6, 1, d)

    n1 = norm1_g.reshape(depth, 1, d)
    n2 = norm2_g.reshape(depth, 1, d)
    conv_b3 = conv_b.reshape(depth, 1, -1)
    fg = final_g.reshape(1, d)
    rope_tabs = _rope_tables(seq)

    tm = _tile(seq, 1024)
    tm_out = _tile(seq, 1024)
    tn = 2 * N_KV_HEADS * HEAD_DIM
    tf = _tile(d_ff, tn)
    tq = _tile(seq, 2048)
    m_ctx = batch * n_ctx
    lat_row = lambda t: (lambda i: i // (seq // t))
    ctx_row = lambda i: batch

    late = lambda l: tuple((w, l) for w in (w_attn_out, w_conv_out, w_o, w_ffn_in, w_ffn_out))
    wts = {0: (w_in[0].astype(BF16),)}

    xl = x.reshape(batch * seq, d)
    xc = ctx.reshape(m_ctx, d)
    for l in range(depth):
        last = l == depth - 1
        w_in_bf = wts[l][0]
        q_c, kvx_c, h_c = _inproj(xc, n1, mods, w_in_bf, l, tm=m_ctx, row_of_tile=ctx_row, rope_tabs=None,
                                  seq=n_ctx, with_q=not last)
        q, kvx, h = _inproj(xl, n1, mods, w_in_bf, l, tm=tm, row_of_tile=lat_row(tm),
                            rope_tabs=rope_tabs, seq=seq, with_q=True)
        y, cast0 = _conv_branch(h, w_in_bf, conv_w, conv_b3, l, tm=tm, tn=tn, seq=seq,
                                cast=late(0) if l == 0 else ())
        if l == 0:
            wts[0] = wts[0] + cast0
        attn, cast1 = _attention(q, kvx, kvx_c, sink[l], batch=batch, seq=seq, n_ctx=n_ctx, tq=tq,
                                 cast=() if last else ((w_in, l + 1),) + late(l + 1))
        if not last:
            wts[l + 1] = cast1
        _, wa_bf, wc_bf, wo_bf, wfi_bf, wfo_bf = wts[l]
        mm = _merge(attn, y, h, w_in_bf, wa_bf, wc_bf, l, tm=tm, tn=tn)
        xl = _outproj(mm, xl, mods, wo_bf, l, tm=tm_out, tn=d, row_of_tile=lat_row(tm_out))

        if not last:
            y_c, _ = _conv_branch(h_c, w_in_bf, conv_w, conv_b3, l, tm=n_ctx, tn=tn, seq=n_ctx)
            attn_c = _ctx_attention(q_c, kvx_c, sink[l], batch=batch, n_ctx=n_ctx)
            mm_c = _merge(attn_c, y_c, h_c, w_in_bf, wa_bf, wc_bf, l, tm=m_ctx, tn=tn)
            xc = _outproj(mm_c, xc, mods, wo_bf, l, tm=m_ctx, tn=d, row_of_tile=ctx_row)
            xc = _ffn(xc, n2, mods, wfi_bf, wfo_bf, l, tm=m_ctx, tf=tf, row_of_tile=ctx_row, final_g=None)

        xl = _ffn(xl, n2, mods, wfi_bf, wfo_bf, l, tm=tm, tf=tf, row_of_tile=lat_row(tm),
                  final_g=fg if last else None)
    return xl.reshape(batch, seq, d)
```

```python
import functools

import jax
import jax.numpy as jnp
import numpy as np
from jax import lax
from jax.experimental import pallas as pl
from jax.experimental.pallas import tpu as pltpu

GRID_W = 64
N_HEADS = 32
N_KV_HEADS = 4
HEAD_DIM = 64
WINDOW = 128
ROPE_THETA = 10000.0
EPS = 1e-6
NEG = -1e30
LOG2E = 1.4426950408889634
Q_SCALE = HEAD_DIM ** -0.5 * LOG2E

LANES = 128
BF16_ROWS = 16
ROW_CHUNK = 256
V7X_VMEM_BYTES = 64 * 1024 * 1024
VMEM_LIMIT_BYTES = V7X_VMEM_BYTES * 15 // 16
ATT_BLOCK = 128
MOD_ROWS = 8
FFN_OUT_CHUNK = 512
OUT_ROW_CHUNK = 512
FFN_ROW_CHUNK = 512

F32 = jnp.float32
BF16 = jnp.bfloat16


def _params(*sem):
    return pltpu.CompilerParams(dimension_semantics=sem, vmem_limit_bytes=VMEM_LIMIT_BYTES)


def _sigmoid(v):
    return 1.0 / (1.0 + jnp.exp(-v))


def _norm_mod(x, gain, shift):
    y = x * lax.rsqrt(jnp.mean(x * x, axis=-1, keepdims=True) + EPS)
    return y * gain + shift


def _dot(a, b):
    return jnp.dot(a, b, preferred_element_type=F32)


def _dot_nt(a, b):
    return lax.dot_general(a, b, (((1,), (1,)), ((), ())), preferred_element_type=F32)


def _ada_kernel(c_ref, w_ref, b_ref, o_ref):
    c = c_ref[...]
    s = c * _sigmoid(c)
    o_ref[...] = _dot(s.astype(BF16), w_ref[...].astype(BF16)) + b_ref[...]


def _ada(cc, ada_w, ada_b):
    depth, d, n = ada_w.shape
    tn = _tile(n, 1024)
    return pl.pallas_call(
        _ada_kernel,
        grid=(depth, n // tn),
        in_specs=[
            pl.BlockSpec((MOD_ROWS, d), lambda l, j: (0, 0)),
            pl.BlockSpec((None, d, tn), lambda l, j: (l, 0, j)),
            pl.BlockSpec((None, 1, tn), lambda l, j: (l, 0, j)),
        ],
        out_specs=pl.BlockSpec((None, MOD_ROWS, tn), lambda l, j: (l, 0, j)),
        out_shape=jax.ShapeDtypeStruct((depth, MOD_ROWS, n), F32),
        compiler_params=_params("parallel", "parallel"),
        name="ada_mod",
    )(cc, ada_w, ada_b.reshape(depth, 1, n))


def _mod_spec(layer, which, row_of_tile):
    def index(i, j):
        return (layer, row_of_tile(i), which, 0, 0)
    return index


def _rope(v, cos, sin_a, sin_b):
    return v * cos + pltpu.roll(v, LANES - 16, 1) * sin_a + pltpu.roll(v, 16, 1) * sin_b


def _inproj_kernel(*refs, use_rope, with_q):
    refs = list(refs)
    x_ref, g_ref, sh_ref, sc_ref, wkv_ref = refs[:5]
    del refs[:5]
    wq_ref = refs.pop(0) if with_q else None
    if use_rope:
        cos_ref, sa_ref, sb_ref = refs[:3]
        del refs[:3]
    q_ref = refs.pop(0) if with_q else None
    kvx_ref, h_ref = refs

    rows = h_ref.shape[0]
    rc = _tile(rows, ROW_CHUNK)
    gain, sh = g_ref[...] * (1.0 + sc_ref[...]), sh_ref[...]
    lo_mask = lax.broadcasted_iota(jnp.int32, (rc, LANES), 1) < HEAD_DIM
    for r0 in range(0, rows, rc):
        rs = slice(r0, r0 + rc)
        rope = (lambda v: _rope(v, cos_ref[rs], sa_ref[rs], sb_ref[rs])) if use_rope else (lambda v: v)
        h = _norm_mod(x_ref[rs], gain, sh).astype(BF16)
        h_ref[rs] = h
        acc = _dot(h, wkv_ref[...])
        kvw = acc.shape[1] // 2
        for part in range(2):
            for pr in range(kvw // LANES):
                v = acc[:, part * kvw + pr * LANES: part * kvw + (pr + 1) * LANES]
                if part == 0:
                    v = rope(v)
                r = pltpu.roll(v, HEAD_DIM, 1)
                tiles = ((jnp.where(lo_mask, v, 0.0), jnp.where(lo_mask, 0.0, r)),
                         (jnp.where(lo_mask, r, 0.0), jnp.where(lo_mask, 0.0, v)))
                for odd in range(2):
                    head = 2 * pr + odd
                    for hi in range(2):
                        col = head * 4 * LANES + (2 * part + hi) * LANES
                        kvx_ref[rs, col:col + LANES] = tiles[odd][hi].astype(BF16)
        if with_q:
            acc = _dot(h, wq_ref[...])
            for cg in range(acc.shape[1] // LANES):
                v = rope(acc[:, cg * LANES:(cg + 1) * LANES])
                q_ref[rs, cg * LANES:(cg + 1) * LANES] = (v * Q_SCALE).astype(BF16)


def _inproj(xr, norm_g, mods, w_bf, layer, *, tm, row_of_tile, rope_tabs, seq, with_q):
    m, d = xr.shape
    q_w = N_HEADS * HEAD_DIM
    kv_w = N_KV_HEADS * HEAD_DIM
    use_rope = rope_tabs is not None
    kvx_w = N_KV_HEADS * 4 * LANES

    in_specs = [
        pl.BlockSpec((tm, d), lambda i: (i, 0)),
        pl.BlockSpec((None, 1, d), lambda i: (layer, 0, 0)),
        pl.BlockSpec((None, None, None, 1, d), lambda i: (layer, row_of_tile(i), 0, 0, 0)),
        pl.BlockSpec((None, None, None, 1, d), lambda i: (layer, row_of_tile(i), 1, 0, 0)),
        pl.BlockSpec((d, 2 * kv_w), lambda i: (0, q_w // (2 * kv_w))),
    ]
    args = [xr, norm_g, mods, mods, w_bf]
    if with_q:
        in_specs.append(pl.BlockSpec((d, q_w), lambda i: (0, 0)))
        args.append(w_bf)
    if use_rope:
        tps = seq // tm
        for t in rope_tabs:
            in_specs.append(pl.BlockSpec((tm, LANES), lambda i: (i % tps, 0)))
            args.append(t)

    out_specs = [pl.BlockSpec((tm, kvx_w), lambda i: (i, 0)),
                 pl.BlockSpec((tm, d), lambda i: (i, 0))]
    out_shape = [jax.ShapeDtypeStruct((m, kvx_w), BF16), jax.ShapeDtypeStruct((m, d), BF16)]
    if with_q:
        out_specs.insert(0, pl.BlockSpec((tm, q_w), lambda i: (i, 0)))
        out_shape.insert(0, jax.ShapeDtypeStruct((m, q_w), BF16))
    out = pl.pallas_call(
        functools.partial(_inproj_kernel, use_rope=use_rope, with_q=with_q),
        grid=(m // tm,),
        in_specs=in_specs,
        out_specs=out_specs,
        out_shape=out_shape,
        compiler_params=_params("parallel"),
        name="inproj" if with_q else "inproj_kv",
    )(*args)
    return out if with_q else (None,) + tuple(out)


def _cast_plan(weights, n_steps, step_of):
    in_specs, out_specs, out_shape = [], [], []
    for w, layer in weights:
        _, rows, cols = w.shape
        nblk = n_steps
        while rows % (nblk * BF16_ROWS):
            assert nblk % 2 == 0, (rows, n_steps)
            nblk //= 2
        per = n_steps // nblk

        def src_index(*g, per=per, layer=layer):
            return (layer, step_of(*g) // per, 0)

        def dst_index(*g, per=per):
            return (step_of(*g) // per, 0)
        in_specs.append(pl.BlockSpec((None, rows // nblk, cols), src_index))
        out_specs.append(pl.BlockSpec((rows // nblk, cols), dst_index))
        out_shape.append(jax.ShapeDtypeStruct((rows, cols), BF16))
    return in_specs, out_specs, out_shape


def _run_casts(src_refs, dst_refs):
    for src, dst in zip(src_refs, dst_refs):
        dst[...] = src[...].astype(BF16)


def _conv_kernel(*refs, tm, seq, ncast, nchunk, resident):
    hp_ref, h_ref, hn_ref = refs[:3]
    w_refs = refs[3:3 + 3 * nchunk]
    n_in = 5 + 3 * nchunk
    cw_ref, cb_ref = refs[n_in - 2:n_in]
    y_ref, h_scr = refs[n_in + ncast], refs[-1]
    _run_casts(refs[n_in:n_in + ncast], refs[n_in + ncast + 1:-1])
    i = pl.program_id(0)
    halo = BF16_ROWS

    def fill_halo():
        h_scr[0:halo] = hp_ref[...]
        h_scr[halo:halo + tm] = h_ref[...]
        h_scr[halo + tm:] = hn_ref[...]

    if resident:
        fill_halo()
    else:
        pl.when(pl.program_id(1) == 0)(fill_halo)

    hx = h_scr[...]
    tn = w_refs[0].shape[1]
    rows = tm + 2 * halo
    row = lax.broadcasted_iota(jnp.int32, (tm, 1), 0)
    at_start = jnp.logical_and(row == 0, (i * tm) % seq == 0)
    at_end = jnp.logical_and(row == tm - 1, ((i + 1) * tm) % seq == 0)
    for c in range(nchunk):
        cs = slice(c * tn, (c + 1) * tn)
        u = _dot(hx, w_refs[nchunk + c][...]) * _dot(hx, w_refs[2 * nchunk + c][...])
        u_up = jnp.where(at_start, 0.0, pltpu.roll(u, 1, 0)[halo:halo + tm])
        u_dn = jnp.where(at_end, 0.0, pltpu.roll(u, rows - 1, 0)[halo:halo + tm])
        w = cw_ref[:, cs]
        conv = u_up * w[0:1] + u[halo:halo + tm] * w[1:2] + u_dn * w[2:3] + cb_ref[:, cs]
        y_ref[:, cs] = (_dot(h_ref[...], w_refs[c][...]) * conv).astype(BF16)


def _conv_branch(h, w_bf, conv_w, conv_b, layer, *, tm, tn, seq, cast=()):
    m, d = h.shape
    d_conv = conv_w.shape[-1]
    halo = BF16_ROWS
    off = (N_HEADS * HEAD_DIM + 2 * N_KV_HEADS * HEAD_DIM) // tn
    nc = d_conv // tn
    hb = tm // halo
    last_hb = m // halo - 1
    resident = not cast
    nchunk, yw = (nc, d_conv) if resident else (1, tn)
    col = (lambda g: 0) if resident else (lambda g: g[1])

    w_specs = [pl.BlockSpec((d, tn), lambda *g, k=k, c=c: (0, off + k * nc + c + col(g)))
               for k in range(3) for c in range(nchunk)]
    grid = (m // tm,) if resident else (m // tm, nc)
    c_in, c_out, c_shape = _cast_plan(cast, (m // tm) * nc, lambda i, j: i * nc + j)
    out = pl.pallas_call(
        functools.partial(_conv_kernel, tm=tm, seq=seq, ncast=len(cast), nchunk=nchunk, resident=resident),
        grid=grid,
        in_specs=[
            pl.BlockSpec((halo, d), lambda *g: (jnp.maximum(g[0] * hb - 1, 0), 0)),
            pl.BlockSpec((tm, d), lambda *g: (g[0], 0)),
            pl.BlockSpec((halo, d), lambda *g: (jnp.minimum((g[0] + 1) * hb, last_hb), 0)),
        ] + w_specs + [
            pl.BlockSpec((None, 3, yw), lambda *g: (layer, 0, col(g))),
            pl.BlockSpec((None, 1, yw), lambda *g: (layer, 0, col(g))),
        ] + c_in,
        out_specs=[pl.BlockSpec((tm, yw), lambda *g: (g[0], col(g)))] + c_out,
        out_shape=[jax.ShapeDtypeStruct((m, d_conv), BF16)] + c_shape,
        scratch_shapes=[pltpu.VMEM((tm + 2 * halo, d), BF16)],
        compiler_params=_params(*(["arbitrary"] * len(grid))),
        name="conv_branch",
    )(h, h, h, *([w_bf] * (3 * nchunk)), conv_w, conv_b, *(w for w, _ in cast))
    return out[0], tuple(out[1:])


def _group_attention(q_ref, rows, pieces, sinks, ones_sel, o_ref):
    blk = ATT_BLOCK
    npair = len(sinks) // 2
    qs = jnp.concatenate([q_ref[rows, jp * LANES:(jp + 1) * LANES] for jp in range(npair)], axis=0)
    probs, vals, sink_p = [], [], []
    for par in range(2):
        chunks = []
        for kvx, biases in pieces:
            s = _dot_nt(qs, kvx[:, par * LANES:(par + 1) * LANES])
            for ci in range(s.shape[1] // LANES):
                ch = s[:, ci * LANES:(ci + 1) * LANES]
                if biases is not None and biases[ci] is not None:
                    ch = ch + jnp.concatenate([biases[ci]] * npair, axis=0)
                chunks.append(ch)
            vals.append(kvx[:, (2 + par) * LANES:(3 + par) * LANES])
        sink = jnp.concatenate([jnp.full((blk, LANES), sinks[2 * jp + par], F32) for jp in range(npair)], axis=0)
        mx = jnp.maximum(jnp.max(functools.reduce(jnp.maximum, chunks), axis=-1, keepdims=True), sink)
        probs.extend(jnp.exp2(ch - mx).astype(BF16) for ch in chunks)
        sink_p.append(jnp.exp2(sink - mx))
    rhs = jnp.concatenate([jnp.concatenate(vals, axis=0), ones_sel], axis=1)
    out = _dot(jnp.concatenate(probs, axis=1), rhs)
    lane = lax.broadcasted_iota(jnp.int32, (npair * blk, LANES), 1)
    den = out[:, LANES:] + jnp.where(lane < HEAD_DIM, sink_p[0], sink_p[1])
    res = out[:, :LANES] / den
    for jp in range(npair):
        o_ref[rows, jp * LANES:(jp + 1) * LANES] = res[jp * blk:(jp + 1) * blk].astype(BF16)


def _ones_selector(n):
    row = lax.broadcasted_iota(jnp.int32, (2 * n, LANES), 0)
    lane = lax.broadcasted_iota(jnp.int32, (2 * n, LANES), 1)
    return jnp.where((row < n) == (lane < HEAD_DIM), 1.0, 0.0).astype(BF16)


def _attn_kernel(*refs, tq, nblk_seq, ncast):
    sink_ref, q_ref, kvp_ref, kvc_ref, kvn_ref, ctx_ref = refs[:6]
    o_ref, band_scr = refs[6 + ncast], refs[-1]
    _run_casts(refs[6:6 + ncast], refs[7 + ncast:-1])
    i = pl.program_id(1)
    h = pl.program_id(2)
    blk = ATT_BLOCK
    groups = N_HEADS // N_KV_HEADS
    band_scr[0:blk] = kvp_ref[...]
    band_scr[blk:blk + tq] = kvc_ref[...]
    band_scr[blk + tq:] = kvn_ref[...]
    ctx = ctx_ref[...]
    sinks = [sink_ref[h * groups + g] * LOG2E for g in range(groups)]
    r = lax.broadcasted_iota(jnp.int32, (blk, blk), 0)
    c = lax.broadcasted_iota(jnp.int32, (blk, blk), 1)
    bias_prev = jnp.where(c >= r, 0.0, NEG)
    bias_next = jnp.where(c <= r, 0.0, NEG)
    ones_sel = _ones_selector(ctx.shape[0] + 3 * blk)

    for sb in range(tq // blk):
        r0 = sb * blk
        gblk = i * (tq // blk) + sb
        pen_first = jnp.where(gblk == 0, NEG, 0.0)
        pen_last = jnp.where(gblk == nblk_seq - 1, NEG, 0.0)
        biases = (bias_prev + pen_first, None, bias_next + pen_last)
        band = band_scr[r0:r0 + 3 * blk, :]
        _group_attention(q_ref, pl.ds(r0, blk), ((ctx, None), (band, biases)), sinks, ones_sel, o_ref)


def _attention(q, kvx, kvx_ctx, sink, *, batch, seq, n_ctx, tq, cast=()):
    m, q_w = q.shape
    blk = ATT_BLOCK
    hw = q_w // N_KV_HEADS
    kw = 4 * LANES
    nb = seq // blk
    tpb = tq // blk

    nt = seq // tq
    c_in, c_out, c_shape = _cast_plan(cast, batch * nt * N_KV_HEADS,
                                      lambda b, i, h: (b * nt + i) * N_KV_HEADS + h)
    out = pl.pallas_call(
        functools.partial(_attn_kernel, tq=tq, nblk_seq=nb, ncast=len(cast)),
        grid=(batch, nt, N_KV_HEADS),
        in_specs=[
            pl.BlockSpec(memory_space=pltpu.SMEM),
            pl.BlockSpec((tq, hw), lambda b, i, h: (b * nt + i, h)),
            pl.BlockSpec((blk, kw), lambda b, i, h: (b * nb + jnp.maximum(i * tpb - 1, 0), h)),
            pl.BlockSpec((tq, kw), lambda b, i, h: (b * nt + i, h)),
            pl.BlockSpec((blk, kw), lambda b, i, h: (b * nb + jnp.minimum((i + 1) * tpb, nb - 1), h)),
            pl.BlockSpec((n_ctx, kw), lambda b, i, h: (b, h)),
        ] + c_in,
        out_specs=[pl.BlockSpec((tq, hw), lambda b, i, h: (b * nt + i, h))] + c_out,
        out_shape=[jax.ShapeDtypeStruct((m, q_w), BF16)] + c_shape,
        scratch_shapes=[pltpu.VMEM((tq + 2 * blk, kw), BF16)],
        compiler_params=_params("arbitrary", "arbitrary", "arbitrary"),
        name="band_attention",
    )(sink, q, kvx, kvx, kvx, kvx_ctx, *(w for w, _ in cast))
    return out[0], tuple(out[1:])


def _ctx_attn_kernel(sink_ref, q_ref, ctx_ref, o_ref, *, n_ctx):
    h = pl.program_id(1)
    groups = N_HEADS // N_KV_HEADS
    ctx = ctx_ref[...]
    sinks = [sink_ref[h * groups + g] * LOG2E for g in range(groups)]
    ones_sel = _ones_selector(n_ctx)
    for r0 in range(0, n_ctx, ATT_BLOCK):
        _group_attention(q_ref, pl.ds(r0, ATT_BLOCK), ((ctx, None),), sinks, ones_sel, o_ref)


def _ctx_attention(q, kvx_ctx, sink, *, batch, n_ctx):
    m, q_w = q.shape
    hw = q_w // N_KV_HEADS
    kw = 4 * LANES
    return pl.pallas_call(
        functools.partial(_ctx_attn_kernel, n_ctx=n_ctx),
        grid=(batch, N_KV_HEADS),
        in_specs=[
            pl.BlockSpec(memory_space=pltpu.SMEM),
            pl.BlockSpec((n_ctx, hw), lambda b, h: (b, h)),
            pl.BlockSpec((n_ctx, kw), lambda b, h: (b, h)),
        ],
        out_specs=pl.BlockSpec((n_ctx, hw), lambda b, h: (b, h)),
        out_shape=jax.ShapeDtypeStruct((m, q_w), BF16),
        compiler_params=_params("parallel", "arbitrary"),
        name="ctx_attention",
    )(sink, q, kvx_ctx)


def _merge_kernel(a_ref, y_ref, h_ref, wa_ref, wc_ref, wga_ref, wgc_ref, m_ref):
    h = h_ref[...]
    m = (_sigmoid(_dot(h, wga_ref[...])) * _dot(a_ref[...], wa_ref[...])
         + _sigmoid(_dot(h, wgc_ref[...])) * _dot(y_ref[...], wc_ref[...]))
    m_ref[...] = m.astype(BF16)


def _merge(attn, y, h, w_in_bf, wa_bf, wc_bf, layer, *, tm, tn):
    m, q_w = attn.shape
    d_conv = y.shape[1]
    d_in = h.shape[1]
    d = wa_bf.shape[-1]
    nn = d // tn
    gate_off = (q_w + 2 * N_KV_HEADS * HEAD_DIM + 3 * d_conv) // tn
    return pl.pallas_call(
        _merge_kernel,
        grid=(m // tm, nn),
        in_specs=[
            pl.BlockSpec((tm, q_w), lambda i, j: (i, 0)),
            pl.BlockSpec((tm, d_conv), lambda i, j: (i, 0)),
            pl.BlockSpec((tm, d_in), lambda i, j: (i, 0)),
            pl.BlockSpec((q_w, tn), lambda i, j: (0, j)),
            pl.BlockSpec((d_conv, tn), lambda i, j: (0, j)),
            pl.BlockSpec((d_in, tn), lambda i, j: (0, gate_off + j)),
            pl.BlockSpec((d_in, tn), lambda i, j: (0, gate_off + nn + j)),
        ],
        out_specs=pl.BlockSpec((tm, tn), lambda i, j: (i, j)),
        out_shape=jax.ShapeDtypeStruct((m, d), BF16),
        compiler_params=_params("parallel", "arbitrary"),
        name="merge",
    )(attn, y, h, wa_bf, wc_bf, w_in_bf, w_in_bf)


def _outproj_kernel(m_ref, w_ref, x_ref, g_ref, o_ref):
    rows = o_ref.shape[0]
    rc = _tile(rows, OUT_ROW_CHUNK)
    for r0 in range(0, rows, rc):
        rs = slice(r0, r0 + rc)
        o_ref[rs] = x_ref[rs] + g_ref[...] * _dot(m_ref[rs], w_ref[...])


def _outproj(mm, xr, mods, wo_bf, layer, *, tm, tn, row_of_tile):
    m, d = xr.shape
    return pl.pallas_call(
        _outproj_kernel,
        grid=(m // tm, d // tn),
        in_specs=[
            pl.BlockSpec((tm, d), lambda i, j: (i, 0)),
            pl.BlockSpec((d, tn), lambda i, j: (0, j)),
            pl.BlockSpec((tm, tn), lambda i, j: (i, j)),
            pl.BlockSpec((None, None, None, 1, tn), lambda i, j: (layer, row_of_tile(i), 2, 0, j)),
        ],
        out_specs=pl.BlockSpec((tm, tn), lambda i, j: (i, j)),
        out_shape=jax.ShapeDtypeStruct((m, d), F32),
        compiler_params=_params("parallel", "arbitrary"),
        name="outproj",
    )(mm, wo_bf, xr, mods)


def _ffn_kernel(*refs, final, nf):
    if final:
        x_ref, g_ref, sh_ref, sc_ref, gt_ref, wg_ref, wu_ref, wo_ref, fg_ref, o_ref, h_scr = refs
    else:
        x_ref, g_ref, sh_ref, sc_ref, gt_ref, wg_ref, wu_ref, wo_ref, o_ref, h_scr = refs
    f = pl.program_id(1)
    rows, d = o_ref.shape
    cw = _tile(d, FFN_OUT_CHUNK)

    def step(first, last):
        rc = _tile(rows, FFN_ROW_CHUNK) if first or last else rows
        for r0 in range(0, rows, rc):
            rs = slice(r0, r0 + rc)
            if first:
                h = _norm_mod(x_ref[rs], g_ref[...] * (1.0 + sc_ref[...]), sh_ref[...]).astype(BF16)
                h_scr[rs] = h
            else:
                h = h_scr[rs]
            gate = _dot(h, wg_ref[...])
            act = (gate * _sigmoid(gate) * _dot(h, wu_ref[...])).astype(BF16)
            for c0 in range(0, d, cw):
                part = _dot(act, wo_ref[:, c0:c0 + cw])
                if first:
                    o_ref[rs, c0:c0 + cw] = part
                else:
                    o_ref[rs, c0:c0 + cw] += part
            if last:
                xn = x_ref[rs] + gt_ref[...] * o_ref[rs]
                if final:
                    xn = xn * lax.rsqrt(jnp.mean(xn * xn, axis=-1, keepdims=True) + EPS) * fg_ref[...]
                o_ref[rs] = xn

    if nf == 1:
        step(True, True)
    else:
        pl.when(f == 0)(lambda: step(True, False))
        pl.when(jnp.logical_and(f > 0, f < nf - 1))(lambda: step(False, False))
        pl.when(f == nf - 1)(lambda: step(False, True))


def _ffn(xr, norm_g, mods, wi_bf, wo_bf, layer, *, tm, tf, row_of_tile, final_g):
    m, d = xr.shape
    d_ff = wo_bf.shape[0]
    nf = d_ff // tf
    final = final_g is not None
    in_specs = [
        pl.BlockSpec((tm, d), lambda i, f: (i, 0)),
        pl.BlockSpec((None, 1, d), lambda i, f: (layer, 0, 0)),
        pl.BlockSpec((None, None, None, 1, d), _mod_spec(layer, 3, row_of_tile)),
        pl.BlockSpec((None, None, None, 1, d), _mod_spec(layer, 4, row_of_tile)),
        pl.BlockSpec((None, None, None, 1, d), _mod_spec(layer, 5, row_of_tile)),
        pl.BlockSpec((d, tf), lambda i, f: (0, f)),
        pl.BlockSpec((d, tf), lambda i, f: (0, nf + f)),
        pl.BlockSpec((tf, d), lambda i, f: (f, 0)),
    ]
    args = [xr, norm_g, mods, mods, mods, wi_bf, wi_bf, wo_bf]
    if final:
        in_specs.append(pl.BlockSpec((1, d), lambda i, f: (0, 0)))
        args.append(final_g)
    return pl.pallas_call(
        functools.partial(_ffn_kernel, final=final, nf=nf),
        grid=(m // tm, nf),
        in_specs=in_specs,
        out_specs=pl.BlockSpec((tm, d), lambda i, f: (i, 0)),
        out_shape=jax.ShapeDtypeStruct((m, d), F32),
        scratch_shapes=[pltpu.VMEM((tm, d), BF16)],
        compiler_params=_params("parallel", "arbitrary"),
        name="ffn_final" if final else "ffn",
    )(*args)


def _rope_tables(seq):
    quarter = HEAD_DIM // 4
    t = np.arange(seq)
    pos = np.stack([t // GRID_W, t % GRID_W], axis=1).astype(np.float32)
    d = np.arange(LANES) % HEAD_DIM
    axis = d // (HEAD_DIM // 2)
    first_half = (d // quarter) % 2 == 0
    freqs = np.float32(ROPE_THETA) ** (-np.arange(quarter, dtype=np.float32) / np.float32(quarter))
    ang = (pos[:, axis] * freqs[d % quarter][None, :]).astype(np.float32)
    cos, sin = np.cos(ang), np.sin(ang)
    fh = first_half[None, :]
    zero = np.float32(0.0)
    return tuple(jnp.asarray(t, F32) for t in (cos, np.where(fh, -sin, zero), np.where(fh, zero, sin)))


def _tile(n, pref):
    return pref if n % pref == 0 else n


def kernel(x, c, ctx, c_ctx, ada_w, ada_b, norm1_g, norm2_g, w_in, conv_w, conv_b, sink,
           w_attn_out, w_conv_out, w_o, w_ffn_in, w_ffn_out, final_g):
    batch, seq, d = x.shape
    n_ctx = ctx.shape[1]
    depth = ada_w.shape[0]
    d_ff = w_ffn_out.shape[1]
    assert batch + 1 <= MOD_ROWS and seq % ATT_BLOCK == 0 and n_ctx % ATT_BLOCK == 0
    assert WINDOW == ATT_BLOCK, "the band masks assume the window spans exactly one query block each side"
    assert x.shape[-1] == conv_w.shape[-1], "w_in column offsets assume D_CONV == D_MODEL"

    cc = jnp.zeros((MOD_ROWS, d), F32).at[:batch].set(c).at[batch].set(c_ctx)
    mods = _ada(cc, ada_w, ada_b).reshape(depth, MOD_ROWS, 6, 1, d)

    n1 = norm1_g.reshape(depth, 1, d)
    n2 = norm2_g.reshape(depth, 1, d)
    conv_b3 = conv_b.reshape(depth, 1, -1)
    fg = final_g.reshape(1, d)
    rope_tabs = _rope_tables(seq)

    tm = _tile(seq, 1024)
    tm_out = _tile(seq, 1024)
    tn = 2 * N_KV_HEADS * HEAD_DIM
    tf = _tile(d_ff, tn)
    tq = _tile(seq, 2048)
    m_ctx = batch * n_ctx
    lat_row = lambda t: (lambda i: i // (seq // t))
    ctx_row = lambda i: batch

    late = lambda l: tuple((w, l) for w in (w_attn_out, w_conv_out, w_o, w_ffn_in, w_ffn_out))
    wts = {0: (w_in[0].astype(BF16),)}

    xl = x.reshape(batch * seq, d)
    xc = ctx.reshape(m_ctx, d)
    for l in range(depth):
        last = l == depth - 1
        w_in_bf = wts[l][0]
        q_c, kvx_c, h_c = _inproj(xc, n1, mods, w_in_bf, l, tm=m_ctx, row_of_tile=ctx_row, rope_tabs=None,
                                  seq=n_ctx, with_q=not last)
        q, kvx, h = _inproj(xl, n1, mods, w_in_bf, l, tm=tm, row_of_tile=lat_row(tm),
                            rope_tabs=rope_tabs, seq=seq, with_q=True)
        y, cast0 = _conv_branch(h, w_in_bf, conv_w, conv_b3, l, tm=tm, tn=tn, seq=seq,
                                cast=late(0) if l == 0 else ())
        if l == 0:
            wts[0] = wts[0] + cast0
        attn, cast1 = _attention(q, kvx, kvx_c, sink[l], batch=batch, seq=seq, n_ctx=n_ctx, tq=tq,
                                 cast=() if last else ((w_in, l + 1),) + late(l + 1))
        if not last:
            wts[l + 1] = cast1
        _, wa_bf, wc_bf, wo_bf, wfi_bf, wfo_bf = wts[l]
        mm = _merge(attn, y, h, w_in_bf, wa_bf, wc_bf, l, tm=tm, tn=tn)
        xl = _outproj(mm, xl, mods, wo_bf, l, tm=tm_out, tn=d, row_of_tile=lat_row(tm_out))

        if not last:
            y_c, _ = _conv_branch(h_c, w_in_bf, conv_w, conv_b3, l, tm=n_ctx, tn=tn, seq=n_ctx)
            attn_c = _ctx_attention(q_c, kvx_c, sink[l], batch=batch, n_ctx=n_ctx)
            mm_c = _merge(attn_c, y_c, h_c, w_in_bf, wa_bf, wc_bf, l, tm=m_ctx, tn=tn)
            xc = _outproj(mm_c, xc, mods, wo_bf, l, tm=m_ctx, tn=d, row_of_tile=ctx_row)
            xc = _ffn(xc, n2, mods, wfi_bf, wfo_bf, l, tm=m_ctx, tf=tf, row_of_tile=ctx_row, final_g=None)

        xl = _ffn(xl, n2, mods, wfi_bf, wfo_bf, l, tm=tm, tf=tf, row_of_tile=lat_row(tm),
                  final_g=fg if last else None)
    return xl.reshape(batch, seq, d)
```

```python
import functools

import jax
import jax.numpy as jnp
import numpy as np
from jax import lax
from jax.experimental import pallas as pl
from jax.experimental.pallas import tpu as pltpu

GRID_W = 64
N_HEADS = 32
N_KV_HEADS = 4
HEAD_DIM = 64
WINDOW = 128
ROPE_THETA = 10000.0
EPS = 1e-6
NEG = -1e30
LOG2E = 1.4426950408889634
Q_SCALE = HEAD_DIM ** -0.5 * LOG2E

LANES = 128
BF16_ROWS = 16
ROW_CHUNK = 256
V7X_VMEM_BYTES = 64 * 1024 * 1024
VMEM_LIMIT_BYTES = V7X_VMEM_BYTES * 15 // 16
ATT_BLOCK = 128
MOD_ROWS = 8
FFN_OUT_CHUNK = 512
OUT_ROW_CHUNK = 512
FFN_ROW_CHUNK = 512

F32 = jnp.float32
BF16 = jnp.bfloat16


def _params(*sem):
    return pltpu.CompilerParams(dimension_semantics=sem, vmem_limit_bytes=VMEM_LIMIT_BYTES)


def _sigmoid(v):
    return 1.0 / (1.0 + jnp.exp(-v))


def _norm_mod(x, gain, shift):
    y = x * lax.rsqrt(jnp.mean(x * x, axis=-1, keepdims=True) + EPS)
    return y * gain + shift


def _dot(a, b):
    return jnp.dot(a, b, preferred_element_type=F32)


def _dot_nt(a, b):
    return lax.dot_general(a, b, (((1,), (1,)), ((), ())), preferred_element_type=F32)


def _ada_kernel(c_ref, w_ref, b_ref, o_ref):
    c = c_ref[...]
    s = c * _sigmoid(c)
    o_ref[...] = _dot(s.astype(BF16), w_ref[...].astype(BF16)) + b_ref[...]


def _ada(cc, ada_w, ada_b3, n_layers):
    _, d, n = ada_w.shape
    tn = _tile(n, 1024)
    return pl.pallas_call(
        _ada_kernel,
        grid=(n_layers, n // tn),
        in_specs=[
            pl.BlockSpec((MOD_ROWS, d), lambda l, j: (0, 0)),
            pl.BlockSpec((None, d, tn), lambda l, j: (l, 0, j)),
            pl.BlockSpec((None, 1, tn), lambda l, j: (l, 0, j)),
        ],
        out_specs=pl.BlockSpec((None, MOD_ROWS, tn), lambda l, j: (l, 0, j)),
        out_shape=jax.ShapeDtypeStruct((n_layers, MOD_ROWS, n), F32),
        compiler_params=_params("parallel", "parallel"),
        name="ada_mod",
    )(cc, ada_w, ada_b3)


def _mod_spec(which, row_of_tile):
    def index(i, j):
        return (0, row_of_tile(i), which, 0, 0)
    return index


def _rope(v, cos, sin_a, sin_b):
    return v * cos + pltpu.roll(v, LANES - 16, 1) * sin_a + pltpu.roll(v, 16, 1) * sin_b


def _inproj_kernel(*refs, use_rope, with_q):
    refs = list(refs)
    x_ref, g_ref, sh_ref, sc_ref, wkv_ref = refs[:5]
    del refs[:5]
    wq_ref = refs.pop(0) if with_q else None
    if use_rope:
        cos_ref, sa_ref, sb_ref = refs[:3]
        del refs[:3]
    q_ref = refs.pop(0) if with_q else None
    kvx_ref, h_ref = refs

    rows = h_ref.shape[0]
    rc = _tile(rows, ROW_CHUNK)
    gain, sh = g_ref[...] * (1.0 + sc_ref[...]), sh_ref[...]
    lo_mask = lax.broadcasted_iota(jnp.int32, (rc, LANES), 1) < HEAD_DIM
    for r0 in range(0, rows, rc):
        rs = slice(r0, r0 + rc)
        rope = (lambda v: _rope(v, cos_ref[rs], sa_ref[rs], sb_ref[rs])) if use_rope else (lambda v: v)
        h = _norm_mod(x_ref[rs], gain, sh).astype(BF16)
        h_ref[rs] = h
        acc = _dot(h, wkv_ref[...])
        kvw = acc.shape[1] // 2
        for part in range(2):
            for pr in range(kvw // LANES):
                v = acc[:, part * kvw + pr * LANES: part * kvw + (pr + 1) * LANES]
                if part == 0:
                    v = rope(v)
                r = pltpu.roll(v, HEAD_DIM, 1)
                tiles = ((jnp.where(lo_mask, v, 0.0), jnp.where(lo_mask, 0.0, r)),
                         (jnp.where(lo_mask, r, 0.0), jnp.where(lo_mask, 0.0, v)))
                for odd in range(2):
                    head = 2 * pr + odd
                    for hi in range(2):
                        col = head * 4 * LANES + (2 * part + hi) * LANES
                        kvx_ref[rs, col:col + LANES] = tiles[odd][hi].astype(BF16)
        if with_q:
            acc = _dot(h, wq_ref[...])
            for cg in range(acc.shape[1] // LANES):
                v = rope(acc[:, cg * LANES:(cg + 1) * LANES])
                q_ref[rs, cg * LANES:(cg + 1) * LANES] = (v * Q_SCALE).astype(BF16)


def _inproj(xr, norm_g, mods, w_bf, layer, *, tm, row_of_tile, rope_tabs, seq, with_q):
    m, d = xr.shape
    q_w = N_HEADS * HEAD_DIM
    kv_w = N_KV_HEADS * HEAD_DIM
    use_rope = rope_tabs is not None
    kvx_w = N_KV_HEADS * 4 * LANES

    in_specs = [
        pl.BlockSpec((tm, d), lambda i: (i, 0)),
        pl.BlockSpec((None, 1, d), lambda i: (layer, 0, 0)),
        pl.BlockSpec((None, None, None, 1, d), lambda i: (0, row_of_tile(i), 0, 0, 0)),
        pl.BlockSpec((None, None, None, 1, d), lambda i: (0, row_of_tile(i), 1, 0, 0)),
        pl.BlockSpec((d, 2 * kv_w), lambda i: (0, q_w // (2 * kv_w))),
    ]
    args = [xr, norm_g, mods, mods, w_bf]
    if with_q:
        in_specs.append(pl.BlockSpec((d, q_w), lambda i: (0, 0)))
        args.append(w_bf)
    if use_rope:
        tps = seq // tm
        for t in rope_tabs:
            in_specs.append(pl.BlockSpec((tm, LANES), lambda i: (i % tps, 0)))
            args.append(t)

    out_specs = [pl.BlockSpec((tm, kvx_w), lambda i: (i, 0)),
                 pl.BlockSpec((tm, d), lambda i: (i, 0))]
    out_shape = [jax.ShapeDtypeStruct((m, kvx_w), BF16), jax.ShapeDtypeStruct((m, d), BF16)]
    if with_q:
        out_specs.insert(0, pl.BlockSpec((tm, q_w), lambda i: (i, 0)))
        out_shape.insert(0, jax.ShapeDtypeStruct((m, q_w), BF16))
    out = pl.pallas_call(
        functools.partial(_inproj_kernel, use_rope=use_rope, with_q=with_q),
        grid=(m // tm,),
        in_specs=in_specs,
        out_specs=out_specs,
        out_shape=out_shape,
        compiler_params=_params("parallel"),
        name="inproj" if with_q else "inproj_kv",
    )(*args)
    return out if with_q else (None,) + tuple(out)


def _cast_plan(weights, n_steps, step_of):
    in_specs, out_specs, out_shape = [], [], []
    for w, layer in weights:
        _, rows, cols = w.shape
        nblk = n_steps
        while rows % (nblk * BF16_ROWS):
            assert nblk % 2 == 0, (rows, n_steps)
            nblk //= 2
        per = n_steps // nblk

        def src_index(*g, per=per, layer=layer):
            return (layer, step_of(*g) // per, 0)

        def dst_index(*g, per=per):
            return (step_of(*g) // per, 0)
        in_specs.append(pl.BlockSpec((None, rows // nblk, cols), src_index))
        out_specs.append(pl.BlockSpec((rows // nblk, cols), dst_index))
        out_shape.append(jax.ShapeDtypeStruct((rows, cols), BF16))
    return in_specs, out_specs, out_shape


def _run_casts(src_refs, dst_refs):
    for src, dst in zip(src_refs, dst_refs):
        dst[...] = src[...].astype(BF16)


def _conv_kernel(*refs, tm, seq, ncast, nchunk, resident):
    hp_ref, h_ref, hn_ref = refs[:3]
    w_refs = refs[3:3 + 3 * nchunk]
    n_in = 5 + 3 * nchunk
    cw_ref, cb_ref = refs[n_in - 2:n_in]
    y_ref, h_scr = refs[n_in + ncast], refs[-1]
    _run_casts(refs[n_in:n_in + ncast], refs[n_in + ncast + 1:-1])
    i = pl.program_id(0)
    halo = BF16_ROWS

    def fill_halo():
        h_scr[0:halo] = hp_ref[...]
        h_scr[halo:halo + tm] = h_ref[...]
        h_scr[halo + tm:] = hn_ref[...]

    if resident:
        fill_halo()
    else:
        pl.when(pl.program_id(1) == 0)(fill_halo)

    hx = h_scr[...]
    tn = w_refs[0].shape[1]
    rows = tm + 2 * halo
    row = lax.broadcasted_iota(jnp.int32, (tm, 1), 0)
    at_start = jnp.logical_and(row == 0, (i * tm) % seq == 0)
    at_end = jnp.logical_and(row == tm - 1, ((i + 1) * tm) % seq == 0)
    for c in range(nchunk):
        cs = slice(c * tn, (c + 1) * tn)
        u = _dot(hx, w_refs[nchunk + c][...]) * _dot(hx, w_refs[2 * nchunk + c][...])
        u_up = jnp.where(at_start, 0.0, pltpu.roll(u, 1, 0)[halo:halo + tm])
        u_dn = jnp.where(at_end, 0.0, pltpu.roll(u, rows - 1, 0)[halo:halo + tm])
        w = cw_ref[:, cs]
        conv = u_up * w[0:1] + u[halo:halo + tm] * w[1:2] + u_dn * w[2:3] + cb_ref[:, cs]
        y_ref[:, cs] = (_dot(h_ref[...], w_refs[c][...]) * conv).astype(BF16)


def _conv_branch(h, w_bf, conv_w, conv_b, layer, *, tm, tn, seq, cast=()):
    m, d = h.shape
    d_conv = conv_w.shape[-1]
    halo = BF16_ROWS
    off = (N_HEADS * HEAD_DIM + 2 * N_KV_HEADS * HEAD_DIM) // tn
    nc = d_conv // tn
    hb = tm // halo
    last_hb = m // halo - 1
    resident = not cast
    nchunk, yw = (nc, d_conv) if resident else (1, tn)
    col = (lambda g: 0) if resident else (lambda g: g[1])

    w_specs = [pl.BlockSpec((d, tn), lambda *g, k=k, c=c: (0, off + k * nc + c + col(g)))
               for k in range(3) for c in range(nchunk)]
    grid = (m // tm,) if resident else (m // tm, nc)
    c_in, c_out, c_shape = _cast_plan(cast, (m // tm) * nc, lambda i, j: i * nc + j)
    out = pl.pallas_call(
        functools.partial(_conv_kernel, tm=tm, seq=seq, ncast=len(cast), nchunk=nchunk, resident=resident),
        grid=grid,
        in_specs=[
            pl.BlockSpec((halo, d), lambda *g: (jnp.maximum(g[0] * hb - 1, 0), 0)),
            pl.BlockSpec((tm, d), lambda *g: (g[0], 0)),
            pl.BlockSpec((halo, d), lambda *g: (jnp.minimum((g[0] + 1) * hb, last_hb), 0)),
        ] + w_specs + [
            pl.BlockSpec((None, 3, yw), lambda *g: (layer, 0, col(g))),
            pl.BlockSpec((None, 1, yw), lambda *g: (layer, 0, col(g))),
        ] + c_in,
        out_specs=[pl.BlockSpec((tm, yw), lambda *g: (g[0], col(g)))] + c_out,
        out_shape=[jax.ShapeDtypeStruct((m, d_conv), BF16)] + c_shape,
        scratch_shapes=[pltpu.VMEM((tm + 2 * halo, d), BF16)],
        compiler_params=_params(*(["arbitrary"] * len(grid))),
        name="conv_branch",
    )(h, h, h, *([w_bf] * (3 * nchunk)), conv_w, conv_b, *(w for w, _ in cast))
    return out[0], tuple(out[1:])


def _group_attention(q_ref, rows, pieces, sinks, ones_sel, o_ref):
    blk = ATT_BLOCK
    npair = len(sinks) // 2
    qs = jnp.concatenate([q_ref[rows, jp * LANES:(jp + 1) * LANES] for jp in range(npair)], axis=0)
    probs, vals, sink_p = [], [], []
    for par in range(2):
        chunks = []
        for kvx, biases in pieces:
            s = _dot_nt(qs, kvx[:, par * LANES:(par + 1) * LANES])
            for ci in range(s.shape[1] // LANES):
                ch = s[:, ci * LANES:(ci + 1) * LANES]
                if biases is not None and biases[ci] is not None:
                    ch = ch + jnp.concatenate([biases[ci]] * npair, axis=0)
                chunks.append(ch)
            vals.append(kvx[:, (2 + par) * LANES:(3 + par) * LANES])
        sink = jnp.concatenate([jnp.full((blk, LANES), sinks[2 * jp + par], F32) for jp in range(npair)], axis=0)
        mx = jnp.maximum(jnp.max(functools.reduce(jnp.maximum, chunks), axis=-1, keepdims=True), sink)
        probs.extend(jnp.exp2(ch - mx).astype(BF16) for ch in chunks)
        sink_p.append(jnp.exp2(sink - mx))
    rhs = jnp.concatenate([jnp.concatenate(vals, axis=0), ones_sel], axis=1)
    out = _dot(jnp.concatenate(probs, axis=1), rhs)
    lane = lax.broadcasted_iota(jnp.int32, (npair * blk, LANES), 1)
    den = out[:, LANES:] + jnp.where(lane < HEAD_DIM, sink_p[0], sink_p[1])
    res = out[:, :LANES] / den
    for jp in range(npair):
        o_ref[rows, jp * LANES:(jp + 1) * LANES] = res[jp * blk:(jp + 1) * blk].astype(BF16)


def _ones_selector(n):
    row = lax.broadcasted_iota(jnp.int32, (2 * n, LANES), 0)
    lane = lax.broadcasted_iota(jnp.int32, (2 * n, LANES), 1)
    return jnp.where((row < n) == (lane < HEAD_DIM), 1.0, 0.0).astype(BF16)


def _attn_kernel(*refs, tq, nblk_seq, ncast):
    sink_ref, q_ref, kvp_ref, kvc_ref, kvn_ref, ctx_ref = refs[:6]
    o_ref, band_scr = refs[6 + ncast], refs[-1]
    _run_casts(refs[6:6 + ncast], refs[7 + ncast:-1])
    i = pl.program_id(1)
    h = pl.program_id(2)
    blk = ATT_BLOCK
    groups = N_HEADS // N_KV_HEADS
    band_scr[0:blk] = kvp_ref[...]
    band_scr[blk:blk + tq] = kvc_ref[...]
    band_scr[blk + tq:] = kvn_ref[...]
    ctx = ctx_ref[...]
    sinks = [sink_ref[h * groups + g] * LOG2E for g in range(groups)]
    r = lax.broadcasted_iota(jnp.int32, (blk, blk), 0)
    c = lax.broadcasted_iota(jnp.int32, (blk, blk), 1)
    bias_prev = jnp.where(c >= r, 0.0, NEG)
    bias_next = jnp.where(c <= r, 0.0, NEG)
    ones_sel = _ones_selector(ctx.shape[0] + 3 * blk)

    for sb in range(tq // blk):
        r0 = sb * blk
        gblk = i * (tq // blk) + sb
        pen_first = jnp.where(gblk == 0, NEG, 0.0)
        pen_last = jnp.where(gblk == nblk_seq - 1, NEG, 0.0)
        biases = (bias_prev + pen_first, None, bias_next + pen_last)
        band = band_scr[r0:r0 + 3 * blk, :]
        _group_attention(q_ref, pl.ds(r0, blk), ((ctx, None), (band, biases)), sinks, ones_sel, o_ref)


def _attention(q, kvx, kvx_ctx, sink, *, batch, seq, n_ctx, tq, cast=()):
    m, q_w = q.shape
    blk = ATT_BLOCK
    hw = q_w // N_KV_HEADS
    kw = 4 * LANES
    nb = seq // blk
    tpb = tq // blk

    nt = seq // tq
    c_in, c_out, c_shape = _cast_plan(cast, batch * nt * N_KV_HEADS,
                                      lambda b, i, h: (b * nt + i) * N_KV_HEADS + h)
    out = pl.pallas_call(
        functools.partial(_attn_kernel, tq=tq, nblk_seq=nb, ncast=len(cast)),
        grid=(batch, nt, N_KV_HEADS),
        in_specs=[
            pl.BlockSpec(memory_space=pltpu.SMEM),
            pl.BlockSpec((tq, hw), lambda b, i, h: (b * nt + i, h)),
            pl.BlockSpec((blk, kw), lambda b, i, h: (b * nb + jnp.maximum(i * tpb - 1, 0), h)),
            pl.BlockSpec((tq, kw), lambda b, i, h: (b * nt + i, h)),
            pl.BlockSpec((blk, kw), lambda b, i, h: (b * nb + jnp.minimum((i + 1) * tpb, nb - 1), h)),
            pl.BlockSpec((n_ctx, kw), lambda b, i, h: (b, h)),
        ] + c_in,
        out_specs=[pl.BlockSpec((tq, hw), lambda b, i, h: (b * nt + i, h))] + c_out,
        out_shape=[jax.ShapeDtypeStruct((m, q_w), BF16)] + c_shape,
        scratch_shapes=[pltpu.VMEM((tq + 2 * blk, kw), BF16)],
        compiler_params=_params("arbitrary", "arbitrary", "arbitrary"),
        name="band_attention",
    )(sink, q, kvx, kvx, kvx, kvx_ctx, *(w for w, _ in cast))
    return out[0], tuple(out[1:])


def _ctx_attn_kernel(sink_ref, q_ref, ctx_ref, o_ref, *, n_ctx):
    h = pl.program_id(1)
    groups = N_HEADS // N_KV_HEADS
    ctx = ctx_ref[...]
    sinks = [sink_ref[h * groups + g] * LOG2E for g in range(groups)]
    ones_sel = _ones_selector(n_ctx)
    for r0 in range(0, n_ctx, ATT_BLOCK):
        _group_attention(q_ref, pl.ds(r0, ATT_BLOCK), ((ctx, None),), sinks, ones_sel, o_ref)


def _ctx_attention(q, kvx_ctx, sink, *, batch, n_ctx):
    m, q_w = q.shape
    hw = q_w // N_KV_HEADS
    kw = 4 * LANES
    return pl.pallas_call(
        functools.partial(_ctx_attn_kernel, n_ctx=n_ctx),
        grid=(batch, N_KV_HEADS),
        in_specs=[
            pl.BlockSpec(memory_space=pltpu.SMEM),
            pl.BlockSpec((n_ctx, hw), lambda b, h: (b, h)),
            pl.BlockSpec((n_ctx, kw), lambda b, h: (b, h)),
        ],
        out_specs=pl.BlockSpec((n_ctx, hw), lambda b, h: (b, h)),
        out_shape=jax.ShapeDtypeStruct((m, q_w), BF16),
        compiler_params=_params("parallel", "arbitrary"),
        name="ctx_attention",
    )(sink, q, kvx_ctx)


def _merge_kernel(a_ref, y_ref, h_ref, wa_ref, wc_ref, wga_ref, wgc_ref, *rest):
    m_ref = rest[-1]
    if len(rest) > 1:
        c_ref, aw_ref, ab_ref, m_ref, mod_ref = rest
        _ada_kernel(c_ref, aw_ref, ab_ref, mod_ref)
    h = h_ref[...]
    m = (_sigmoid(_dot(h, wga_ref[...])) * _dot(a_ref[...], wa_ref[...])
         + _sigmoid(_dot(h, wgc_ref[...])) * _dot(y_ref[...], wc_ref[...]))
    m_ref[...] = m.astype(BF16)


def _merge(attn, y, h, w_in_bf, wa_bf, wc_bf, *, tm, tn, ada_next=None):
    m, q_w = attn.shape
    d_conv = y.shape[1]
    d_in = h.shape[1]
    d = wa_bf.shape[-1]
    nn = d // tn
    gate_off = (q_w + 2 * N_KV_HEADS * HEAD_DIM + 3 * d_conv) // tn
    extra_in, extra_args, extra_out, extra_shape = [], [], [], []
    if ada_next is not None:
        cc, ada_w, ada_b3, nxt = ada_next
        n6 = ada_w.shape[-1]
        nblk = (m // tm) * nn
        while n6 % nblk or (n6 // nblk) % LANES:
            assert nblk % 2 == 0, (n6, nblk)
            nblk //= 2
        per, wcol = (m // tm) * nn // nblk, n6 // nblk
        extra_in = [pl.BlockSpec(cc.shape, lambda i, j: (0, 0)),
                    pl.BlockSpec((None, d_in, wcol), lambda i, j: (nxt, 0, (i * nn + j) // per)),
                    pl.BlockSpec((None, 1, wcol), lambda i, j: (nxt, 0, (i * nn + j) // per))]
        extra_args = [cc, ada_w, ada_b3]
        extra_out = [pl.BlockSpec((MOD_ROWS, wcol), lambda i, j: (0, (i * nn + j) // per))]
        extra_shape = [jax.ShapeDtypeStruct((MOD_ROWS, n6), F32)]
    out = pl.pallas_call(
        _merge_kernel,
        grid=(m // tm, nn),
        in_specs=[
            pl.BlockSpec((tm, q_w), lambda i, j: (i, 0)),
            pl.BlockSpec((tm, d_conv), lambda i, j: (i, 0)),
            pl.BlockSpec((tm, d_in), lambda i, j: (i, 0)),
            pl.BlockSpec((q_w, tn), lambda i, j: (0, j)),
            pl.BlockSpec((d_conv, tn), lambda i, j: (0, j)),
            pl.BlockSpec((d_in, tn), lambda i, j: (0, gate_off + j)),
            pl.BlockSpec((d_in, tn), lambda i, j: (0, gate_off + nn + j)),
        ] + extra_in,
        out_specs=[pl.BlockSpec((tm, tn), lambda i, j: (i, j))] + extra_out,
        out_shape=[jax.ShapeDtypeStruct((m, d), BF16)] + extra_shape,
        compiler_params=_params("arbitrary", "arbitrary"),
        name="merge",
    )(attn, y, h, wa_bf, wc_bf, w_in_bf, w_in_bf, *extra_args)
    return out[0], (out[1] if ada_next is not None else None)


def _outproj_kernel(m_ref, w_ref, x_ref, g_ref, o_ref):
    rows = o_ref.shape[0]
    rc = _tile(rows, OUT_ROW_CHUNK)
    for r0 in range(0, rows, rc):
        rs = slice(r0, r0 + rc)
        o_ref[rs] = x_ref[rs] + g_ref[...] * _dot(m_ref[rs], w_ref[...])


def _outproj(mm, xr, mods, wo_bf, layer, *, tm, tn, row_of_tile):
    m, d = xr.shape
    return pl.pallas_call(
        _outproj_kernel,
        grid=(m // tm, d // tn),
        in_specs=[
            pl.BlockSpec((tm, d), lambda i, j: (i, 0)),
            pl.BlockSpec((d, tn), lambda i, j: (0, j)),
            pl.BlockSpec((tm, tn), lambda i, j: (i, j)),
            pl.BlockSpec((None, None, None, 1, tn), lambda i, j: (0, row_of_tile(i), 2, 0, j)),
        ],
        out_specs=pl.BlockSpec((tm, tn), lambda i, j: (i, j)),
        out_shape=jax.ShapeDtypeStruct((m, d), F32),
        compiler_params=_params("parallel", "arbitrary"),
        name="outproj",
    )(mm, wo_bf, xr, mods)


def _ffn_kernel(*refs, final, nf):
    if final:
        x_ref, g_ref, sh_ref, sc_ref, gt_ref, wg_ref, wu_ref, wo_ref, fg_ref, o_ref, h_scr = refs
    else:
        x_ref, g_ref, sh_ref, sc_ref, gt_ref, wg_ref, wu_ref, wo_ref, o_ref, h_scr = refs
    f = pl.program_id(1)
    rows, d = o_ref.shape
    cw = _tile(d, FFN_OUT_CHUNK)

    def step(first, last):
        rc = _tile(rows, FFN_ROW_CHUNK) if first or last else rows
        for r0 in range(0, rows, rc):
            rs = slice(r0, r0 + rc)
            if first:
                h = _norm_mod(x_ref[rs], g_ref[...] * (1.0 + sc_ref[...]), sh_ref[...]).astype(BF16)
                h_scr[rs] = h
            else:
                h = h_scr[rs]
            gate = _dot(h, wg_ref[...])
            act = (gate * _sigmoid(gate) * _dot(h, wu_ref[...])).astype(BF16)
            for c0 in range(0, d, cw):
                part = _dot(act, wo_ref[:, c0:c0 + cw])
                if first:
                    o_ref[rs, c0:c0 + cw] = part
                else:
                    o_ref[rs, c0:c0 + cw] += part
            if last:
                xn = x_ref[rs] + gt_ref[...] * o_ref[rs]
                if final:
                    xn = xn * lax.rsqrt(jnp.mean(xn * xn, axis=-1, keepdims=True) + EPS) * fg_ref[...]
                o_ref[rs] = xn

    if nf == 1:
        step(True, True)
    else:
        pl.when(f == 0)(lambda: step(True, False))
        pl.when(jnp.logical_and(f > 0, f < nf - 1))(lambda: step(False, False))
        pl.when(f == nf - 1)(lambda: step(False, True))


def _ffn(xr, norm_g, mods, wi_bf, wo_bf, layer, *, tm, tf, row_of_tile, final_g):
    m, d = xr.shape
    d_ff = wo_bf.shape[0]
    nf = d_ff // tf
    final = final_g is not None
    in_specs = [
        pl.BlockSpec((tm, d), lambda i, f: (i, 0)),
        pl.BlockSpec((None, 1, d), lambda i, f: (layer, 0, 0)),
        pl.BlockSpec((None, None, None, 1, d), _mod_spec(3, row_of_tile)),
        pl.BlockSpec((None, None, None, 1, d), _mod_spec(4, row_of_tile)),
        pl.BlockSpec((None, None, None, 1, d), _mod_spec(5, row_of_tile)),
        pl.BlockSpec((d, tf), lambda i, f: (0, f)),
        pl.BlockSpec((d, tf), lambda i, f: (0, nf + f)),
        pl.BlockSpec((tf, d), lambda i, f: (f, 0)),
    ]
    args = [xr, norm_g, mods, mods, mods, wi_bf, wi_bf, wo_bf]
    if final:
        in_specs.append(pl.BlockSpec((1, d), lambda i, f: (0, 0)))
        args.append(final_g)
    return pl.pallas_call(
        functools.partial(_ffn_kernel, final=final, nf=nf),
        grid=(m // tm, nf),
        in_specs=in_specs,
        out_specs=pl.BlockSpec((tm, d), lambda i, f: (i, 0)),
        out_shape=jax.ShapeDtypeStruct((m, d), F32),
        scratch_shapes=[pltpu.VMEM((tm, d), BF16)],
        compiler_params=_params("parallel", "arbitrary"),
        name="ffn_final" if final else "ffn",
    )(*args)


def _rope_tables(seq):
    quarter = HEAD_DIM // 4
    t = np.arange(seq)
    pos = np.stack([t // GRID_W, t % GRID_W], axis=1).astype(np.float32)
    d = np.arange(LANES) % HEAD_DIM
    axis = d // (HEAD_DIM // 2)
    first_half = (d // quarter) % 2 == 0
    freqs = np.float32(ROPE_THETA) ** (-np.arange(quarter, dtype=np.float32) / np.float32(quarter))
    ang = (pos[:, axis] * freqs[d % quarter][None, :]).astype(np.float32)
    cos, sin = np.cos(ang), np.sin(ang)
    fh = first_half[None, :]
    zero = np.float32(0.0)
    return tuple(jnp.asarray(t, F32) for t in (cos, np.where(fh, -sin, zero), np.where(fh, zero, sin)))


def _tile(n, pref):
    return pref if n % pref == 0 else n


def kernel(x, c, ctx, c_ctx, ada_w, ada_b, norm1_g, norm2_g, w_in, conv_w, conv_b, sink,
           w_attn_out, w_conv_out, w_o, w_ffn_in, w_ffn_out, final_g):
    batch, seq, d = x.shape
    n_ctx = ctx.shape[1]
    depth = ada_w.shape[0]
    d_ff = w_ffn_out.shape[1]
    assert batch + 1 <= MOD_ROWS and seq % ATT_BLOCK == 0 and n_ctx % ATT_BLOCK == 0
    assert WINDOW == ATT_BLOCK, "the band masks assume the window spans exactly one query block each side"
    assert x.shape[-1] == conv_w.shape[-1], "w_in column offsets assume D_CONV == D_MODEL"

    cc = jnp.zeros((MOD_ROWS, d), F32).at[:batch].set(c).at[batch].set(c_ctx)
    ada_b3 = ada_b.reshape(depth, 1, -1)
    mods = _ada(cc, ada_w, ada_b3, 1).reshape(1, MOD_ROWS, 6, 1, d)

    n1 = norm1_g.reshape(depth, 1, d)
    n2 = norm2_g.reshape(depth, 1, d)
    conv_b3 = conv_b.reshape(depth, 1, -1)
    fg = final_g.reshape(1, d)
    rope_tabs = _rope_tables(seq)

    tm = _tile(seq, 1024)
    tm_out = _tile(seq, 1024)
    tn = 2 * N_KV_HEADS * HEAD_DIM
    tf = _tile(d_ff, tn)
    tq = _tile(seq, 2048)
    m_ctx = batch * n_ctx
    lat_row = lambda t: (lambda i: i // (seq // t))
    ctx_row = lambda i: batch

    late = lambda l: tuple((w, l) for w in (w_attn_out, w_conv_out, w_o, w_ffn_in, w_ffn_out))
    wts = {0: (w_in[0].astype(BF16),)}

    xl = x.reshape(batch * seq, d)
    xc = ctx.reshape(m_ctx, d)
    for l in range(depth):
        last = l == depth - 1
        w_in_bf = wts[l][0]
        q_c, kvx_c, h_c = _inproj(xc, n1, mods, w_in_bf, l, tm=m_ctx, row_of_tile=ctx_row, rope_tabs=None,
                                  seq=n_ctx, with_q=not last)
        q, kvx, h = _inproj(xl, n1, mods, w_in_bf, l, tm=tm, row_of_tile=lat_row(tm),
                            rope_tabs=rope_tabs, seq=seq, with_q=True)
        y, cast0 = _conv_branch(h, w_in_bf, conv_w, conv_b3, l, tm=tm, tn=tn, seq=seq,
                                cast=late(0) if l == 0 else ())
        if l == 0:
            wts[0] = wts[0] + cast0
        attn, cast1 = _attention(q, kvx, kvx_c, sink[l], batch=batch, seq=seq, n_ctx=n_ctx, tq=tq,
                                 cast=() if last else ((w_in, l + 1),) + late(l + 1))
        if not last:
            wts[l + 1] = cast1
        _, wa_bf, wc_bf, wo_bf, wfi_bf, wfo_bf = wts[l]
        mm, mods_next = _merge(attn, y, h, w_in_bf, wa_bf, wc_bf, tm=tm, tn=tn,
                               ada_next=None if last else (cc, ada_w, ada_b3, l + 1))
        xl = _outproj(mm, xl, mods, wo_bf, l, tm=tm_out, tn=d, row_of_tile=lat_row(tm_out))

        if not last:
            y_c, _ = _conv_branch(h_c, w_in_bf, conv_w, conv_b3, l, tm=n_ctx, tn=tn, seq=n_ctx)
            attn_c = _ctx_attention(q_c, kvx_c, sink[l], batch=batch, n_ctx=n_ctx)
            mm_c, _ = _merge(attn_c, y_c, h_c, w_in_bf, wa_bf, wc_bf, tm=m_ctx, tn=tn)
            xc = _outproj(mm_c, xc, mods, wo_bf, l, tm=m_ctx, tn=d, row_of_tile=ctx_row)
            xc = _ffn(xc, n2, mods, wfi_bf, wfo_bf, l, tm=m_ctx, tf=tf, row_of_tile=ctx_row, final_g=None)

        xl = _ffn(xl, n2, mods, wfi_bf, wfo_bf, l, tm=tm, tf=tf, row_of_tile=lat_row(tm),
                  final_g=fg if last else None)
        if not last:
            mods = mods_next.reshape(1, MOD_ROWS, 6, 1, d)
    return xl.reshape(batch, seq, d)
```

```python
import functools

import jax
import jax.numpy as jnp
import numpy as np
from jax import lax
from jax.experimental import pallas as pl
from jax.experimental.pallas import tpu as pltpu

GRID_W = 64
N_HEADS = 32
N_KV_HEADS = 4
HEAD_DIM = 64
WINDOW = 128
ROPE_THETA = 10000.0
EPS = 1e-6
NEG = -1e30
LOG2E = 1.4426950408889634
Q_SCALE = HEAD_DIM ** -0.5 * LOG2E

LANES = 128
BF16_ROWS = 16
ROW_CHUNK = 256
V7X_VMEM_BYTES = 64 * 1024 * 1024
VMEM_LIMIT_BYTES = V7X_VMEM_BYTES * 15 // 16
ATT_BLOCK = 128
MOD_ROWS = 8
FFN_OUT_CHUNK = 512
OUT_ROW_CHUNK = 512
FFN_ROW_CHUNK = 512

F32 = jnp.float32
BF16 = jnp.bfloat16


def _params(*sem):
    return pltpu.CompilerParams(dimension_semantics=sem, vmem_limit_bytes=VMEM_LIMIT_BYTES)


def _sigmoid(v):
    return 1.0 / (1.0 + jnp.exp(-v))


def _norm_mod(x, gain, shift):
    y = x * lax.rsqrt(jnp.mean(x * x, axis=-1, keepdims=True) + EPS)
    return y * gain + shift


def _dot(a, b):
    return jnp.dot(a, b, preferred_element_type=F32)


def _dot_nt(a, b):
    return lax.dot_general(a, b, (((1,), (1,)), ((), ())), preferred_element_type=F32)


def _ada_kernel(c_ref, w_ref, b_ref, o_ref):
    c = c_ref[...]
    s = c * _sigmoid(c)
    o_ref[...] = _dot(s.astype(BF16), w_ref[...].astype(BF16)) + b_ref[...]


def _ada(cc, ada_w, ada_b3, n_layers):
    _, d, n = ada_w.shape
    tn = _tile(n, 1024)
    return pl.pallas_call(
        _ada_kernel,
        grid=(n_layers, n // tn),
        in_specs=[
            pl.BlockSpec((MOD_ROWS, d), lambda l, j: (0, 0)),
            pl.BlockSpec((None, d, tn), lambda l, j: (l, 0, j)),
            pl.BlockSpec((None, 1, tn), lambda l, j: (l, 0, j)),
        ],
        out_specs=pl.BlockSpec((None, MOD_ROWS, tn), lambda l, j: (l, 0, j)),
        out_shape=jax.ShapeDtypeStruct((n_layers, MOD_ROWS, n), F32),
        compiler_params=_params("parallel", "parallel"),
        name="ada_mod",
    )(cc, ada_w, ada_b3)


def _mod_spec(which, row_of_tile):
    def index(i, j):
        return (0, row_of_tile(i), which, 0, 0)
    return index


def _rope(v, cos, sin_a, sin_b):
    return v * cos + pltpu.roll(v, LANES - 16, 1) * sin_a + pltpu.roll(v, 16, 1) * sin_b


def _inproj_kernel(*refs, use_rope, with_q):
    refs = list(refs)
    x_ref, g_ref, sh_ref, sc_ref, wkv_ref = refs[:5]
    del refs[:5]
    wq_ref = refs.pop(0) if with_q else None
    if use_rope:
        cos_ref, sa_ref, sb_ref = refs[:3]
        del refs[:3]
    q_ref = refs.pop(0) if with_q else None
    kvx_ref, h_ref = refs

    rows = h_ref.shape[0]
    rc = _tile(rows, ROW_CHUNK)
    gain, sh = g_ref[...] * (1.0 + sc_ref[...]), sh_ref[...]
    lo_mask = lax.broadcasted_iota(jnp.int32, (rc, LANES), 1) < HEAD_DIM
    for r0 in range(0, rows, rc):
        rs = slice(r0, r0 + rc)
        rope = (lambda v: _rope(v, cos_ref[rs], sa_ref[rs], sb_ref[rs])) if use_rope else (lambda v: v)
        h = _norm_mod(x_ref[rs], gain, sh).astype(BF16)
        h_ref[rs] = h
        acc = _dot(h, wkv_ref[...])
        kvw = acc.shape[1] // 2
        for part in range(2):
            for pr in range(kvw // LANES):
                v = acc[:, part * kvw + pr * LANES: part * kvw + (pr + 1) * LANES]
                if part == 0:
                    v = rope(v)
                r = pltpu.roll(v, HEAD_DIM, 1)
                tiles = ((jnp.where(lo_mask, v, 0.0), jnp.where(lo_mask, 0.0, r)),
                         (jnp.where(lo_mask, r, 0.0), jnp.where(lo_mask, 0.0, v)))
                for odd in range(2):
                    head = 2 * pr + odd
                    for hi in range(2):
                        col = head * 4 * LANES + (2 * part + hi) * LANES
                        kvx_ref[rs, col:col + LANES] = tiles[odd][hi].astype(BF16)
        if with_q:
            acc = _dot(h, wq_ref[...])
            for cg in range(acc.shape[1] // LANES):
                v = rope(acc[:, cg * LANES:(cg + 1) * LANES])
                q_ref[rs, cg * LANES:(cg + 1) * LANES] = (v * Q_SCALE).astype(BF16)


def _inproj(xr, norm_g, mods, w_bf, layer, *, tm, row_of_tile, rope_tabs, seq, with_q):
    m, d = xr.shape
    q_w = N_HEADS * HEAD_DIM
    kv_w = N_KV_HEADS * HEAD_DIM
    use_rope = rope_tabs is not None
    kvx_w = N_KV_HEADS * 4 * LANES

    in_specs = [
        pl.BlockSpec((tm, d), lambda i: (i, 0)),
        pl.BlockSpec((None, 1, d), lambda i: (layer, 0, 0)),
        pl.BlockSpec((None, None, None, 1, d), lambda i: (0, row_of_tile(i), 0, 0, 0)),
        pl.BlockSpec((None, None, None, 1, d), lambda i: (0, row_of_tile(i), 1, 0, 0)),
        pl.BlockSpec((d, 2 * kv_w), lambda i: (0, q_w // (2 * kv_w))),
    ]
    args = [xr, norm_g, mods, mods, w_bf]
    if with_q:
        in_specs.append(pl.BlockSpec((d, q_w), lambda i: (0, 0)))
        args.append(w_bf)
    if use_rope:
        tps = seq // tm
        for t in rope_tabs:
            in_specs.append(pl.BlockSpec((tm, LANES), lambda i: (i % tps, 0)))
            args.append(t)

    out_specs = [pl.BlockSpec((tm, kvx_w), lambda i: (i, 0)),
                 pl.BlockSpec((tm, d), lambda i: (i, 0))]
    out_shape = [jax.ShapeDtypeStruct((m, kvx_w), BF16), jax.ShapeDtypeStruct((m, d), BF16)]
    if with_q:
        out_specs.insert(0, pl.BlockSpec((tm, q_w), lambda i: (i, 0)))
        out_shape.insert(0, jax.ShapeDtypeStruct((m, q_w), BF16))
    out = pl.pallas_call(
        functools.partial(_inproj_kernel, use_rope=use_rope, with_q=with_q),
        grid=(m // tm,),
        in_specs=in_specs,
        out_specs=out_specs,
        out_shape=out_shape,
        compiler_params=_params("parallel"),
        name="inproj" if with_q else "inproj_kv",
    )(*args)
    return out if with_q else (None,) + tuple(out)


def _cast_plan(weights, n_steps, step_of):
    in_specs, out_specs, out_shape = [], [], []
    for w, layer in weights:
        _, rows, cols = w.shape
        nblk = n_steps
        while rows % (nblk * BF16_ROWS):
            assert nblk % 2 == 0, (rows, n_steps)
            nblk //= 2
        per = n_steps // nblk

        def src_index(*g, per=per, layer=layer):
            return (layer, step_of(*g) // per, 0)

        def dst_index(*g, per=per):
            return (step_of(*g) // per, 0)
        in_specs.append(pl.BlockSpec((None, rows // nblk, cols), src_index))
        out_specs.append(pl.BlockSpec((rows // nblk, cols), dst_index))
        out_shape.append(jax.ShapeDtypeStruct((rows, cols), BF16))
    return in_specs, out_specs, out_shape


def _run_casts(src_refs, dst_refs):
    for src, dst in zip(src_refs, dst_refs):
        dst[...] = src[...].astype(BF16)


def _conv_kernel(*refs, tm, seq, ncast, nchunk, resident):
    hp_ref, h_ref, hn_ref = refs[:3]
    w_refs = refs[3:3 + 3 * nchunk]
    n_in = 5 + 3 * nchunk
    cw_ref, cb_ref = refs[n_in - 2:n_in]
    y_ref, h_scr = refs[n_in + ncast], refs[-1]
    _run_casts(refs[n_in:n_in + ncast], refs[n_in + ncast + 1:-1])
    i = pl.program_id(0)
    halo = BF16_ROWS

    def fill_halo():
        h_scr[0:halo] = hp_ref[...]
        h_scr[halo:halo + tm] = h_ref[...]
        h_scr[halo + tm:] = hn_ref[...]

    if resident:
        fill_halo()
    else:
        pl.when(pl.program_id(1) == 0)(fill_halo)

    hx = h_scr[...]
    tn = w_refs[0].shape[1]
    rows = tm + 2 * halo
    row = lax.broadcasted_iota(jnp.int32, (tm, 1), 0)
    at_start = jnp.logical_and(row == 0, (i * tm) % seq == 0)
    at_end = jnp.logical_and(row == tm - 1, ((i + 1) * tm) % seq == 0)
    for c in range(nchunk):
        cs = slice(c * tn, (c + 1) * tn)
        u = _dot(hx, w_refs[nchunk + c][...]) * _dot(hx, w_refs[2 * nchunk + c][...])
        u_up = jnp.where(at_start, 0.0, pltpu.roll(u, 1, 0)[halo:halo + tm])
        u_dn = jnp.where(at_end, 0.0, pltpu.roll(u, rows - 1, 0)[halo:halo + tm])
        w = cw_ref[:, cs]
        conv = u_up * w[0:1] + u[halo:halo + tm] * w[1:2] + u_dn * w[2:3] + cb_ref[:, cs]
        y_ref[:, cs] = (_dot(h_ref[...], w_refs[c][...]) * conv).astype(BF16)


def _conv_branch(h, w_bf, conv_w, conv_b, layer, *, tm, tn, seq, cast=(), gate_cast=None):
    m, d = h.shape
    d_conv = conv_w.shape[-1]
    halo = BF16_ROWS
    off = (N_HEADS * HEAD_DIM + 2 * N_KV_HEADS * HEAD_DIM) // tn
    nc = d_conv // tn
    hb = tm // halo
    last_hb = m // halo - 1
    resident = not cast and gate_cast is None
    nchunk, yw = (nc, d_conv) if resident else (1, tn)
    col = (lambda g: 0) if resident else (lambda g: g[1])

    w_specs = [pl.BlockSpec((d, tn), lambda *g, k=k, c=c: (0, off + k * nc + c + col(g)))
               for k in range(3) for c in range(nchunk)]
    grid = (m // tm,) if resident else (m // tm, nc)
    c_in, c_out, c_shape = _cast_plan(cast, (m // tm) * nc, lambda i, j: i * nc + j)
    cast_args = [w for w, _ in cast]
    if gate_cast is not None:
        gw, gl, col0, ncols = gate_cast
        n_steps, ncb = (m // tm) * nc, ncols // tn
        rb = gw.shape[1] * ncb // n_steps
        assert n_steps % ncb == 0 and rb % BF16_ROWS == 0 and col0 % tn == 0 and ncols % tn == 0
        c_in.append(pl.BlockSpec((None, rb, tn),
                                 lambda i, j: (gl, (i * nc + j) // ncb, col0 // tn + (i * nc + j) % ncb)))
        c_out.append(pl.BlockSpec((rb, tn), lambda i, j: ((i * nc + j) // ncb, (i * nc + j) % ncb)))
        c_shape.append(jax.ShapeDtypeStruct((gw.shape[1], ncols), BF16))
        cast_args.append(gw)
    out = pl.pallas_call(
        functools.partial(_conv_kernel, tm=tm, seq=seq, ncast=len(cast_args), nchunk=nchunk, resident=resident),
        grid=grid,
        in_specs=[
            pl.BlockSpec((halo, d), lambda *g: (jnp.maximum(g[0] * hb - 1, 0), 0)),
            pl.BlockSpec((tm, d), lambda *g: (g[0], 0)),
            pl.BlockSpec((halo, d), lambda *g: (jnp.minimum((g[0] + 1) * hb, last_hb), 0)),
        ] + w_specs + [
            pl.BlockSpec((None, 3, yw), lambda *g: (layer, 0, col(g))),
            pl.BlockSpec((None, 1, yw), lambda *g: (layer, 0, col(g))),
        ] + c_in,
        out_specs=[pl.BlockSpec((tm, yw), lambda *g: (g[0], col(g)))] + c_out,
        out_shape=[jax.ShapeDtypeStruct((m, d_conv), BF16)] + c_shape,
        scratch_shapes=[pltpu.VMEM((tm + 2 * halo, d), BF16)],
        compiler_params=_params(*(["arbitrary"] * len(grid))),
        name="conv_branch",
    )(h, h, h, *([w_bf] * (3 * nchunk)), conv_w, conv_b, *cast_args)
    return out[0], tuple(out[1:])


def _group_attention(q_ref, rows, pieces, sinks, ones_sel, o_ref):
    blk = ATT_BLOCK
    npair = len(sinks) // 2
    qs = jnp.concatenate([q_ref[rows, jp * LANES:(jp + 1) * LANES] for jp in range(npair)], axis=0)
    probs, vals, sink_p = [], [], []
    for par in range(2):
        chunks = []
        for kvx, biases in pieces:
            s = _dot_nt(qs, kvx[:, par * LANES:(par + 1) * LANES])
            for ci in range(s.shape[1] // LANES):
                ch = s[:, ci * LANES:(ci + 1) * LANES]
                if biases is not None and biases[ci] is not None:
                    ch = ch + jnp.concatenate([biases[ci]] * npair, axis=0)
                chunks.append(ch)
            vals.append(kvx[:, (2 + par) * LANES:(3 + par) * LANES])
        sink = jnp.concatenate([jnp.full((blk, LANES), sinks[2 * jp + par], F32) for jp in range(npair)], axis=0)
        mx = jnp.maximum(jnp.max(functools.reduce(jnp.maximum, chunks), axis=-1, keepdims=True), sink)
        probs.extend(jnp.exp2(ch - mx).astype(BF16) for ch in chunks)
        sink_p.append(jnp.exp2(sink - mx))
    rhs = jnp.concatenate([jnp.concatenate(vals, axis=0), ones_sel], axis=1)
    out = _dot(jnp.concatenate(probs, axis=1), rhs)
    lane = lax.broadcasted_iota(jnp.int32, (npair * blk, LANES), 1)
    den = out[:, LANES:] + jnp.where(lane < HEAD_DIM, sink_p[0], sink_p[1])
    res = out[:, :LANES] / den
    for jp in range(npair):
        o_ref[rows, jp * LANES:(jp + 1) * LANES] = res[jp * blk:(jp + 1) * blk].astype(BF16)


def _ones_selector(n):
    row = lax.broadcasted_iota(jnp.int32, (2 * n, LANES), 0)
    lane = lax.broadcasted_iota(jnp.int32, (2 * n, LANES), 1)
    return jnp.where((row < n) == (lane < HEAD_DIM), 1.0, 0.0).astype(BF16)


def _attn_kernel(*refs, tq, nblk_seq, ncast):
    sink_ref, q_ref, kvp_ref, kvc_ref, kvn_ref, ctx_ref = refs[:6]
    o_ref, band_scr = refs[6 + ncast], refs[-1]
    _run_casts(refs[6:6 + ncast], refs[7 + ncast:-1])
    i = pl.program_id(1)
    h = pl.program_id(2)
    blk = ATT_BLOCK
    groups = N_HEADS // N_KV_HEADS
    band_scr[0:blk] = kvp_ref[...]
    band_scr[blk:blk + tq] = kvc_ref[...]
    band_scr[blk + tq:] = kvn_ref[...]
    ctx = ctx_ref[...]
    sinks = [sink_ref[h * groups + g] * LOG2E for g in range(groups)]
    r = lax.broadcasted_iota(jnp.int32, (blk, blk), 0)
    c = lax.broadcasted_iota(jnp.int32, (blk, blk), 1)
    bias_prev = jnp.where(c >= r, 0.0, NEG)
    bias_next = jnp.where(c <= r, 0.0, NEG)
    ones_sel = _ones_selector(ctx.shape[0] + 3 * blk)

    for sb in range(tq // blk):
        r0 = sb * blk
        gblk = i * (tq // blk) + sb
        pen_first = jnp.where(gblk == 0, NEG, 0.0)
        pen_last = jnp.where(gblk == nblk_seq - 1, NEG, 0.0)
        biases = (bias_prev + pen_first, None, bias_next + pen_last)
        band = band_scr[r0:r0 + 3 * blk, :]
        _group_attention(q_ref, pl.ds(r0, blk), ((ctx, None), (band, biases)), sinks, ones_sel, o_ref)


def _attention(q, kvx, kvx_ctx, sink, *, batch, seq, n_ctx, tq, cast=()):
    m, q_w = q.shape
    blk = ATT_BLOCK
    hw = q_w // N_KV_HEADS
    kw = 4 * LANES
    nb = seq // blk
    tpb = tq // blk

    nt = seq // tq
    c_in, c_out, c_shape = _cast_plan(cast, batch * nt * N_KV_HEADS,
                                      lambda b, i, h: (b * nt + i) * N_KV_HEADS + h)
    out = pl.pallas_call(
        functools.partial(_attn_kernel, tq=tq, nblk_seq=nb, ncast=len(cast)),
        grid=(batch, nt, N_KV_HEADS),
        in_specs=[
            pl.BlockSpec(memory_space=pltpu.SMEM),
            pl.BlockSpec((tq, hw), lambda b, i, h: (b * nt + i, h)),
            pl.BlockSpec((blk, kw), lambda b, i, h: (b * nb + jnp.maximum(i * tpb - 1, 0), h)),
            pl.BlockSpec((tq, kw), lambda b, i, h: (b * nt + i, h)),
            pl.BlockSpec((blk, kw), lambda b, i, h: (b * nb + jnp.minimum((i + 1) * tpb, nb - 1), h)),
            pl.BlockSpec((n_ctx, kw), lambda b, i, h: (b, h)),
        ] + c_in,
        out_specs=[pl.BlockSpec((tq, hw), lambda b, i, h: (b * nt + i, h))] + c_out,
        out_shape=[jax.ShapeDtypeStruct((m, q_w), BF16)] + c_shape,
        scratch_shapes=[pltpu.VMEM((tq + 2 * blk, kw), BF16)],
        compiler_params=_params("arbitrary", "arbitrary", "arbitrary"),
        name="band_attention",
    )(sink, q, kvx, kvx, kvx, kvx_ctx, *(w for w, _ in cast))
    return out[0], tuple(out[1:])


def _ctx_attn_kernel(sink_ref, q_ref, ctx_ref, o_ref, *, n_ctx):
    h = pl.program_id(1)
    groups = N_HEADS // N_KV_HEADS
    ctx = ctx_ref[...]
    sinks = [sink_ref[h * groups + g] * LOG2E for g in range(groups)]
    ones_sel = _ones_selector(n_ctx)
    for r0 in range(0, n_ctx, ATT_BLOCK):
        _group_attention(q_ref, pl.ds(r0, ATT_BLOCK), ((ctx, None),), sinks, ones_sel, o_ref)


def _ctx_attention(q, kvx_ctx, sink, *, batch, n_ctx):
    m, q_w = q.shape
    hw = q_w // N_KV_HEADS
    kw = 4 * LANES
    return pl.pallas_call(
        functools.partial(_ctx_attn_kernel, n_ctx=n_ctx),
        grid=(batch, N_KV_HEADS),
        in_specs=[
            pl.BlockSpec(memory_space=pltpu.SMEM),
            pl.BlockSpec((n_ctx, hw), lambda b, h: (b, h)),
            pl.BlockSpec((n_ctx, kw), lambda b, h: (b, h)),
        ],
        out_specs=pl.BlockSpec((n_ctx, hw), lambda b, h: (b, h)),
        out_shape=jax.ShapeDtypeStruct((m, q_w), BF16),
        compiler_params=_params("parallel", "arbitrary"),
        name="ctx_attention",
    )(sink, q, kvx_ctx)


def _merge_kernel(a_ref, y_ref, h_ref, wa_ref, wc_ref, wga_ref, wgc_ref, *rest):
    m_ref = rest[-1]
    if len(rest) > 1:
        c_ref, aw_ref, ab_ref, m_ref, mod_ref = rest
        _ada_kernel(c_ref, aw_ref, ab_ref, mod_ref)
    h = h_ref[...]
    m = (_sigmoid(_dot(h, wga_ref[...])) * _dot(a_ref[...], wa_ref[...])
         + _sigmoid(_dot(h, wgc_ref[...])) * _dot(y_ref[...], wc_ref[...]))
    m_ref[...] = m.astype(BF16)


def _merge(attn, y, h, wgate_bf, gate_col, wa_bf, wc_bf, *, tm, tn, ada_next=None):
    m, q_w = attn.shape
    d_conv = y.shape[1]
    d_in = h.shape[1]
    d = wa_bf.shape[-1]
    nn = d // tn
    gate_off = gate_col // tn
    extra_in, extra_args, extra_out, extra_shape = [], [], [], []
    if ada_next is not None:
        cc, ada_w, ada_b3, nxt = ada_next
        n6 = ada_w.shape[-1]
        nblk = (m // tm) * nn
        while n6 % nblk or (n6 // nblk) % LANES:
            assert nblk % 2 == 0, (n6, nblk)
            nblk //= 2
        per, wcol = (m // tm) * nn // nblk, n6 // nblk
        extra_in = [pl.BlockSpec(cc.shape, lambda i, j: (0, 0)),
                    pl.BlockSpec((None, d_in, wcol), lambda i, j: (nxt, 0, (i * nn + j) // per)),
                    pl.BlockSpec((None, 1, wcol), lambda i, j: (nxt, 0, (i * nn + j) // per))]
        extra_args = [cc, ada_w, ada_b3]
        extra_out = [pl.BlockSpec((MOD_ROWS, wcol), lambda i, j: (0, (i * nn + j) // per))]
        extra_shape = [jax.ShapeDtypeStruct((MOD_ROWS, n6), F32)]
    out = pl.pallas_call(
        _merge_kernel,
        grid=(m // tm, nn),
        in_specs=[
            pl.BlockSpec((tm, q_w), lambda i, j: (i, 0)),
            pl.BlockSpec((tm, d_conv), lambda i, j: (i, 0)),
            pl.BlockSpec((tm, d_in), lambda i, j: (i, 0)),
            pl.BlockSpec((q_w, tn), lambda i, j: (0, j)),
            pl.BlockSpec((d_conv, tn), lambda i, j: (0, j)),
            pl.BlockSpec((d_in, tn), lambda i, j: (0, gate_off + j)),
            pl.BlockSpec((d_in, tn), lambda i, j: (0, gate_off + nn + j)),
        ] + extra_in,
        out_specs=[pl.BlockSpec((tm, tn), lambda i, j: (i, j))] + extra_out,
        out_shape=[jax.ShapeDtypeStruct((m, d), BF16)] + extra_shape,
        compiler_params=_params("arbitrary", "arbitrary"),
        name="merge",
    )(attn, y, h, wa_bf, wc_bf, wgate_bf, wgate_bf, *extra_args)
    return out[0], (out[1] if ada_next is not None else None)


def _outproj_kernel(m_ref, w_ref, x_ref, g_ref, o_ref):
    rows = o_ref.shape[0]
    rc = _tile(rows, OUT_ROW_CHUNK)
    for r0 in range(0, rows, rc):
        rs = slice(r0, r0 + rc)
        o_ref[rs] = x_ref[rs] + g_ref[...] * _dot(m_ref[rs], w_ref[...])


def _outproj(mm, xr, mods, wo_bf, layer, *, tm, tn, row_of_tile):
    m, d = xr.shape
    return pl.pallas_call(
        _outproj_kernel,
        grid=(m // tm, d // tn),
        in_specs=[
            pl.BlockSpec((tm, d), lambda i, j: (i, 0)),
            pl.BlockSpec((d, tn), lambda i, j: (0, j)),
            pl.BlockSpec((tm, tn), lambda i, j: (i, j)),
            pl.BlockSpec((None, None, None, 1, tn), lambda i, j: (0, row_of_tile(i), 2, 0, j)),
        ],
        out_specs=pl.BlockSpec((tm, tn), lambda i, j: (i, j)),
        out_shape=jax.ShapeDtypeStruct((m, d), F32),
        compiler_params=_params("parallel", "arbitrary"),
        name="outproj",
    )(mm, wo_bf, xr, mods)


def _ffn_kernel(*refs, final, nf):
    if final:
        x_ref, g_ref, sh_ref, sc_ref, gt_ref, wg_ref, wu_ref, wo_ref, fg_ref, o_ref, h_scr = refs
    else:
        x_ref, g_ref, sh_ref, sc_ref, gt_ref, wg_ref, wu_ref, wo_ref, o_ref, h_scr = refs
    f = pl.program_id(1)
    rows, d = o_ref.shape
    cw = _tile(d, FFN_OUT_CHUNK)

    def step(first, last):
        rc = _tile(rows, FFN_ROW_CHUNK) if first or last else rows
        for r0 in range(0, rows, rc):
            rs = slice(r0, r0 + rc)
            if first:
                h = _norm_mod(x_ref[rs], g_ref[...] * (1.0 + sc_ref[...]), sh_ref[...]).astype(BF16)
                h_scr[rs] = h
            else:
                h = h_scr[rs]
            gate = _dot(h, wg_ref[...])
            act = (gate * _sigmoid(gate) * _dot(h, wu_ref[...])).astype(BF16)
            for c0 in range(0, d, cw):
                part = _dot(act, wo_ref[:, c0:c0 + cw])
                if first:
                    o_ref[rs, c0:c0 + cw] = part
                else:
                    o_ref[rs, c0:c0 + cw] += part
            if last:
                xn = x_ref[rs] + gt_ref[...] * o_ref[rs]
                if final:
                    xn = xn * lax.rsqrt(jnp.mean(xn * xn, axis=-1, keepdims=True) + EPS) * fg_ref[...]
                o_ref[rs] = xn

    if nf == 1:
        step(True, True)
    else:
        pl.when(f == 0)(lambda: step(True, False))
        pl.when(jnp.logical_and(f > 0, f < nf - 1))(lambda: step(False, False))
        pl.when(f == nf - 1)(lambda: step(False, True))


def _ffn(xr, norm_g, mods, wi_bf, wo_bf, layer, *, tm, tf, row_of_tile, final_g):
    m, d = xr.shape
    d_ff = wo_bf.shape[0]
    nf = d_ff // tf
    final = final_g is not None
    in_specs = [
        pl.BlockSpec((tm, d), lambda i, f: (i, 0)),
        pl.BlockSpec((None, 1, d), lambda i, f: (layer, 0, 0)),
        pl.BlockSpec((None, None, None, 1, d), _mod_spec(3, row_of_tile)),
        pl.BlockSpec((None, None, None, 1, d), _mod_spec(4, row_of_tile)),
        pl.BlockSpec((None, None, None, 1, d), _mod_spec(5, row_of_tile)),
        pl.BlockSpec((d, tf), lambda i, f: (0, f)),
        pl.BlockSpec((d, tf), lambda i, f: (0, nf + f)),
        pl.BlockSpec((tf, d), lambda i, f: (f, 0)),
    ]
    args = [xr, norm_g, mods, mods, mods, wi_bf, wi_bf, wo_bf]
    if final:
        in_specs.append(pl.BlockSpec((1, d), lambda i, f: (0, 0)))
        args.append(final_g)
    return pl.pallas_call(
        functools.partial(_ffn_kernel, final=final, nf=nf),
        grid=(m // tm, nf),
        in_specs=in_specs,
        out_specs=pl.BlockSpec((tm, d), lambda i, f: (i, 0)),
        out_shape=jax.ShapeDtypeStruct((m, d), F32),
        scratch_shapes=[pltpu.VMEM((tm, d), BF16)],
        compiler_params=_params("parallel", "arbitrary"),
        name="ffn_final" if final else "ffn",
    )(*args)


def _rope_tables(seq):
    quarter = HEAD_DIM // 4
    t = np.arange(seq)
    pos = np.stack([t // GRID_W, t % GRID_W], axis=1).astype(np.float32)
    d = np.arange(LANES) % HEAD_DIM
    axis = d // (HEAD_DIM // 2)
    first_half = (d // quarter) % 2 == 0
    freqs = np.float32(ROPE_THETA) ** (-np.arange(quarter, dtype=np.float32) / np.float32(quarter))
    ang = (pos[:, axis] * freqs[d % quarter][None, :]).astype(np.float32)
    cos, sin = np.cos(ang), np.sin(ang)
    fh = first_half[None, :]
    zero = np.float32(0.0)
    return tuple(jnp.asarray(t, F32) for t in (cos, np.where(fh, -sin, zero), np.where(fh, zero, sin)))


def _tile(n, pref):
    return pref if n % pref == 0 else n


def kernel(x, c, ctx, c_ctx, ada_w, ada_b, norm1_g, norm2_g, w_in, conv_w, conv_b, sink,
           w_attn_out, w_conv_out, w_o, w_ffn_in, w_ffn_out, final_g):
    batch, seq, d = x.shape
    n_ctx = ctx.shape[1]
    depth = ada_w.shape[0]
    d_ff = w_ffn_out.shape[1]
    assert batch + 1 <= MOD_ROWS and seq % ATT_BLOCK == 0 and n_ctx % ATT_BLOCK == 0
    assert WINDOW == ATT_BLOCK, "the band masks assume the window spans exactly one query block each side"
    assert x.shape[-1] == conv_w.shape[-1], "w_in column offsets assume D_CONV == D_MODEL"

    cc = jnp.zeros((MOD_ROWS, d), F32).at[:batch].set(c).at[batch].set(c_ctx)
    ada_b3 = ada_b.reshape(depth, 1, -1)
    mods = _ada(cc, ada_w, ada_b3, 1).reshape(1, MOD_ROWS, 6, 1, d)

    n1 = norm1_g.reshape(depth, 1, d)
    n2 = norm2_g.reshape(depth, 1, d)
    conv_b3 = conv_b.reshape(depth, 1, -1)
    fg = final_g.reshape(1, d)
    rope_tabs = _rope_tables(seq)

    tm = _tile(seq, 1024)
    tm_out = _tile(seq, 1024)
    tn = 2 * N_KV_HEADS * HEAD_DIM
    tf = _tile(d_ff, tn)
    tq = _tile(seq, 2048)
    m_ctx = batch * n_ctx
    lat_row = lambda t: (lambda i: i // (seq // t))
    ctx_row = lambda i: batch

    late = lambda l: tuple((w, l) for w in (w_attn_out, w_conv_out, w_o, w_ffn_in, w_ffn_out))
    gate_col = N_HEADS * HEAD_DIM + 2 * N_KV_HEADS * HEAD_DIM + 3 * conv_w.shape[-1]
    wts = {0: (w_in[0, :, :gate_col].astype(BF16),)}

    xl = x.reshape(batch * seq, d)
    xc = ctx.reshape(m_ctx, d)
    for l in range(depth):
        last = l == depth - 1
        w_in_bf = wts[l][0]
        q_c, kvx_c, h_c = _inproj(xc, n1, mods, w_in_bf, l, tm=m_ctx, row_of_tile=ctx_row, rope_tabs=None,
                                  seq=n_ctx, with_q=not last)
        q, kvx, h = _inproj(xl, n1, mods, w_in_bf, l, tm=tm, row_of_tile=lat_row(tm),
                            rope_tabs=rope_tabs, seq=seq, with_q=True)
        y, cast0 = _conv_branch(h, w_in_bf, conv_w, conv_b3, l, tm=tm, tn=tn, seq=seq,
                                cast=late(0) if l == 0 else (),
                                gate_cast=(w_in, 0, gate_col, 2 * d) if l == 0 else None)
        if l == 0:
            wts[0] = wts[0] + cast0[:-1]
        wgate_bf, gcol = (cast0[-1], 0) if l == 0 else (w_in_bf, gate_col)
        attn, cast1 = _attention(q, kvx, kvx_c, sink[l], batch=batch, seq=seq, n_ctx=n_ctx, tq=tq,
                                 cast=() if last else ((w_in, l + 1),) + late(l + 1))
        if not last:
            wts[l + 1] = cast1
        _, wa_bf, wc_bf, wo_bf, wfi_bf, wfo_bf = wts[l]
        mm, mods_next = _merge(attn, y, h, wgate_bf, gcol, wa_bf, wc_bf, tm=tm, tn=tn,
                               ada_next=None if last else (cc, ada_w, ada_b3, l + 1))
        xl = _outproj(mm, xl, mods, wo_bf, l, tm=tm_out, tn=d, row_of_tile=lat_row(tm_out))

        if not last:
            y_c, _ = _conv_branch(h_c, w_in_bf, conv_w, conv_b3, l, tm=n_ctx, tn=tn, seq=n_ctx)
            attn_c = _ctx_attention(q_c, kvx_c, sink[l], batch=batch, n_ctx=n_ctx)
            mm_c, _ = _merge(attn_c, y_c, h_c, wgate_bf, gcol, wa_bf, wc_bf, tm=m_ctx, tn=tn)
            xc = _outproj(mm_c, xc, mods, wo_bf, l, tm=m_ctx, tn=d, row_of_tile=ctx_row)
            xc = _ffn(xc, n2, mods, wfi_bf, wfo_bf, l, tm=m_ctx, tf=tf, row_of_tile=ctx_row, final_g=None)

        xl = _ffn(xl, n2, mods, wfi_bf, wfo_bf, l, tm=tm, tf=tf, row_of_tile=lat_row(tm),
                  final_g=fg if last else None)
        if not last:
            mods = mods_next.reshape(1, MOD_ROWS, 6, 1, d)
    return xl.reshape(batch, seq, d)
```
